```python
import math
import jax, jax.numpy as jnp
from jax import lax
import numpy as np

D_MODEL = 1024
BATCH = 4
SEQ = 4096
DEPTH = 2

MIX_WIDTH = D_MODEL
DN_WIDTH = MIX_WIDTH // 2
DN_HEAD_DIM = 128
DN_HEADS = DN_WIDTH // DN_HEAD_DIM
DN_CONV = 5
DN_CHUNK = 64
DN_IN = 4 * DN_WIDTH + 4 * DN_HEADS
S5_WIDTH = MIX_WIDTH - DN_WIDTH
S5_GROUP = 16
S5_GROUPS = S5_WIDTH // S5_GROUP
S5_STATE = 64
AB_IN = DN_IN + S5_WIDTH
RW_WIDTH = MIX_WIDTH // 2
RW_HEAD_DIM = 64
RW_HEADS = RW_WIDTH // RW_HEAD_DIM
RW_DECAY_LORA = 64
RW_AAA_LORA = 64
RW_GATE_LORA = 128
RW_GN_EPS = 64e-5
RW_IN = 3 * RW_WIDTH + 2 * RW_DECAY_LORA + 2 * RW_AAA_LORA + RW_GATE_LORA
DF_WIDTH = MIX_WIDTH - RW_WIDTH
DF_HEAD_DIM = 64
DF_HEADS = DF_WIDTH // (2 * DF_HEAD_DIM)
DF_Q_BLOCK = 128
CD_IN = RW_IN + 3 * DF_WIDTH
N_EXPERTS = 16
EC_CAPACITY_FACTOR = 2
D_EXPERT = 2816
NORM_EPS = 1e-6

kernel_name = 'hybrid_deltanet_s5_rwkv7_diffattn_ec_moe'

F32 = jnp.float32


def rms_norm(x, gain, eps=NORM_EPS):
    xf = x.astype(F32)
    y = xf * lax.rsqrt(jnp.mean(xf * xf, axis=-1, keepdims=True) + eps)
    return (y * gain.astype(F32)).astype(x.dtype)


def l2_normalize(x, eps=1e-6):
    xf = x.astype(F32)
    return xf * lax.rsqrt(jnp.sum(xf * xf, axis=-1, keepdims=True) + eps)


def centred_depthwise_conv(x, w):
    pad = w.shape[0] // 2
    return lax.conv_general_dilated(x, w[:, None, :], window_strides=(1,), padding=[(pad, pad)],
                                    dimension_numbers=('NWC', 'WIO', 'NWC'),
                                    feature_group_count=x.shape[-1])


def chunk_gated_delta_rule(q, k, v, g, beta):
    b, h, s, dk = q.shape
    dv = v.shape[-1]
    cs = DN_CHUNK
    n = s // cs
    q = q * dk ** -0.5
    rs = lambda t: t.reshape(b, h, n, cs, *t.shape[3:])
    q, k, v, g, beta = rs(q), rs(k), rs(v), rs(g), rs(beta)
    gc = jnp.cumsum(g, axis=-1)
    tri_incl = jnp.tril(jnp.ones((cs, cs), bool))
    tri_strict = jnp.tril(jnp.ones((cs, cs), bool), -1)
    decay = jnp.exp(jnp.where(tri_incl, gc[..., :, None] - gc[..., None, :], -jnp.inf))
    k_beta = k * beta[..., None]
    v_beta = v * beta[..., None]
    lower = jnp.where(tri_strict, jnp.einsum('bhncd,bhnsd->bhncs', k_beta, k) * decay, 0.0)
    eye = jnp.eye(cs, dtype=F32)
    t_mat = lax.linalg.triangular_solve(eye + lower, jnp.broadcast_to(eye, lower.shape),
                                        left_side=True, lower=True)
    u = t_mat @ v_beta
    w = t_mat @ (k_beta * jnp.exp(gc)[..., None])
    attn_intra = jnp.einsum('bhncd,bhnsd->bhncs', q, k) * decay
    q_dec = q * jnp.exp(gc)[..., None]
    k_dec = k * jnp.exp(gc[..., -1:] - gc)[..., None]
    g_last = jnp.exp(gc[..., -1])

    def step(state, xs):
        u_c, w_c, qd_c, kd_c, a_c, gl_c = xs
        v_new = u_c - w_c @ state
        o = qd_c @ state + a_c @ v_new
        state = state * gl_c[..., None, None] + jnp.swapaxes(kd_c, -1, -2) @ v_new
        return state, o

    xs = tuple(jnp.moveaxis(t, 2, 0) for t in (u, w, q_dec, k_dec, attn_intra, g_last))
    _, o = lax.scan(step, jnp.zeros((b, h, dk, dv), F32), xs)
    return jnp.moveaxis(o, 0, 2).reshape(b, h, s, dv)


def gated_deltanet(p, conv_w, a_log, dt_bias, norm_w):
    b, s, _ = p.shape
    p = p.astype(F32)
    wd = DN_WIDTH
    qkv = jax.nn.silu(centred_depthwise_conv(p[..., :3 * wd], conv_w.astype(F32)))
    z = p[..., 3 * wd:4 * wd].reshape(b, s, DN_HEADS, DN_HEAD_DIM)
    a_in = p[..., 4 * wd:4 * wd + 2 * DN_HEADS].reshape(b, s, 2, DN_HEADS)
    b_in = p[..., 4 * wd + 2 * DN_HEADS:].reshape(b, s, 2, DN_HEADS)
    hd = lambda t: t.reshape(b, s, DN_HEADS, DN_HEAD_DIM)
    q = jnp.moveaxis(l2_normalize(hd(qkv[..., :wd])), 2, 1)
    k = jnp.moveaxis(l2_normalize(hd(qkv[..., wd:2 * wd])), 2, 1)
    v = jnp.moveaxis(hd(qkv[..., 2 * wd:]), 2, 1)
    g = -jnp.exp(a_log.astype(F32)) * jax.nn.softplus(a_in + dt_bias.astype(F32))
    g = jnp.moveaxis(g, 1, -1)
    beta = jnp.moveaxis(jax.nn.sigmoid(b_in), 1, -1)
    flip = lambda t: jnp.flip(t, axis=2)
    o_f = chunk_gated_delta_rule(q, k, v, g[:, 0], beta[:, 0])
    o_b = flip(chunk_gated_delta_rule(flip(q), flip(k), flip(v), flip(g[:, 1]), flip(beta[:, 1])))
    o = jnp.moveaxis(o_f + o_b, 1, 2)
    o = rms_norm(o, norm_w) * jax.nn.silu(z)
    return o.reshape(b, s, wd)


def s5_scan(u, a_re, a_im, log_dt, b_re, b_im, c_re, c_im, reverse):
    a_re, a_im, b_re, b_im, c_re, c_im = (t.astype(F32) for t in (a_re, a_im, b_re, b_im, c_re, c_im))
    dt = jnp.exp(log_dt.astype(F32))[:, None]
    mag = jnp.exp(a_re * dt)
    lb_re, lb_im = mag * jnp.cos(a_im * dt), mag * jnp.sin(a_im * dt)
    den = a_re * a_re + a_im * a_im
    nr, ni = lb_re - 1.0, lb_im
    f_re = (nr * a_re + ni * a_im) / den
    f_im = (ni * a_re - nr * a_im) / den
    bb_re = f_re[..., None] * b_re - f_im[..., None] * b_im
    bb_im = f_re[..., None] * b_im + f_im[..., None] * b_re
    bu_re = jnp.einsum('gph,bsgh->bsgp', bb_re, u)
    bu_im = jnp.einsum('gph,bsgh->bsgp', bb_im, u)
    shape = bu_re.shape
    elems = (jnp.broadcast_to(lb_re, shape), jnp.broadcast_to(lb_im, shape), bu_re, bu_im)

    def combine(e1, e2):
        a1r, a1i, b1r, b1i = e1
        a2r, a2i, b2r, b2i = e2
        return (a2r * a1r - a2i * a1i, a2r * a1i + a2i * a1r,
                a2r * b1r - a2i * b1i + b2r, a2r * b1i + a2i * b1r + b2i)

    _, _, x_re, x_im = lax.associative_scan(combine, elems, axis=1, reverse=reverse)
    return jnp.einsum('ghp,bsgp->bsgh', c_re, x_re) - jnp.einsum('ghp,bsgp->bsgh', c_im, x_im)


def s5_mixer(p, a_re, a_im, log_dt, b_re, b_im, c_re, c_im, d_skip, glu_w, glu_b):
    bsz, s, _ = p.shape
    u = p.astype(F32)
    ug = u.reshape(bsz, s, S5_GROUPS, S5_GROUP)
    y = d_skip.astype(F32) * u
    for direction in range(2):
        y = y + s5_scan(ug, a_re[direction], a_im[direction], log_dt[direction], b_re[direction],
                        b_im[direction], c_re[direction], c_im[direction],
                        reverse=(direction == 1)).reshape(bsz, s, S5_WIDTH)
    y = jax.nn.gelu(y)
    return y * jax.nn.sigmoid(y @ glu_w.astype(F32) + glu_b.astype(F32))


def rwkv7_scan(r, w, k, v, kk, a, reverse):
    b, s, h, n = r.shape
    xs = tuple(jnp.moveaxis(t, 1, 0) for t in (r, w, k, v, -kk, kk * a))

    def step(state, xt):
        r_t, w_t, k_t, v_t, ka_t, kb_t = xt
        sa = jnp.einsum('bhvk,bhk->bhv', state, ka_t)
        state = (state * w_t[:, :, None, :] + sa[..., None] * kb_t[:, :, None, :]
                 + v_t[..., None] * k_t[:, :, None, :])
        return state, jnp.einsum('bhvk,bhk->bhv', state, r_t)

    _, y = lax.scan(step, jnp.zeros((b, h, n, n), F32), xs, reverse=reverse)
    return jnp.moveaxis(y, 0, 1)


def rwkv7_mixer(p, mu, w0, w2, a0, a2, g2, k_k, k_a, r_k, ln_w, ln_b):
    b, s, _ = p.shape
    pf = p.astype(F32)
    zero = jnp.zeros_like(pf[:, :1])
    neigh = 0.5 * (jnp.concatenate([zero, pf[:, :-1]], 1) + jnp.concatenate([pf[:, 1:], zero], 1))
    pf = pf + mu.astype(F32) * (neigh - pf)
    n = RW_WIDTH
    r, k, v = pf[..., :n], pf[..., n:2 * n], pf[..., 2 * n:3 * n]
    o = 3 * n
    w_lo = pf[..., o:o + 2 * RW_DECAY_LORA].reshape(b, s, 2, RW_DECAY_LORA)
    o += 2 * RW_DECAY_LORA
    a_lo = pf[..., o:o + 2 * RW_AAA_LORA].reshape(b, s, 2, RW_AAA_LORA)
    o += 2 * RW_AAA_LORA
    g_lo = pf[..., o:]
    w = -jax.nn.softplus(-(w0.astype(F32) + jnp.einsum('bsdr,drc->bsdc', jnp.tanh(w_lo), w2.astype(F32)))) - 0.5
    decay = jnp.exp(-jnp.exp(w))
    a = jax.nn.sigmoid(a0.astype(F32) + jnp.einsum('bsdr,drc->bsdc', a_lo, a2.astype(F32)))
    g = jax.nn.sigmoid(g_lo) @ g2.astype(F32)
    hd = lambda t: t.reshape(*t.shape[:-1], RW_HEADS, RW_HEAD_DIM)
    kk = l2_normalize(hd(k * k_k.astype(F32)))
    k_dir = k[:, :, None, :] * (1.0 + (a - 1.0) * k_a.astype(F32))
    rh, vh = hd(r), hd(v)
    y = (rwkv7_scan(rh, hd(decay[:, :, 0]), hd(k_dir[:, :, 0]), vh, kk, hd(a[:, :, 0]), False)
         + rwkv7_scan(rh, hd(decay[:, :, 1]), hd(k_dir[:, :, 1]), vh, kk, hd(a[:, :, 1]), True))
    mean = jnp.mean(y, axis=-1, keepdims=True)
    var = jnp.mean(jnp.square(y - mean), axis=-1, keepdims=True)
    y = (y - mean) * lax.rsqrt(var + RW_GN_EPS) * hd(ln_w.astype(F32)) + hd(ln_b.astype(F32))
    k_bonus = hd(jnp.mean(k_dir, axis=2))
    y = y + jnp.sum(rh * k_bonus * r_k.astype(F32), axis=-1, keepdims=True) * vh
    return y.reshape(b, s, n) * g


def alibi_slopes(n_heads):
    return jnp.asarray(2.0 ** (-8.0 * np.arange(1, n_heads + 1) / n_heads), dtype=F32)


def diff_attention(p, lam_params, subln_w, lambda_init):
    b, s, _ = p.shape
    pf = p.astype(F32)
    q = pf[..., :DF_WIDTH].reshape(b, s, DF_HEADS, 2, DF_HEAD_DIM) * DF_HEAD_DIM ** -0.5
    k = pf[..., DF_WIDTH:2 * DF_WIDTH].reshape(b, s, DF_HEADS, 2, DF_HEAD_DIM)
    v = pf[..., 2 * DF_WIDTH:].reshape(b, s, DF_HEADS, 2 * DF_HEAD_DIM)
    lp = lam_params.astype(F32)
    lam = jnp.exp(jnp.sum(lp[0] * lp[1])) - jnp.exp(jnp.sum(lp[2] * lp[3])) + lambda_init
    slopes = alibi_slopes(DF_HEADS)
    pos = jnp.arange(s)
    nb = s // DF_Q_BLOCK
    q_blocks = jnp.moveaxis(q.reshape(b, nb, DF_Q_BLOCK, DF_HEADS, 2, DF_HEAD_DIM), 1, 0)
    starts = jnp.arange(nb) * DF_Q_BLOCK

    def block(args):
        qb, q0 = args
        scores = jnp.einsum('bqhmd,bkhmd->bhmqk', qb, k)
        dist = jnp.abs((q0 + jnp.arange(DF_Q_BLOCK))[:, None] - pos[None, :]).astype(F32)
        scores = scores - slopes[:, None, None, None] * dist
        probs = jax.nn.softmax(scores, axis=-1)
        attn = probs[:, :, 0] - lam * probs[:, :, 1]
        return jnp.einsum('bhqk,bkhd->bqhd', attn, v)

    o = lax.map(block, (q_blocks, starts))
    o = jnp.moveaxis(o, 0, 1).reshape(b, s, DF_HEADS, 2 * DF_HEAD_DIM)
    o = rms_norm(o, subln_w, eps=1e-5) * (1.0 - lambda_init)
    return o.reshape(b, s, DF_WIDTH)


def expert_choice_ffn(h, router_w, w_gate, w_up, w_down):
    b, s, d = h.shape
    cap = EC_CAPACITY_FACTOR * s // N_EXPERTS
    probs = jax.nn.softmax((h @ router_w).astype(F32), axis=-1)
    gate, idx = lax.top_k(jnp.swapaxes(probs, 1, 2), cap)
    xin = jax.vmap(lambda hb, ib: hb[ib])(h, idx)
    hid = jax.nn.silu(jnp.einsum('becd,edf->becf', xin, w_gate)) * jnp.einsum('becd,edf->becf', xin, w_up)
    out = jnp.einsum('becf,efd->becd', hid, w_down) * gate[..., None].astype(h.dtype)
    bidx = jnp.arange(b)[:, None, None]
    return jnp.zeros_like(h).at[bidx, idx].add(out)


def setup_inputs(seed: int = 0) -> dict:
    key = jax.random.key(seed)
    ks = iter(jax.random.split(key, 48))
    nl, ne, no = DEPTH, (DEPTH + 1) // 2, DEPTH // 2
    d = D_MODEL

    def nrm(shape, scale=1.0):
        return jax.random.normal(next(ks), shape, F32) * scale

    def unif(shape, lo, hi):
        return jax.random.uniform(next(ks), shape, F32, lo, hi)

    dn_dt = jnp.exp(unif((ne, 2, DN_HEADS), math.log(1e-3), math.log(1e-1)))
    s5_n = jnp.arange(S5_STATE, dtype=F32)
    return {
        'x': nrm((BATCH, SEQ, d)),
        'c': nrm((BATCH, d)),
        'ada_w': nrm((nl, d, 6 * d), 0.5 * d ** -0.5),
        'ada_b': nrm((nl, 6 * d), 0.02),
        'norm_mix_pre': 1.0 + nrm((nl, d), 0.05),
        'norm_mix_post': 1.0 + nrm((nl, d), 0.05),
        'norm_ffn_pre': 1.0 + nrm((nl, d), 0.05),
        'norm_ffn_post': 1.0 + nrm((nl, d), 0.05),
        'router_w': nrm((nl, d, N_EXPERTS), d ** -0.5),
        'exp_w_gate': nrm((nl, N_EXPERTS, d, D_EXPERT), d ** -0.5),
        'exp_w_up': nrm((nl, N_EXPERTS, d, D_EXPERT), d ** -0.5),
        'exp_w_down': nrm((nl, N_EXPERTS, D_EXPERT, d), D_EXPERT ** -0.5),
        'ab_w_in': nrm((ne, d, AB_IN), d ** -0.5),
        'ab_w_out': nrm((ne, MIX_WIDTH, d), MIX_WIDTH ** -0.5),
        'dn_conv': nrm((ne, DN_CONV, 3 * DN_WIDTH), DN_CONV ** -0.5),
        'dn_a_log': jnp.log(unif((ne, 2, DN_HEADS), 1.0, 16.0)),
        'dn_dt_bias': dn_dt + jnp.log(-jnp.expm1(-dn_dt)),
        'dn_norm': 1.0 + nrm((ne, DN_HEAD_DIM), 0.05),
        's5_a_re': -0.5 + nrm((ne, 2, S5_GROUPS, S5_STATE), 0.01),
        's5_a_im': jnp.pi * s5_n + nrm((ne, 2, S5_GROUPS, S5_STATE), 0.01),
        's5_log_dt': unif((ne, 2, S5_GROUPS), math.log(1e-3), math.log(1e-1)),
        's5_b_re': nrm((ne, 2, S5_GROUPS, S5_STATE, S5_GROUP), (2 * S5_GROUP) ** -0.5),
        's5_b_im': nrm((ne, 2, S5_GROUPS, S5_STATE, S5_GROUP), (2 * S5_GROUP) ** -0.5),
        's5_c_re': nrm((ne, 2, S5_GROUPS, S5_GROUP, S5_STATE), S5_STATE ** -0.5),
        's5_c_im': nrm((ne, 2, S5_GROUPS, S5_GROUP, S5_STATE), S5_STATE ** -0.5),
        's5_d': nrm((ne, S5_WIDTH)),
        's5_glu_w': nrm((ne, S5_WIDTH, S5_WIDTH), S5_WIDTH ** -0.5),
        's5_glu_b': nrm((ne, S5_WIDTH), 0.02),
        'cd_w_in': nrm((no, d, CD_IN), d ** -0.5),
        'cd_w_out': nrm((no, MIX_WIDTH, d), MIX_WIDTH ** -0.5),
        'rw_mu': unif((no, RW_IN), 0.0, 1.0),
        'rw_w0': unif((no, 2, RW_WIDTH), -6.5, -1.5),
        'rw_w2': nrm((no, 2, RW_DECAY_LORA, RW_WIDTH), 0.5 * RW_DECAY_LORA ** -0.5),
        'rw_a0': nrm((no, 2, RW_WIDTH), 0.1),
        'rw_a2': nrm((no, 2, RW_AAA_LORA, RW_WIDTH), 0.5 * RW_AAA_LORA ** -0.5),
        'rw_g2': nrm((no, RW_GATE_LORA, RW_WIDTH), RW_GATE_LORA ** -0.5),
        'rw_k_k': 0.85 + nrm((no, RW_WIDTH), 0.05),
        'rw_k_a': 1.0 + nrm((no, RW_WIDTH), 0.05),
        'rw_r_k': nrm((no, RW_HEADS, RW_HEAD_DIM), 0.1),
        'rw_ln_w': 1.0 + nrm((no, RW_WIDTH), 0.05),
        'rw_ln_b': nrm((no, RW_WIDTH), 0.02),
        'df_lambda': nrm((no, 4, DF_HEAD_DIM), 0.1),
        'df_subln': 1.0 + nrm((no, 2 * DF_HEAD_DIM), 0.05),
    }


def reference(x, c, ada_w, ada_b, norm_mix_pre, norm_mix_post, norm_ffn_pre, norm_ffn_post,
              router_w, exp_w_gate, exp_w_up, exp_w_down,
              ab_w_in, ab_w_out, dn_conv, dn_a_log, dn_dt_bias, dn_norm,
              s5_a_re, s5_a_im, s5_log_dt, s5_b_re, s5_b_im, s5_c_re, s5_c_im, s5_d, s5_glu_w, s5_glu_b,
              cd_w_in, cd_w_out, rw_mu, rw_w0, rw_w2, rw_a0, rw_a2, rw_g2, rw_k_k, rw_k_a, rw_r_k,
              rw_ln_w, rw_ln_b, df_lambda, df_subln):
    for layer in range(DEPTH):
        mod = jax.nn.silu(c) @ ada_w[layer] + ada_b[layer]
        sh1, sc1, g1, sh2, sc2, g2 = jnp.split(mod[:, None, :], 6, axis=-1)
        h = rms_norm(x, norm_mix_pre[layer]) * (1.0 + sc1) + sh1
        if layer % 2 == 0:
            i = layer // 2
            p = h @ ab_w_in[i]
            ya = gated_deltanet(p[..., :DN_IN], dn_conv[i], dn_a_log[i], dn_dt_bias[i], dn_norm[i])
            yb = s5_mixer(p[..., DN_IN:], s5_a_re[i], s5_a_im[i], s5_log_dt[i], s5_b_re[i], s5_b_im[i],
                          s5_c_re[i], s5_c_im[i], s5_d[i], s5_glu_w[i], s5_glu_b[i])
            y = jnp.concatenate([ya, yb], axis=-1).astype(h.dtype) @ ab_w_out[i]
        else:
            i = layer // 2
            lambda_init = 0.8 - 0.6 * math.exp(-0.3 * layer)
            p = h @ cd_w_in[i]
            yc = rwkv7_mixer(p[..., :RW_IN], rw_mu[i], rw_w0[i], rw_w2[i], rw_a0[i], rw_a2[i], rw_g2[i],
                             rw_k_k[i], rw_k_a[i], rw_r_k[i], rw_ln_w[i], rw_ln_b[i])
            yd = diff_attention(p[..., RW_IN:], df_lambda[i], df_subln[i], lambda_init)
            y = jnp.concatenate([yc, yd], axis=-1).astype(h.dtype) @ cd_w_out[i]
        x = x + g1 * rms_norm(y, norm_mix_post[layer])
        h = rms_norm(x, norm_ffn_pre[layer]) * (1.0 + sc2) + sh2
        y = expert_choice_ffn(h, router_w[layer], exp_w_gate[layer], exp_w_up[layer], exp_w_down[layer])
        x = x + g2 * rms_norm(y, norm_ffn_post[layer])
    return x
```

```python
import functools
import math

import jax
import jax.numpy as jnp
import numpy as np
from jax import lax
from jax.experimental import pallas as pl
from jax.experimental.pallas import tpu as pltpu

F32 = jnp.float32
BF16 = jnp.bfloat16
HI = lax.Precision.HIGHEST

LANE = 128
SUBLANE = 8
VMEM_LIMIT = 56 * 1024 * 1024

NORM_EPS = 1e-6
DN_HEADS = 4
DN_HEAD_DIM = 128
DN_WIDTH = 512
DN_CONV = 5
DN_CHUNK = 64
S5_WIDTH = 512
S5_GROUP = 16
S5_GROUPS = 32
S5_STATE = 64
S5_CHUNK = 16
RW_WIDTH = 512
RW_HEADS = 8
RW_HEAD_DIM = 64
RW_LORA = 64
RW_GATE_LORA = 128
RW_GN_EPS = 64e-5
RW_CHUNK = 64
RW_IN = 1920
DF_WIDTH = 512
DF_HEADS = 4
DF_HEAD_DIM = 64
N_EXPERTS = 16
EC_CAPACITY_FACTOR = 2
NEG_BIG = -1e30
TOPK_BISECTIONS = 152


def _params(*sem):
    return pltpu.CompilerParams(dimension_semantics=sem, vmem_limit_bytes=VMEM_LIMIT)


def _sigmoid(x):
    return 1.0 / (1.0 + jnp.exp(-x))


def _silu(x):
    return x * _sigmoid(x)


def _softplus(x):
    return jnp.maximum(x, 0.0) + jnp.log(1.0 + jnp.exp(-jnp.abs(x)))


def _rms(x, eps):
    return x * lax.rsqrt(jnp.mean(x * x, axis=-1, keepdims=True) + eps)


def _dot(a, b, precision=None):
    return jnp.dot(a, b, preferred_element_type=F32, precision=precision)


def _dot_nt(a, b, precision=None):
    return lax.dot_general(a, b, (((1,), (1,)), ((), ())), preferred_element_type=F32, precision=precision)


def _dot_tn(a, b, precision=None):
    return lax.dot_general(a, b, (((0,), (0,)), ((), ())), preferred_element_type=F32, precision=precision)


def _ada_kernel(c_ref, w_ref, b_ref, o_ref):
    o_ref[...] = _dot(_silu(c_ref[...]), w_ref[...], HI) + b_ref[...]


def _ada_mod(c, ada_w, ada_b):
    depth, d, n = ada_w.shape
    bsz = c.shape[0]
    rows = -(-bsz // SUBLANE) * SUBLANE
    c_pad = jnp.zeros((rows, d), F32).at[:bsz].set(c)
    out = pl.pallas_call(
        _ada_kernel,
        grid=(depth, n // d),
        in_specs=[pl.BlockSpec((rows, d), lambda l, j: (0, 0)),
                  pl.BlockSpec((None, d, d), lambda l, j: (l, 0, j)),
                  pl.BlockSpec((None, 1, d), lambda l, j: (l, 0, j))],
        out_specs=pl.BlockSpec((None, rows, d), lambda l, j: (l, 0, j)),
        out_shape=jax.ShapeDtypeStruct((depth, rows, n), F32),
        compiler_params=_params("parallel", "parallel"),
        name="ada_mod",
    )(c_pad, ada_w, ada_b.reshape(depth, 1, n))
    return out[:, :bsz]


def _in_proj_kernel(x_ref, gain_ref, sh_ref, sc_ref, w_ref, o_ref):
    h = _rms(x_ref[...], NORM_EPS) * gain_ref[...] * (1.0 + sc_ref[...]) + sh_ref[...]
    o_ref[...] = _dot(h.astype(BF16), w_ref[...])


def _in_proj(x, gain, mod, w_bf16, tm=256):
    bsz, s, d = x.shape
    n = w_bf16.shape[1]
    return pl.pallas_call(
        _in_proj_kernel,
        grid=(bsz, s // tm),
        in_specs=[pl.BlockSpec((None, tm, d), lambda b, i: (b, i, 0)),
                  pl.BlockSpec((1, d), lambda b, i: (0, 0)),
                  pl.BlockSpec((None, 1, d), lambda b, i: (b, 0, 0)),
                  pl.BlockSpec((None, 1, d), lambda b, i: (b, 0, 1)),
                  pl.BlockSpec((d, n), lambda b, i: (0, 0))],
        out_specs=pl.BlockSpec((None, tm, n), lambda b, i: (b, i, 0)),
        out_shape=jax.ShapeDtypeStruct((bsz, s, n), F32),
        compiler_params=_params("parallel", "parallel"),
        name="in_proj",
    )(x, gain.reshape(1, d), mod, mod, w_bf16)


class _ChunkMasks:
    def __init__(self, n, chunk, reverse):
        ii = lax.broadcasted_iota(jnp.int32, (n, n), 0)
        jj = lax.broadcasted_iota(jnp.int32, (n, n), 1)
        same = (ii // chunk) == (jj // chunk)
        tri = (ii <= jj) if reverse else (ii >= jj)
        self.same = same
        self.incl = same & tri
        self.strict = self.incl & (ii != jj)
        self.eye = (ii == jj).astype(F32)
        m16 = (ii // 16) == (jj // 16)
        m32 = (ii // 32) == (jj // 32)
        self.m16 = m16.astype(F32)
        self.m32_only = (m32 & ~m16).astype(F32)
        self.m64_only = (same & ~m32).astype(F32)


def _unit_tri_inverse(lm, masks):
    n = -(lm * masks.m16)
    t = masks.eye + n
    p = n
    for _ in range(3):
        p = _dot(p, p)
        t = t + _dot(t, p)
    t = t - _dot(t, _dot(lm * masks.m32_only, t))
    t = t - _dot(t, _dot(lm * masks.m64_only, t))
    return t


def _dn_prep_kernel(cur_ref, prev_ref, next_ref, gates_ref, convw_ref, alog_ref, dtb_ref,
                    qkv_ref, gb_ref, ext_ref, *, ts):
    i = pl.program_id(1)
    n = pl.num_programs(1)
    halo = SUBLANE
    ext_ref[0:halo, :] = jnp.where(i > 0, prev_ref[...], 0.0)
    ext_ref[halo:halo + ts, :] = cur_ref[...]
    ext_ref[halo + ts:2 * halo + ts, :] = jnp.where(i < n - 1, next_ref[...], 0.0)
    pad = DN_CONV // 2
    for blk in range(3 * DN_HEADS):
        cols = slice(blk * DN_HEAD_DIM, (blk + 1) * DN_HEAD_DIM)
        acc = jnp.zeros((ts, DN_HEAD_DIM), F32)
        for tap in range(DN_CONV):
            acc = acc + ext_ref[halo - pad + tap:halo - pad + tap + ts, cols] * convw_ref[tap:tap + 1, cols]
        val = _silu(acc)
        if blk < 2 * DN_HEADS:
            val = val * lax.rsqrt(jnp.sum(val * val, axis=-1, keepdims=True) + 1e-6)
            if blk < DN_HEADS:
                val = val * (DN_HEAD_DIM ** -0.5)
        qkv_ref[:, cols] = val
    gin = gates_ref[...]
    g = -jnp.exp(alog_ref[...]) * _softplus(gin + dtb_ref[...])
    lane = lax.broadcasted_iota(jnp.int32, gin.shape, 1)
    gb_ref[...] = jnp.where(lane < 2 * DN_HEADS, g, _sigmoid(gin))


def _dn_prep(p0, conv_w, a_log, dt_bias, ts=512):
    bsz, s, _ = p0.shape
    c3 = 3 * DN_WIDTH
    gate_blk = (4 * DN_WIDTH + S5_WIDTH) // LANE
    nb8 = s // SUBLANE
    r8 = ts // SUBLANE
    alog = jnp.zeros((1, LANE), F32).at[0, :2 * DN_HEADS].set(a_log.reshape(-1))
    dtb = jnp.zeros((1, LANE), F32).at[0, :2 * DN_HEADS].set(dt_bias.reshape(-1))
    return pl.pallas_call(
        functools.partial(_dn_prep_kernel, ts=ts),
        grid=(bsz, s // ts),
        in_specs=[pl.BlockSpec((None, ts, c3), lambda b, i: (b, i, 0)),
                  pl.BlockSpec((None, SUBLANE, c3), lambda b, i: (b, jnp.maximum(i * r8 - 1, 0), 0)),
                  pl.BlockSpec((None, SUBLANE, c3), lambda b, i: (b, jnp.minimum((i + 1) * r8, nb8 - 1), 0)),
                  pl.BlockSpec((None, ts, LANE), lambda b, i: (b, i, gate_blk)),
                  pl.BlockSpec((DN_CONV, c3), lambda b, i: (0, 0)),
                  pl.BlockSpec((1, LANE), lambda b, i: (0, 0)),
                  pl.BlockSpec((1, LANE), lambda b, i: (0, 0))],
        out_specs=[pl.BlockSpec((None, ts, c3), lambda b, i: (b, i, 0)),
                   pl.BlockSpec((None, ts, LANE), lambda b, i: (b, i, 0))],
        out_shape=[jax.ShapeDtypeStruct((bsz, s, c3), F32),
                   jax.ShapeDtypeStruct((bsz, s, LANE), F32)],
        scratch_shapes=[pltpu.VMEM((ts + 2 * SUBLANE, c3), F32)],
        compiler_params=_params("parallel", "parallel"),
        name="dn_prep",
    )(p0, p0, p0, p0, conv_w, alog, dtb)


def _dn_direction(q_ref, k_ref, v_ref, gb_ref, gbt_ref, o_ref, state_ref, *, sc, reverse, direction):
    chunk = DN_CHUNK
    nck = sc // chunk
    masks = _ChunkMasks(sc, chunk, reverse)
    bd_incl = masks.incl.astype(F32)
    gcols = gb_ref[...]
    gc_cols = _dot(bd_incl, gcols, HI)
    gt_cols = _dot(masks.same.astype(F32), gcols, HI)
    gc_rows = _dot_nt(gbt_ref[...], bd_incl, HI)
    e_gc = jnp.exp(gc_cols)
    e_rest = jnp.exp(gt_cols - gc_cols)
    e_tot = jnp.exp(gt_cols)
    order = range(nck - 1, -1, -1) if reverse else range(nck)
    steps = []
    for h in range(DN_HEADS):
        gi = direction * DN_HEADS + h
        bi = 2 * DN_HEADS + gi
        cols = slice(h * DN_HEAD_DIM, (h + 1) * DN_HEAD_DIM)
        q = q_ref[:, cols]
        k = k_ref[:, cols]
        beta = gcols[:, bi:bi + 1]
        eg = e_gc[:, gi:gi + 1]
        decay = jnp.exp(jnp.where(masks.incl, gc_cols[:, gi:gi + 1] - gc_rows[gi:gi + 1, :], NEG_BIG))
        kb = k * beta
        lm = jnp.where(masks.strict, _dot_nt(kb, k) * decay, 0.0)
        t = _unit_tri_inverse(lm, masks)
        steps.append(dict(
            u=_dot(t, v_ref[:, cols] * beta), w=_dot(t, kb * eg), attn=_dot_nt(q, k) * decay, qd=q * eg,
            kd=k * e_rest[:, gi:gi + 1], gl=e_tot[:, gi:gi + 1], cols=cols, o_ref=o_ref, state_ref=state_ref, h=h,
            order=order))
    return steps


def _dn_recurrence(steps, chunk):
    states = [st['state_ref'][st['h']] for st in steps]
    for pos in range(len(steps[0]['order'])):
        for n, st in enumerate(steps):
            c = st['order'][pos]
            rows = slice(c * chunk, (c + 1) * chunk)
            v_new = st['u'][rows] - _dot(st['w'][rows], states[n])
            st['o_ref'][rows, st['cols']] = _dot(st['qd'][rows], states[n]) + _dot(st['attn'][rows, rows], v_new)
            states[n] = states[n] * st['gl'][c * chunk:c * chunk + 1] + _dot_tn(st['kd'][rows], v_new)
    for n, st in enumerate(steps):
        st['state_ref'][st['h']] = states[n]


def _dn_chunk_kernel(qf, kf, vf, gbf, gbtf, qr, kr, vr, gbr, gbtr, of_ref, or_ref, sf_ref, sr_ref, *, sc):
    @pl.when(pl.program_id(1) == 0)
    def _():
        sf_ref[...] = jnp.zeros_like(sf_ref)
        sr_ref[...] = jnp.zeros_like(sr_ref)

    steps = (_dn_direction(qf, kf, vf, gbf, gbtf, of_ref, sf_ref, sc=sc, reverse=False, direction=0)
             + _dn_direction(qr, kr, vr, gbr, gbtr, or_ref, sr_ref, sc=sc, reverse=True, direction=1))
    _dn_recurrence(steps, DN_CHUNK)


def _dn_chunked(qkv, gb, sc=256):
    bsz, s, _ = qkv.shape
    n = s // sc
    gbt = jnp.swapaxes(gb[..., :4 * DN_HEADS], 1, 2)
    fwd = lambda col: pl.BlockSpec((None, sc, DN_WIDTH), lambda b, i: (b, i, col))
    rev = lambda col: pl.BlockSpec((None, sc, DN_WIDTH), lambda b, i: (b, n - 1 - i, col))
    in_specs = ([fwd(0), fwd(1), fwd(2),
                 pl.BlockSpec((None, sc, LANE), lambda b, i: (b, i, 0)),
                 pl.BlockSpec((None, 4 * DN_HEADS, sc), lambda b, i: (b, 0, i))]
                + [rev(0), rev(1), rev(2),
                   pl.BlockSpec((None, sc, LANE), lambda b, i: (b, n - 1 - i, 0)),
                   pl.BlockSpec((None, 4 * DN_HEADS, sc), lambda b, i: (b, 0, n - 1 - i))])
    state = pltpu.VMEM((DN_HEADS, DN_HEAD_DIM, DN_HEAD_DIM), F32)
    return pl.pallas_call(
        functools.partial(_dn_chunk_kernel, sc=sc),
        grid=(bsz, n),
        in_specs=in_specs,
        out_specs=[pl.BlockSpec((None, sc, DN_WIDTH), lambda b, i: (b, i, 0)),
                   pl.BlockSpec((None, sc, DN_WIDTH), lambda b, i: (b, n - 1 - i, 0))],
        out_shape=[jax.ShapeDtypeStruct((bsz, s, DN_WIDTH), F32)] * 2,
        scratch_shapes=[state, state],
        compiler_params=_params("parallel", "arbitrary"),
        name="dn_chunked",
    )(qkv, qkv, qkv, gb, gbt, qkv, qkv, qkv, gb, gbt)


def _dn_post_kernel(of_ref, or_ref, z_ref, nw_ref, o_ref):
    for h in range(DN_HEADS):
        cols = slice(h * DN_HEAD_DIM, (h + 1) * DN_HEAD_DIM)
        o = of_ref[:, cols] + or_ref[:, cols]
        o_ref[:, cols] = _rms(o, NORM_EPS) * nw_ref[...] * _silu(z_ref[:, cols])


def _dn_post(o_f, o_r, p0, norm_w, ts=512):
    bsz, s, w = o_f.shape
    z_blk = 3 * DN_WIDTH // w
    spec = pl.BlockSpec((None, ts, w), lambda b, i: (b, i, 0))
    return pl.pallas_call(
        _dn_post_kernel,
        grid=(bsz, s // ts),
        in_specs=[spec, spec,
                  pl.BlockSpec((None, ts, w), lambda b, i: (b, i, z_blk)),
                  pl.BlockSpec((1, DN_HEAD_DIM), lambda b, i: (0, 0))],
        out_specs=spec,
        out_shape=jax.ShapeDtypeStruct((bsz, s, w), F32),
        compiler_params=_params("parallel", "parallel"),
        name="dn_post",
    )(o_f, o_r, p0, norm_w.reshape(1, DN_HEAD_DIM))


def _gated_deltanet(p0, conv_w, a_log, dt_bias, norm_w):
    qkv, gb = _dn_prep(p0, conv_w, a_log, dt_bias)
    o_f, o_r = _dn_chunked(qkv, gb)
    return _dn_post(o_f, o_r, p0, norm_w)


def _s5_tables(a_re, a_im, log_dt, b_re, b_im, c_re, c_im, n_chunks):
    lc, p, hh = S5_CHUNK, S5_STATE, S5_GROUP
    levels = int(math.log2(n_chunks))
    dt = jnp.exp(log_dt)[..., None]
    mag = jnp.exp(a_re * dt)
    lb_re, lb_im = mag * jnp.cos(a_im * dt), mag * jnp.sin(a_im * dt)
    den = a_re * a_re + a_im * a_im
    nr, ni = lb_re - 1.0, lb_im
    f_re = (nr * a_re + ni * a_im) / den
    f_im = (ni * a_re - nr * a_im) / den
    bb_re = f_re[..., None] * b_re - f_im[..., None] * b_im
    bb_im = f_re[..., None] * b_im + f_im[..., None] * b_re

    def power(tau):
        tau = jnp.asarray(tau, F32)
        m = jnp.exp(a_re[..., None] * dt[..., None] * tau)
        ang = a_im[..., None] * dt[..., None] * tau
        return m * jnp.cos(ang), m * jnp.sin(ang)

    pw_re, pw_im = power(np.arange(lc + 1))
    lbb_re = pw_re[..., None] * bb_re[:, :, :, None, :] - pw_im[..., None] * bb_im[:, :, :, None, :]
    lbb_im = pw_re[..., None] * bb_im[:, :, :, None, :] + pw_im[..., None] * bb_re[:, :, :, None, :]
    kt = (jnp.einsum('dgop,dgpti->dgtoi', c_re, lbb_re, precision=HI)
          - jnp.einsum('dgop,dgpti->dgtoi', c_im, lbb_im, precision=HI))
    ti = np.arange(lc)[:, None]
    to = np.arange(lc)[None, :]
    toeps, bmats, cmats = [], [], []
    for d in range(2):
        lag = (to - ti) if d == 0 else (ti - to)
        valid = jnp.asarray(lag >= 0, F32)
        blk = kt[d][:, np.clip(lag, 0, lc - 1)] * valid[None, :, :, None, None]
        toeps.append(jnp.transpose(blk, (0, 1, 4, 2, 3)).reshape(-1, lc * hh, lc * hh))
        e_in = (lc - 1 - np.arange(lc)) if d == 0 else np.arange(lc)
        bm_re = jnp.transpose(lbb_re[d][:, :, e_in, :], (0, 2, 3, 1))
        bm_im = jnp.transpose(lbb_im[d][:, :, e_in, :], (0, 2, 3, 1))
        bmats.append(jnp.concatenate([bm_re, bm_im], axis=-1).reshape(-1, lc * hh, 2 * p))
        e_out = (np.arange(lc) + 1) if d == 0 else (lc - np.arange(lc))
        m_re = c_re[d][:, :, :, None] * pw_re[d][:, None, :, e_out] - c_im[d][:, :, :, None] * pw_im[d][:, None, :, e_out]
        m_im = c_re[d][:, :, :, None] * pw_im[d][:, None, :, e_out] + c_im[d][:, :, :, None] * pw_re[d][:, None, :, e_out]
        cm = jnp.concatenate([jnp.transpose(m_re, (0, 2, 3, 1)), -jnp.transpose(m_im, (0, 2, 3, 1))], axis=1)
        cmats.append(cm.reshape(-1, 2 * p, lc * hh))
    sq_re, sq_im = power(lc * 2.0 ** np.arange(levels))
    scan_a = jnp.transpose(jnp.concatenate([sq_re, sq_re], axis=2), (0, 1, 3, 2))
    scan_b = jnp.transpose(jnp.concatenate([-sq_im, sq_im], axis=2), (0, 1, 3, 2))
    return jnp.stack(toeps), jnp.stack(bmats), jnp.stack(cmats), scan_a, scan_b


def _s5_kernel(u_ref, toep_ref, bmat_ref, cmat_ref, sa_ref, sb_ref, y_ref, *, n_chunks):
    u = u_ref[...]
    n = u.shape[0]
    p2 = 2 * S5_STATE
    levels = sa_ref.shape[1]
    cidx = lax.broadcasted_iota(jnp.int32, (n, p2), 0) & (n_chunks - 1)
    y = None
    for d in range(2):
        x = _dot(u, bmat_ref[d])
        for lv in range(levels):
            sh = 1 << lv
            if d == 0:
                xs = jnp.where(cidx >= sh, pltpu.roll(x, sh, 0), 0.0)
            else:
                xs = jnp.where(cidx < n_chunks - sh, pltpu.roll(x, n - sh, 0), 0.0)
            x = x + sa_ref[d, lv:lv + 1, :] * xs + sb_ref[d, lv:lv + 1, :] * pltpu.roll(xs, S5_STATE, 1)
        if d == 0:
            x_in = jnp.where(cidx >= 1, pltpu.roll(x, 1, 0), 0.0)
        else:
            x_in = jnp.where(cidx < n_chunks - 1, pltpu.roll(x, n - 1, 0), 0.0)
        yd = _dot(u, toep_ref[d]) + _dot(x_in, cmat_ref[d])
        y = yd if y is None else y + yd
    y_ref[...] = y


def _s5_scan(u, tables):
    bsz, s, _ = u.shape
    lc, g, hh = S5_CHUNK, S5_GROUPS, S5_GROUP
    nc = s // lc
    toep, bmat, cmat, sa, sb = tables
    uc = jnp.transpose(u.reshape(bsz, nc, lc, g, hh), (3, 0, 1, 2, 4)).reshape(g, bsz * nc, lc * hh)
    levels = sa.shape[2]
    y = pl.pallas_call(
        functools.partial(_s5_kernel, n_chunks=nc),
        grid=(g,),
        in_specs=[pl.BlockSpec((None, bsz * nc, lc * hh), lambda i: (i, 0, 0)),
                  pl.BlockSpec((2, None, lc * hh, lc * hh), lambda i: (0, i, 0, 0)),
                  pl.BlockSpec((2, None, lc * hh, 2 * S5_STATE), lambda i: (0, i, 0, 0)),
                  pl.BlockSpec((2, None, 2 * S5_STATE, lc * hh), lambda i: (0, i, 0, 0)),
                  pl.BlockSpec((2, None, levels, 2 * S5_STATE), lambda i: (0, i, 0, 0)),
                  pl.BlockSpec((2, None, levels, 2 * S5_STATE), lambda i: (0, i, 0, 0))],
        out_specs=pl.BlockSpec((None, bsz * nc, lc * hh), lambda i: (i, 0, 0)),
        out_shape=jax.ShapeDtypeStruct((g, bsz * nc, lc * hh), F32),
        compiler_params=_params("parallel"),
        name="s5_scan",
    )(uc, toep, bmat, cmat, sa, sb)
    return jnp.transpose(y.reshape(g, bsz, nc, lc, hh), (1, 2, 3, 0, 4)).reshape(bsz, s, g * hh)


def _gelu_tanh(x):
    return 0.5 * x * (1.0 + jnp.tanh(math.sqrt(2.0 / math.pi) * (x + 0.044715 * (x * x * x))))


def _s5_post_kernel(u_ref, y_ref, d_ref, w_ref, b_ref, o_ref):
    y = _gelu_tanh(d_ref[...] * u_ref[...] + y_ref[...])
    o_ref[...] = y * _sigmoid(_dot(y, w_ref[...]) + b_ref[...])


def _s5_post(p0, y, d_skip, glu_w, glu_b, ts=512):
    bsz, s, w = y.shape
    u_blk = 4 * DN_WIDTH // w
    spec = pl.BlockSpec((None, ts, w), lambda b, i: (b, i, 0))
    row = pl.BlockSpec((1, w), lambda b, i: (0, 0))
    return pl.pallas_call(
        _s5_post_kernel,
        grid=(bsz, s // ts),
        in_specs=[pl.BlockSpec((None, ts, w), lambda b, i: (b, i, u_blk)), spec, row,
                  pl.BlockSpec((w, w), lambda b, i: (0, 0)), row],
        out_specs=spec,
        out_shape=jax.ShapeDtypeStruct((bsz, s, w), F32),
        compiler_params=_params("parallel", "parallel"),
        name="s5_post",
    )(p0, y, d_skip.reshape(1, w), glu_w, glu_b.reshape(1, w))


def _s5_mixer(p0, a_re, a_im, log_dt, b_re, b_im, c_re, c_im, d_skip, glu_w, glu_b):
    s = p0.shape[1]
    u = p0[..., 4 * DN_WIDTH:4 * DN_WIDTH + S5_WIDTH]
    tables = _s5_tables(a_re, a_im, log_dt, b_re, b_im, c_re, c_im, s // S5_CHUNK)
    return _s5_post(p0, _s5_scan(u, tables), d_skip, glu_w, glu_b)


def _mix_out_kernel(ya_ref, yb_ref, x_ref, w_ref, npost_ref, g1_ref, npre_ref, sh2_ref, sc2_ref, rwt_ref,
                    x1_ref, h2_ref, pt_ref):
    half = ya_ref.shape[-1]
    y = _dot(ya_ref[...].astype(BF16), w_ref[0:half, :]) + _dot(yb_ref[...].astype(BF16), w_ref[half:2 * half, :])
    x1 = x_ref[...] + g1_ref[...] * (_rms(y, NORM_EPS) * npost_ref[...])
    x1_ref[...] = x1
    h2 = _rms(x1, NORM_EPS) * npre_ref[...] * (1.0 + sc2_ref[...]) + sh2_ref[...]
    h2_ref[...] = h2.astype(BF16)
    logits = _dot_nt(rwt_ref[...], h2, HI)
    ex = jnp.exp(logits - jnp.max(logits, axis=0, keepdims=True))
    pt_ref[...] = ex / jnp.sum(ex, axis=0, keepdims=True)


def _mix_out(ya, yb, x, w_out_bf16, npost, mod, npre, router_w, tm=256):
    bsz, s, d = x.shape
    half = ya.shape[-1]
    ne = router_w.shape[-1]
    row = lambda a: a.reshape(1, d)
    vec = pl.BlockSpec((1, d), lambda b, i: (0, 0))
    modblk = lambda k: pl.BlockSpec((None, 1, d), lambda b, i: (b, 0, k))
    tile = pl.BlockSpec((None, tm, d), lambda b, i: (b, i, 0))
    halfspec = pl.BlockSpec((None, tm, half), lambda b, i: (b, i, 0))
    return pl.pallas_call(
        _mix_out_kernel,
        grid=(bsz, s // tm),
        in_specs=[halfspec, halfspec, tile,
                  pl.BlockSpec((d, d), lambda b, i: (0, 0)), vec, modblk(2), vec, modblk(3), modblk(4),
                  pl.BlockSpec((ne, d), lambda b, i: (0, 0))],
        out_specs=[tile, tile, pl.BlockSpec((None, ne, tm), lambda b, i: (b, 0, i))],
        out_shape=[jax.ShapeDtypeStruct((bsz, s, d), F32),
                   jax.ShapeDtypeStruct((bsz, s, d), BF16),
                   jax.ShapeDtypeStruct((bsz, ne, s), F32)],
        compiler_params=_params("parallel", "parallel"),
        name="mix_out",
    )(ya, yb, x, w_out_bf16, row(npost), mod, row(npre), mod, mod, router_w.T)


def _prefix_lanes(x):
    n = x.shape[1]
    lane = lax.broadcasted_iota(jnp.int32, x.shape, 1)
    sh = 1
    while sh < n:
        x = x + jnp.where(lane >= sh, pltpu.roll(x, sh, 1), 0.0)
        sh *= 2
    return x


def _topk_kernel(p_ref, slot_ref, *, cap):
    p = p_ref[...]
    rows = p.shape[0]

    def halve(_, bounds):
        lo, hi = bounds
        mid = 0.5 * (lo + hi)
        ok = jnp.sum((p >= mid).astype(F32), axis=1, keepdims=True) >= cap
        return jnp.where(ok, mid, lo), jnp.where(ok, hi, mid)

    lo, _ = lax.fori_loop(0, TOPK_BISECTIONS, halve, (jnp.zeros((rows, 1), F32), jnp.full((rows, 1), 2.0, F32)))
    gt = p > lo
    eq = p == lo
    need = cap - jnp.sum(gt.astype(F32), axis=1, keepdims=True)
    sel = gt | (eq & (_prefix_lanes(eq.astype(F32)) <= need))
    rank = _prefix_lanes(sel.astype(F32)) - 1.0
    slot_ref[...] = jnp.where(sel, rank, -1.0).astype(jnp.int32)


def _topk_slots(pt, cap):
    bsz, ne, s = pt.shape
    spec = pl.BlockSpec((None, ne, s), lambda b: (b, 0, 0))
    return pl.pallas_call(
        functools.partial(_topk_kernel, cap=cap),
        grid=(bsz,), in_specs=[spec], out_specs=spec,
        out_shape=jax.ShapeDtypeStruct((bsz, ne, s), jnp.int32),
        compiler_params=_params("parallel"),
        name="topk_slots",
    )(pt)


def _gather_kernel(slot_ref, p_ref, h_ref, xin_ref, gate_ref, *, cap, tk):
    s = h_ref.shape[0]
    acc = jnp.zeros(xin_ref.shape, F32)
    gate = jnp.zeros((cap, 1), F32)
    want = lax.broadcasted_iota(jnp.int32, (cap, tk), 0)
    for k in range(s // tk):
        hit = want == slot_ref[:, k * tk:(k + 1) * tk]
        acc = acc + _dot(hit.astype(BF16), h_ref[k * tk:(k + 1) * tk, :])
        gate = gate + jnp.sum(jnp.where(hit, p_ref[:, k * tk:(k + 1) * tk], 0.0), axis=1, keepdims=True)
    xin_ref[...] = acc.astype(BF16)
    gate_ref[...] = gate


def _moe_gather(slot, pt, h_bf16, cap, tk=1024):
    bsz, ne, s = slot.shape
    d = h_bf16.shape[-1]
    rowspec = pl.BlockSpec((None, None, 1, s), lambda b, e: (b, e, 0, 0))
    return pl.pallas_call(
        functools.partial(_gather_kernel, cap=cap, tk=min(tk, s)),
        grid=(bsz, ne),
        in_specs=[rowspec, rowspec, pl.BlockSpec((None, s, d), lambda b, e: (b, 0, 0))],
        out_specs=[pl.BlockSpec((None, None, cap, d), lambda b, e: (b, e, 0, 0)),
                   pl.BlockSpec((None, None, cap, 1), lambda b, e: (b, e, 0, 0))],
        out_shape=[jax.ShapeDtypeStruct((bsz, ne, cap, d), BF16),
                   jax.ShapeDtypeStruct((bsz, ne, cap, 1), F32)],
        compiler_params=_params("parallel", "arbitrary"),
        name="moe_gather",
    )(slot.reshape(bsz, ne, 1, s), pt.reshape(bsz, ne, 1, s), h_bf16)


def _ffn_kernel(xin_ref, gate_ref, wg_ref, wu_ref, wd_ref, y_ref, acc_ref):
    f = pl.program_id(1)
    bsz, cap, d = xin_ref.shape
    x = xin_ref[...].reshape(bsz * cap, d)
    hid = _silu(_dot(x, wg_ref[...].astype(BF16))) * _dot(x, wu_ref[...].astype(BF16))
    part = _dot(hid.astype(BF16), wd_ref[...].astype(BF16))

    @pl.when(f == 0)
    def _():
        acc_ref[...] = part

    @pl.when(f > 0)
    def _():
        acc_ref[...] += part

    @pl.when(f == pl.num_programs(1) - 1)
    def _():
        y_ref[...] = (acc_ref[...] * gate_ref[...].reshape(bsz * cap, 1)).astype(BF16).reshape(bsz, cap, d)


def _moe_ffn(xin, gate, w_gate, w_up, w_down, layer, tf=256):
    bsz, ne, cap, d = xin.shape
    dexp = w_gate.shape[-1]
    return pl.pallas_call(
        _ffn_kernel,
        grid=(ne, dexp // tf),
        in_specs=[pl.BlockSpec((bsz, None, cap, d), lambda e, f: (0, e, 0, 0)),
                  pl.BlockSpec((bsz, None, cap, 1), lambda e, f: (0, e, 0, 0)),
                  pl.BlockSpec((None, None, d, tf), lambda e, f: (layer, e, 0, f)),
                  pl.BlockSpec((None, None, d, tf), lambda e, f: (layer, e, 0, f)),
                  pl.BlockSpec((None, None, tf, d), lambda e, f: (layer, e, f, 0))],
        out_specs=pl.BlockSpec((bsz, None, cap, d), lambda e, f: (0, e, 0, 0)),
        out_shape=jax.ShapeDtypeStruct((bsz, ne, cap, d), BF16),
        scratch_shapes=[pltpu.VMEM((bsz * cap, d), F32)],
        compiler_params=_params("parallel", "arbitrary"),
        name="moe_ffn",
    )(xin, gate, w_gate, w_up, w_down)


def _scatter_kernel(slot_ref, y_ref, x_ref, g2_ref, npost_ref, o_ref, acc_ref, *, cap):
    e = pl.program_id(2)
    ts = slot_ref.shape[0]
    hit = slot_ref[...] == lax.broadcasted_iota(jnp.int32, (ts, cap), 1)
    part = _dot(hit.astype(BF16), y_ref[...])

    @pl.when(e == 0)
    def _():
        acc_ref[...] = part

    @pl.when(e > 0)
    def _():
        acc_ref[...] += part

    @pl.when(e == pl.num_programs(2) - 1)
    def _():
        o_ref[...] = x_ref[...] + g2_ref[...] * (_rms(acc_ref[...], NORM_EPS) * npost_ref[...])


def _moe_scatter(slot, y, x, mod, npost, ts=1024):
    bsz, ne, cap, d = y.shape
    s = x.shape[1]
    ts = min(ts, s)
    tile = pl.BlockSpec((None, ts, d), lambda b, i, e: (b, i, 0))
    return pl.pallas_call(
        functools.partial(_scatter_kernel, cap=cap),
        grid=(bsz, s // ts, ne),
        in_specs=[pl.BlockSpec((None, None, ts, 1), lambda b, i, e: (b, e, i, 0)),
                  pl.BlockSpec((None, None, cap, d), lambda b, i, e: (b, e, 0, 0)),
                  tile,
                  pl.BlockSpec((None, 1, d), lambda b, i, e: (b, 0, 5)),
                  pl.BlockSpec((1, d), lambda b, i, e: (0, 0))],
        out_specs=tile,
        out_shape=jax.ShapeDtypeStruct((bsz, s, d), F32),
        scratch_shapes=[pltpu.VMEM((ts, d), F32)],
        compiler_params=_params("parallel", "parallel", "arbitrary"),
        name="moe_scatter",
    )(slot.reshape(bsz, ne, s, 1), y, x, mod, npost.reshape(1, d))


def _expert_choice_ffn(x1, h2, pt, mod, npost, w_gate, w_up, w_down, layer):
    s = x1.shape[1]
    ne = pt.shape[1]
    cap = EC_CAPACITY_FACTOR * s // ne
    slot = _topk_slots(pt, cap)
    xin, gate = _moe_gather(slot, pt, h2, cap)
    y = _moe_ffn(xin, gate, w_gate, w_up, w_down, layer)
    return _moe_scatter(slot, y, x1, mod, npost)


def _rw_prep_kernel(cur_ref, prev_ref, next_ref, mu_ref, w0_ref, w2_ref, a0_ref, a2_ref, g2_ref, kkw_ref, ka_ref,
                    hsum_ref, r_ref, v_ref, kk_ref, g_ref, lw0_ref, lw1_ref, kd0_ref, kd1_ref, b0_ref, b1_ref,
                    ext_ref, *, ts):
    i = pl.program_id(1)
    n = pl.num_programs(1)
    halo = SUBLANE
    ext_ref[0:halo, :] = jnp.where(i > 0, prev_ref[...], 0.0)
    ext_ref[halo:halo + ts, :] = cur_ref[...]
    ext_ref[halo + ts:2 * halo + ts, :] = jnp.where(i < n - 1, next_ref[...], 0.0)

    def shifted(cols):
        pf = ext_ref[halo:halo + ts, cols]
        neigh = 0.5 * (ext_ref[halo - 1:halo - 1 + ts, cols] + ext_ref[halo + 1:halo + 1 + ts, cols])
        return pf + mu_ref[:, cols] * (neigh - pf)

    w = RW_WIDTH
    r = shifted(slice(0, w))
    k = shifted(slice(w, 2 * w))
    v = shifted(slice(2 * w, 3 * w))
    lora = shifted(slice(3 * w, 3 * w + 3 * LANE))
    wdec = -_softplus(-(w0_ref[...] + _dot(jnp.tanh(lora[:, 0:LANE]), w2_ref[...]))) - 0.5
    lw = -jnp.exp(wdec)
    a = _sigmoid(a0_ref[...] + _dot(lora[:, LANE:2 * LANE], a2_ref[...]))
    g_ref[...] = _dot(_sigmoid(lora[:, 2 * LANE:3 * LANE]), g2_ref[...])
    kk = k * kkw_ref[...]
    kk = kk * lax.rsqrt(_dot(kk * kk, hsum_ref[...], HI) + 1e-6)
    r_ref[...] = r
    v_ref[...] = v
    kk_ref[...] = kk
    for d, (lw_ref, kd_ref, b_ref) in enumerate(((lw0_ref, kd0_ref, b0_ref), (lw1_ref, kd1_ref, b1_ref))):
        ad = a[:, d * w:(d + 1) * w]
        lw_ref[...] = lw[:, d * w:(d + 1) * w]
        kd_ref[...] = k * (1.0 + (ad - 1.0) * ka_ref[...])
        b_ref[...] = kk * ad


def _block_diag2(m):
    z = jnp.zeros_like(m[0])
    return jnp.concatenate([jnp.concatenate([m[0], z], 1), jnp.concatenate([z, m[1]], 1)], 0)


def _head_sum_matrix(width, head_dim):
    idx = np.arange(width) // head_dim
    return jnp.asarray(idx[:, None] == idx[None, :], F32)


def _rw_prep(p1, mu, w0, w2, a0, a2, g2, k_k, k_a, ts=256):
    bsz, s, _ = p1.shape
    cin = RW_IN
    w = RW_WIDTH
    nb8 = s // SUBLANE
    r8 = ts // SUBLANE
    full = lambda shape: pl.BlockSpec(shape, lambda b, i: (0,) * len(shape))
    out_spec = pl.BlockSpec((None, ts, w), lambda b, i: (b, i, 0))
    return pl.pallas_call(
        functools.partial(_rw_prep_kernel, ts=ts),
        grid=(bsz, s // ts),
        in_specs=[pl.BlockSpec((None, ts, cin), lambda b, i: (b, i, 0)),
                  pl.BlockSpec((None, SUBLANE, cin), lambda b, i: (b, jnp.maximum(i * r8 - 1, 0), 0)),
                  pl.BlockSpec((None, SUBLANE, cin), lambda b, i: (b, jnp.minimum((i + 1) * r8, nb8 - 1), 0)),
                  full((1, cin)), full((1, 2 * w)), full((LANE, 2 * w)), full((1, 2 * w)), full((LANE, 2 * w)),
                  full((RW_GATE_LORA, w)), full((1, w)), full((1, w)), full((w, w))],
        out_specs=[out_spec] * 10,
        out_shape=[jax.ShapeDtypeStruct((bsz, s, w), F32)] * 10,
        scratch_shapes=[pltpu.VMEM((ts + 2 * SUBLANE, cin), F32)],
        compiler_params=_params("parallel", "parallel"),
        name="rw_prep",
    )(p1, p1, p1, mu.reshape(1, cin), w0.reshape(1, 2 * w), _block_diag2(w2), a0.reshape(1, 2 * w),
      _block_diag2(a2), g2, k_k.reshape(1, w), k_a.reshape(1, w), _head_sum_matrix(w, RW_HEAD_DIM))


def _rw_direction(r_ref, v_ref, kk_ref, lw_ref, kd_ref, b_ref, o_ref, state_ref, *, sc, reverse):
    chunk = RW_CHUNK
    nck = sc // chunk
    hd = RW_HEAD_DIM
    masks = _ChunkMasks(sc, chunk, reverse)
    lw = lw_ref[...]
    lg = _dot(masks.incl.astype(F32), lw, HI)
    lt = _dot(masks.same.astype(F32), lw, HI)
    e_neg = jnp.exp(-lg)
    kkt = kk_ref[...] * jnp.exp(lg - lw)
    khat = kd_ref[...] * e_neg
    bhat = b_ref[...] * e_neg
    rt = r_ref[...] * jnp.exp(lg)
    e_rest = jnp.exp(lt - lg)
    kbar = kd_ref[...] * e_rest
    bbar = b_ref[...] * e_rest
    e_tot = jnp.exp(lt)
    order = range(nck - 1, -1, -1) if reverse else range(nck)
    steps = []
    for h in range(RW_HEADS):
        cols = slice(h * hd, (h + 1) * hd)
        v = v_ref[:, cols]
        kkt_h = kkt[:, cols]
        khat_h = khat[:, cols]
        bhat_h = bhat[:, cols]
        rt_h = rt[:, cols]
        l_a = jnp.where(masks.strict, _dot_nt(kkt_h, khat_h), 0.0)
        l_b = jnp.where(masks.strict, _dot_nt(kkt_h, bhat_h), 0.0)
        t = _unit_tri_inverse(l_b, masks)
        steps.append(dict(
            wm=_dot(t, kkt_h), u=_dot(t, _dot(l_a, v)), rt=rt_h,
            y0=_dot(jnp.where(masks.incl, _dot_nt(rt_h, khat_h), 0.0), v),
            a_qb=jnp.where(masks.incl, _dot_nt(rt_h, bhat_h), 0.0),
            v=v, kbar=kbar[:, cols], bbar=bbar[:, cols], gl=e_tot[:, cols],
            cols=cols, o_ref=o_ref, state_ref=state_ref, h=h, order=order))
    return steps


def _rw_recurrence(steps, chunk):
    states = [st['state_ref'][st['h']] for st in steps]
    for pos in range(len(steps[0]['order'])):
        for n, st in enumerate(steps):
            c = st['order'][pos]
            rows = slice(c * chunk, (c + 1) * chunk)
            p = _dot_nt(st['wm'][rows], states[n]) + st['u'][rows]
            st['o_ref'][rows, st['cols']] = (st['y0'][rows] + _dot_nt(st['rt'][rows], states[n])
                                             - _dot(st['a_qb'][rows, rows], p))
            states[n] = (states[n] * st['gl'][c * chunk:c * chunk + 1] + _dot_tn(st['v'][rows], st['kbar'][rows])
                         - _dot_tn(p, st['bbar'][rows]))
    for n, st in enumerate(steps):
        st['state_ref'][st['h']] = states[n]


def _rw_chunk_kernel(rf, vf, kkf, lwf, kdf, bf, rr, vr, kkr, lwr, kdr, br, of_ref, or_ref, sf_ref, sr_ref, *, sc):
    @pl.when(pl.program_id(1) == 0)
    def _():
        sf_ref[...] = jnp.zeros_like(sf_ref)
        sr_ref[...] = jnp.zeros_like(sr_ref)

    steps = (_rw_direction(rf, vf, kkf, lwf, kdf, bf, of_ref, sf_ref, sc=sc, reverse=False)
             + _rw_direction(rr, vr, kkr, lwr, kdr, br, or_ref, sr_ref, sc=sc, reverse=True))
    _rw_recurrence(steps, RW_CHUNK)


def _rw_chunked(r, v, kk, lw0, lw1, kd0, kd1, b0, b1, sc=256):
    bsz, s, w = r.shape
    n = s // sc
    fwd = pl.BlockSpec((None, sc, w), lambda b, i: (b, i, 0))
    rev = pl.BlockSpec((None, sc, w), lambda b, i: (b, n - 1 - i, 0))
    state = pltpu.VMEM((RW_HEADS, RW_HEAD_DIM, RW_HEAD_DIM), F32)
    return pl.pallas_call(
        functools.partial(_rw_chunk_kernel, sc=sc),
        grid=(bsz, n),
        in_specs=[fwd] * 6 + [rev] * 6,
        out_specs=[fwd, rev],
        out_shape=[jax.ShapeDtypeStruct((bsz, s, w), F32)] * 2,
        scratch_shapes=[state, state],
        compiler_params=_params("parallel", "arbitrary"),
        name="rw_chunked",
    )(r, v, kk, lw0, kd0, b0, r, v, kk, lw1, kd1, b1)


def _rw_post_kernel(yf_ref, yr_ref, r_ref, v_ref, g_ref, kd0_ref, kd1_ref, rk_ref, lnw_ref, lnb_ref, hsum_ref, o_ref):
    inv = 1.0 / RW_HEAD_DIM
    y = yf_ref[...] + yr_ref[...]
    mean = _dot(y, hsum_ref[...], HI) * inv
    cen = y - mean
    var = _dot(cen * cen, hsum_ref[...], HI) * inv
    y = cen * lax.rsqrt(var + RW_GN_EPS) * lnw_ref[...] + lnb_ref[...]
    k_bonus = 0.5 * (kd0_ref[...] + kd1_ref[...])
    y = y + _dot(r_ref[...] * k_bonus * rk_ref[...], hsum_ref[...], HI) * v_ref[...]
    o_ref[...] = y * g_ref[...]


def _rw_post(y_f, y_r, r, v, g, kd0, kd1, r_k, ln_w, ln_b, ts=512):
    bsz, s, w = r.shape
    spec = pl.BlockSpec((None, ts, w), lambda b, i: (b, i, 0))
    row = pl.BlockSpec((1, w), lambda b, i: (0, 0))
    return pl.pallas_call(
        _rw_post_kernel,
        grid=(bsz, s // ts),
        in_specs=[spec] * 7 + [row, row, row, pl.BlockSpec((w, w), lambda b, i: (0, 0))],
        out_specs=spec,
        out_shape=jax.ShapeDtypeStruct((bsz, s, w), F32),
        compiler_params=_params("parallel", "parallel"),
        name="rw_post",
    )(y_f, y_r, r, v, g, kd0, kd1, r_k.reshape(1, w), ln_w.reshape(1, w), ln_b.reshape(1, w),
      _head_sum_matrix(w, RW_HEAD_DIM))


def _rwkv7_mixer(p1, mu, w0, w2, a0, a2, g2, k_k, k_a, r_k, ln_w, ln_b):
    r, v, kk, g, lw0, lw1, kd0, kd1, b0, b1 = _rw_prep(p1, mu, w0, w2, a0, a2, g2, k_k, k_a)
    y_f, y_r = _rw_chunked(r, v, kk, lw0, lw1, kd0, kd1, b0, b1)
    return _rw_post(y_f, y_r, r, v, g, kd0, kd1, r_k, ln_w, ln_b)


def _diff_attn_kernel(q_ref, k_ref, v_ref, lam_ref, sub_ref, o_ref, *, tq, lambda_init):
    h = pl.program_id(1)
    i = pl.program_id(2)
    s = k_ref.shape[0]
    hd = DF_HEAD_DIM
    slope = jnp.where(h == 0, 2.0 ** -2, jnp.where(h == 1, 2.0 ** -4, jnp.where(h == 2, 2.0 ** -6, 2.0 ** -8)))
    lp = lam_ref[...]
    lam = (jnp.exp(jnp.sum(lp[0:1] * lp[1:2], axis=1, keepdims=True))
           - jnp.exp(jnp.sum(lp[2:3] * lp[3:4], axis=1, keepdims=True)) + lambda_init)
    qpos = i * tq + lax.broadcasted_iota(jnp.int32, (tq, s), 0)
    kpos = lax.broadcasted_iota(jnp.int32, (tq, s), 1)
    bias = slope.astype(F32) * jnp.abs(qpos - kpos).astype(F32)
    probs = []
    for m in range(2):
        q = q_ref[:, m * hd:(m + 1) * hd] * (hd ** -0.5)
        sc = _dot_nt(q, k_ref[:, m * hd:(m + 1) * hd]) - bias
        ex = jnp.exp(sc - jnp.max(sc, axis=1, keepdims=True))
        probs.append(ex / jnp.sum(ex, axis=1, keepdims=True))
    attn = probs[0] - lam * probs[1]
    o = _dot(attn, v_ref[...])
    o_ref[...] = _rms(o, 1e-5) * sub_ref[...] * (1.0 - lambda_init)


def _diff_attention(p1, lam_params, subln_w, lambda_init, tq=256):
    bsz, s, _ = p1.shape
    hw = 2 * DF_HEAD_DIM
    q0 = RW_IN // hw
    return pl.pallas_call(
        functools.partial(_diff_attn_kernel, tq=tq, lambda_init=lambda_init),
        grid=(bsz, DF_HEADS, s // tq),
        in_specs=[pl.BlockSpec((None, tq, hw), lambda b, h, i: (b, i, q0 + h)),
                  pl.BlockSpec((None, s, hw), lambda b, h, i: (b, 0, q0 + DF_HEADS + h)),
                  pl.BlockSpec((None, s, hw), lambda b, h, i: (b, 0, q0 + 2 * DF_HEADS + h)),
                  pl.BlockSpec((4, DF_HEAD_DIM), lambda b, h, i: (0, 0)),
                  pl.BlockSpec((1, hw), lambda b, h, i: (0, 0))],
        out_specs=pl.BlockSpec((None, tq, hw), lambda b, h, i: (b, i, h)),
        out_shape=jax.ShapeDtypeStruct((bsz, s, DF_HEADS * hw), F32),
        compiler_params=_params("parallel", "parallel", "parallel"),
        name="diff_attn",
    )(p1, p1, p1, lam_params, subln_w.reshape(1, hw))


def _layer0_weights(ab_w_in):
    d = ab_w_in.shape[0]
    main = 4 * DN_WIDTH
    gates = 4 * DN_HEADS
    w = jnp.concatenate([ab_w_in[:, :main], ab_w_in[:, main + gates:], ab_w_in[:, main:main + gates],
                         jnp.zeros((d, LANE - gates), ab_w_in.dtype)], axis=1)
    return w.astype(BF16)


def kernel(x, c, ada_w, ada_b, norm_mix_pre, norm_mix_post, norm_ffn_pre, norm_ffn_post, router_w, exp_w_gate, exp_w_up, exp_w_down, ab_w_in, ab_w_out, dn_conv, dn_a_log, dn_dt_bias, dn_norm, s5_a_re, s5_a_im, s5_log_dt, s5_b_re, s5_b_im, s5_c_re, s5_c_im, s5_d, s5_glu_w, s5_glu_b, cd_w_in, cd_w_out, rw_mu, rw_w0, rw_w2, rw_a0, rw_a2, rw_g2, rw_k_k, rw_k_a, rw_r_k, rw_ln_w, rw_ln_b, df_lambda, df_subln):
    depth = ada_w.shape[0]
    mod_all = _ada_mod(c, ada_w, ada_b)
    for layer in range(depth):
        mod = mod_all[layer][:, None, :]
        i = layer // 2
        if layer % 2 == 0:
            p = _in_proj(x, norm_mix_pre[layer], mod, _layer0_weights(ab_w_in[i]))
            ya = _gated_deltanet(p, dn_conv[i], dn_a_log[i], dn_dt_bias[i], dn_norm[i])
            yb = _s5_mixer(p, s5_a_re[i], s5_a_im[i], s5_log_dt[i], s5_b_re[i], s5_b_im[i], s5_c_re[i], s5_c_im[i],
                           s5_d[i], s5_glu_w[i], s5_glu_b[i])
            w_out = ab_w_out[i]
        else:
            lambda_init = 0.8 - 0.6 * math.exp(-0.3 * layer)
            p = _in_proj(x, norm_mix_pre[layer], mod, cd_w_in[i].astype(BF16))
            ya = _rwkv7_mixer(p, rw_mu[i], rw_w0[i], rw_w2[i], rw_a0[i], rw_a2[i], rw_g2[i], rw_k_k[i], rw_k_a[i],
                              rw_r_k[i].reshape(-1), rw_ln_w[i], rw_ln_b[i])
            yb = _diff_attention(p, df_lambda[i], df_subln[i], lambda_init)
            w_out = cd_w_out[i]
        x1, h2, pt = _mix_out(ya, yb, x, w_out.astype(BF16), norm_mix_post[layer], mod, norm_ffn_pre[layer],
                              router_w[layer])
        x = _expert_choice_ffn(x1, h2, pt, mod, norm_ffn_post[layer], exp_w_gate, exp_w_up, exp_w_down, layer)
    return x
```

```python
import functools
import math

import jax
import jax.numpy as jnp
import numpy as np
from jax import lax
from jax.experimental import pallas as pl
from jax.experimental.pallas import tpu as pltpu

F32 = jnp.float32
BF16 = jnp.bfloat16
HI = lax.Precision.HIGHEST

LANE = 128
SUBLANE = 8
VMEM_LIMIT = 56 * 1024 * 1024

NORM_EPS = 1e-6
DN_HEADS = 4
DN_HEAD_DIM = 128
DN_WIDTH = 512
DN_CONV = 5
DN_CHUNK = 64
S5_WIDTH = 512
S5_GROUP = 16
S5_GROUPS = 32
S5_STATE = 64
S5_CHUNK = 16
RW_WIDTH = 512
RW_HEADS = 8
RW_HEAD_DIM = 64
RW_LORA = 64
RW_GATE_LORA = 128
RW_GN_EPS = 64e-5
RW_CHUNK = 64
RW_IN = 1920
DF_WIDTH = 512
DF_HEADS = 4
DF_HEAD_DIM = 64
N_EXPERTS = 16
EC_CAPACITY_FACTOR = 2
NEG_BIG = -1e30
MOE_TOKEN_TILE = 256
TOPK_BISECTIONS = 152


def _params(*sem):
    return pltpu.CompilerParams(dimension_semantics=sem, vmem_limit_bytes=VMEM_LIMIT)


def _sigmoid(x):
    return 1.0 / (1.0 + jnp.exp(-x))


def _silu(x):
    return x * _sigmoid(x)


def _softplus(x):
    return jnp.maximum(x, 0.0) + jnp.log(1.0 + jnp.exp(-jnp.abs(x)))


def _rms(x, eps):
    return x * lax.rsqrt(jnp.mean(x * x, axis=-1, keepdims=True) + eps)


def _mxu_operands(a, b, precision):
    if precision is None:
        return a.astype(BF16), b.astype(BF16)
    return a, b


def _dot(a, b, precision=None):
    a, b = _mxu_operands(a, b, precision)
    return jnp.dot(a, b, preferred_element_type=F32, precision=precision)


def _dot_nt(a, b, precision=None):
    a, b = _mxu_operands(a, b, precision)
    return lax.dot_general(a, b, (((1,), (1,)), ((), ())), preferred_element_type=F32, precision=precision)


def _dot_tn(a, b, precision=None):
    a, b = _mxu_operands(a, b, precision)
    return lax.dot_general(a, b, (((0,), (0,)), ((), ())), preferred_element_type=F32, precision=precision)


def _ada_kernel(c_ref, w_ref, b_ref, o_ref):
    o_ref[...] = _dot(_silu(c_ref[...]), w_ref[...], HI) + b_ref[...]


def _ada_mod(c, ada_w, ada_b):
    depth, d, n = ada_w.shape
    bsz = c.shape[0]
    rows = -(-bsz // SUBLANE) * SUBLANE
    c_pad = jnp.zeros((rows, d), F32).at[:bsz].set(c)
    out = pl.pallas_call(
        _ada_kernel,
        grid=(depth, n // d),
        in_specs=[pl.BlockSpec((rows, d), lambda l, j: (0, 0)),
                  pl.BlockSpec((None, d, d), lambda l, j: (l, 0, j)),
                  pl.BlockSpec((None, 1, d), lambda l, j: (l, 0, j))],
        out_specs=pl.BlockSpec((None, rows, d), lambda l, j: (l, 0, j)),
        out_shape=jax.ShapeDtypeStruct((depth, rows, n), F32),
        compiler_params=_params("parallel", "parallel"),
        name="ada_mod",
    )(c_pad, ada_w, ada_b.reshape(depth, 1, n))
    return out[:, :bsz]


def _in_proj_kernel(x_ref, gain_ref, sh_ref, sc_ref, w_ref, o_ref):
    h = _rms(x_ref[...], NORM_EPS) * gain_ref[...] * (1.0 + sc_ref[...]) + sh_ref[...]
    o_ref[...] = _dot(h.astype(BF16), w_ref[...])


def _in_proj(x, gain, mod, w_bf16, tm=256):
    bsz, s, d = x.shape
    n = w_bf16.shape[1]
    return pl.pallas_call(
        _in_proj_kernel,
        grid=(bsz, s // tm),
        in_specs=[pl.BlockSpec((None, tm, d), lambda b, i: (b, i, 0)),
                  pl.BlockSpec((1, d), lambda b, i: (0, 0)),
                  pl.BlockSpec((None, 1, d), lambda b, i: (b, 0, 0)),
                  pl.BlockSpec((None, 1, d), lambda b, i: (b, 0, 1)),
                  pl.BlockSpec((d, n), lambda b, i: (0, 0))],
        out_specs=pl.BlockSpec((None, tm, n), lambda b, i: (b, i, 0)),
        out_shape=jax.ShapeDtypeStruct((bsz, s, n), F32),
        compiler_params=_params("parallel", "parallel"),
        name="in_proj",
    )(x, gain.reshape(1, d), mod, mod, w_bf16)


class _ChunkMasks:
    def __init__(self, n, chunk, reverse):
        ii = lax.broadcasted_iota(jnp.int32, (n, n), 0)
        jj = lax.broadcasted_iota(jnp.int32, (n, n), 1)
        same = (ii // chunk) == (jj // chunk)
        tri = (ii <= jj) if reverse else (ii >= jj)
        self.same = same
        self.incl = same & tri
        self.strict = self.incl & (ii != jj)
        self.eye = (ii == jj).astype(F32)
        m16 = (ii // 16) == (jj // 16)
        m32 = (ii // 32) == (jj // 32)
        self.m16 = m16.astype(F32)
        self.m32_only = (m32 & ~m16).astype(F32)
        self.m64_only = (same & ~m32).astype(F32)


def _unit_tri_inverses(lms, masks):
    ps = [-(lm * masks.m16) for lm in lms]
    ts = [masks.eye + p for p in ps]
    for _ in range(3):
        ps = [_dot(p, p) for p in ps]
        ts = [t + _dot(t, p) for t, p in zip(ts, ps)]
    for level in (masks.m32_only, masks.m64_only):
        cs = [_dot(lm * level, t) for lm, t in zip(lms, ts)]
        ts = [t - _dot(t, c) for t, c in zip(ts, cs)]
    return ts


def _dn_prep_kernel(cur_ref, prev_ref, next_ref, gates_ref, convw_ref, alog_ref, dtb_ref,
                    qkv_ref, gb_ref, ext_ref, *, ts):
    i = pl.program_id(1)
    n = pl.num_programs(1)
    halo = SUBLANE
    ext_ref[0:halo, :] = jnp.where(i > 0, prev_ref[...], 0.0)
    ext_ref[halo:halo + ts, :] = cur_ref[...]
    ext_ref[halo + ts:2 * halo + ts, :] = jnp.where(i < n - 1, next_ref[...], 0.0)
    pad = DN_CONV // 2
    for blk in range(3 * DN_HEADS):
        cols = slice(blk * DN_HEAD_DIM, (blk + 1) * DN_HEAD_DIM)
        acc = jnp.zeros((ts, DN_HEAD_DIM), F32)
        for tap in range(DN_CONV):
            acc = acc + ext_ref[halo - pad + tap:halo - pad + tap + ts, cols] * convw_ref[tap:tap + 1, cols]
        val = _silu(acc)
        if blk < 2 * DN_HEADS:
            val = val * lax.rsqrt(jnp.sum(val * val, axis=-1, keepdims=True) + 1e-6)
            if blk < DN_HEADS:
                val = val * (DN_HEAD_DIM ** -0.5)
        qkv_ref[:, cols] = val
    gin = gates_ref[...]
    g = -jnp.exp(alog_ref[...]) * _softplus(gin + dtb_ref[...])
    lane = lax.broadcasted_iota(jnp.int32, gin.shape, 1)
    gb_ref[...] = jnp.where(lane < 2 * DN_HEADS, g, _sigmoid(gin))


def _dn_prep(p0, conv_w, a_log, dt_bias, ts=512):
    bsz, s, _ = p0.shape
    c3 = 3 * DN_WIDTH
    gate_blk = (4 * DN_WIDTH + S5_WIDTH) // LANE
    nb8 = s // SUBLANE
    r8 = ts // SUBLANE
    alog = jnp.zeros((1, LANE), F32).at[0, :2 * DN_HEADS].set(a_log.reshape(-1))
    dtb = jnp.zeros((1, LANE), F32).at[0, :2 * DN_HEADS].set(dt_bias.reshape(-1))
    return pl.pallas_call(
        functools.partial(_dn_prep_kernel, ts=ts),
        grid=(bsz, s // ts),
        in_specs=[pl.BlockSpec((None, ts, c3), lambda b, i: (b, i, 0)),
                  pl.BlockSpec((None, SUBLANE, c3), lambda b, i: (b, jnp.maximum(i * r8 - 1, 0), 0)),
                  pl.BlockSpec((None, SUBLANE, c3), lambda b, i: (b, jnp.minimum((i + 1) * r8, nb8 - 1), 0)),
                  pl.BlockSpec((None, ts, LANE), lambda b, i: (b, i, gate_blk)),
                  pl.BlockSpec((DN_CONV, c3), lambda b, i: (0, 0)),
                  pl.BlockSpec((1, LANE), lambda b, i: (0, 0)),
                  pl.BlockSpec((1, LANE), lambda b, i: (0, 0))],
        out_specs=[pl.BlockSpec((None, ts, c3), lambda b, i: (b, i, 0)),
                   pl.BlockSpec((None, ts, LANE), lambda b, i: (b, i, 0))],
        out_shape=[jax.ShapeDtypeStruct((bsz, s, c3), F32),
                   jax.ShapeDtypeStruct((bsz, s, LANE), F32)],
        scratch_shapes=[pltpu.VMEM((ts + 2 * SUBLANE, c3), F32)],
        compiler_params=_params("parallel", "parallel"),
        name="dn_prep",
    )(p0, p0, p0, p0, conv_w, alog, dtb)


def _dn_direction(q_ref, k_ref, v_ref, gb_ref, gbt_ref, o_ref, state_ref, *, sc, reverse, direction):
    chunk = DN_CHUNK
    nck = sc // chunk
    masks = _ChunkMasks(sc, chunk, reverse)
    bd_incl = masks.incl.astype(F32)
    gcols = gb_ref[...]
    gc_cols = _dot(bd_incl, gcols, HI)
    gt_cols = _dot(masks.same.astype(F32), gcols, HI)
    gc_rows = _dot_nt(gbt_ref[...], bd_incl, HI)
    e_gc = jnp.exp(gc_cols)
    e_rest = jnp.exp(gt_cols - gc_cols)
    e_tot = jnp.exp(gt_cols)
    order = range(nck - 1, -1, -1) if reverse else range(nck)
    steps = []
    for h in range(DN_HEADS):
        gi = direction * DN_HEADS + h
        bi = 2 * DN_HEADS + gi
        cols = slice(h * DN_HEAD_DIM, (h + 1) * DN_HEAD_DIM)
        q = q_ref[:, cols]
        k = k_ref[:, cols]
        beta = gcols[:, bi:bi + 1]
        eg = e_gc[:, gi:gi + 1]
        decay = jnp.exp(jnp.where(masks.incl, gc_cols[:, gi:gi + 1] - gc_rows[gi:gi + 1, :], NEG_BIG))
        kb = k * beta
        attn = _dot_nt(q, k) * decay
        steps.append(dict(
            lm=jnp.where(masks.strict, _dot_nt(kb, k) * decay, 0.0), vb=v_ref[:, cols] * beta, kbg=kb * eg,
            attn=[attn[c * chunk:(c + 1) * chunk, c * chunk:(c + 1) * chunk] for c in range(nck)], qd=q * eg,
            kd=k * e_rest[:, gi:gi + 1], gl=e_tot[:, gi:gi + 1], cols=cols, o_ref=o_ref, state_ref=state_ref, h=h,
            order=order))
    for st, t in zip(steps, _unit_tri_inverses([st.pop('lm') for st in steps], masks)):
        st['u'] = _dot(t, st.pop('vb'))
        st['w'] = _dot(t, st.pop('kbg'))
    return steps


def _dn_recurrence(steps, chunk):
    states = [st['state_ref'][st['h']] for st in steps]
    for pos in range(len(steps[0]['order'])):
        for n, st in enumerate(steps):
            c = st['order'][pos]
            rows = slice(c * chunk, (c + 1) * chunk)
            v_new = st['u'][rows] - _dot(st['w'][rows], states[n])
            st['o_ref'][rows, st['cols']] = _dot(st['qd'][rows], states[n]) + _dot(st['attn'][c], v_new)
            states[n] = states[n] * st['gl'][c * chunk:c * chunk + 1] + _dot_tn(st['kd'][rows], v_new)
    for n, st in enumerate(steps):
        st['state_ref'][st['h']] = states[n]


def _dn_chunk_kernel(qf, kf, vf, gbf, gbtf, qr, kr, vr, gbr, gbtr, of_ref, or_ref, sf_ref, sr_ref, *, sc):
    @pl.when(pl.program_id(1) == 0)
    def _():
        sf_ref[...] = jnp.zeros_like(sf_ref)
        sr_ref[...] = jnp.zeros_like(sr_ref)

    steps = (_dn_direction(qf, kf, vf, gbf, gbtf, of_ref, sf_ref, sc=sc, reverse=False, direction=0)
             + _dn_direction(qr, kr, vr, gbr, gbtr, or_ref, sr_ref, sc=sc, reverse=True, direction=1))
    _dn_recurrence(steps, DN_CHUNK)


def _dn_chunked(qkv, gb, sc=256):
    bsz, s, _ = qkv.shape
    n = s // sc
    gbt = jnp.swapaxes(gb[..., :4 * DN_HEADS], 1, 2)
    fwd = lambda col: pl.BlockSpec((None, sc, DN_WIDTH), lambda b, i: (b, i, col))
    rev = lambda col: pl.BlockSpec((None, sc, DN_WIDTH), lambda b, i: (b, n - 1 - i, col))
    in_specs = ([fwd(0), fwd(1), fwd(2),
                 pl.BlockSpec((None, sc, LANE), lambda b, i: (b, i, 0)),
                 pl.BlockSpec((None, 4 * DN_HEADS, sc), lambda b, i: (b, 0, i))]
                + [rev(0), rev(1), rev(2),
                   pl.BlockSpec((None, sc, LANE), lambda b, i: (b, n - 1 - i, 0)),
                   pl.BlockSpec((None, 4 * DN_HEADS, sc), lambda b, i: (b, 0, n - 1 - i))])
    state = pltpu.VMEM((DN_HEADS, DN_HEAD_DIM, DN_HEAD_DIM), F32)
    return pl.pallas_call(
        functools.partial(_dn_chunk_kernel, sc=sc),
        grid=(bsz, n),
        in_specs=in_specs,
        out_specs=[pl.BlockSpec((None, sc, DN_WIDTH), lambda b, i: (b, i, 0)),
                   pl.BlockSpec((None, sc, DN_WIDTH), lambda b, i: (b, n - 1 - i, 0))],
        out_shape=[jax.ShapeDtypeStruct((bsz, s, DN_WIDTH), F32)] * 2,
        scratch_shapes=[state, state],
        compiler_params=_params("parallel", "arbitrary"),
        name="dn_chunked",
    )(qkv, qkv, qkv, gb, gbt, qkv, qkv, qkv, gb, gbt)


def _dn_post_kernel(of_ref, or_ref, z_ref, nw_ref, o_ref):
    for h in range(DN_HEADS):
        cols = slice(h * DN_HEAD_DIM, (h + 1) * DN_HEAD_DIM)
        o = of_ref[:, cols] + or_ref[:, cols]
        o_ref[:, cols] = _rms(o, NORM_EPS) * nw_ref[...] * _silu(z_ref[:, cols])


def _dn_post(o_f, o_r, p0, norm_w, ts=512):
    bsz, s, w = o_f.shape
    z_blk = 3 * DN_WIDTH // w
    spec = pl.BlockSpec((None, ts, w), lambda b, i: (b, i, 0))
    return pl.pallas_call(
        _dn_post_kernel,
        grid=(bsz, s // ts),
        in_specs=[spec, spec,
                  pl.BlockSpec((None, ts, w), lambda b, i: (b, i, z_blk)),
                  pl.BlockSpec((1, DN_HEAD_DIM), lambda b, i: (0, 0))],
        out_specs=spec,
        out_shape=jax.ShapeDtypeStruct((bsz, s, w), F32),
        compiler_params=_params("parallel", "parallel"),
        name="dn_post",
    )(o_f, o_r, p0, norm_w.reshape(1, DN_HEAD_DIM))


def _gated_deltanet(p0, conv_w, a_log, dt_bias, norm_w):
    qkv, gb = _dn_prep(p0, conv_w, a_log, dt_bias)
    o_f, o_r = _dn_chunked(qkv, gb)
    return _dn_post(o_f, o_r, p0, norm_w)


def _s5_tables(a_re, a_im, log_dt, b_re, b_im, c_re, c_im, n_chunks):
    lc, p, hh = S5_CHUNK, S5_STATE, S5_GROUP
    levels = int(math.log2(n_chunks))
    dt = jnp.exp(log_dt)[..., None]
    mag = jnp.exp(a_re * dt)
    lb_re, lb_im = mag * jnp.cos(a_im * dt), mag * jnp.sin(a_im * dt)
    den = a_re * a_re + a_im * a_im
    nr, ni = lb_re - 1.0, lb_im
    f_re = (nr * a_re + ni * a_im) / den
    f_im = (ni * a_re - nr * a_im) / den
    bb_re = f_re[..., None] * b_re - f_im[..., None] * b_im
    bb_im = f_re[..., None] * b_im + f_im[..., None] * b_re

    def power(tau):
        tau = jnp.asarray(tau, F32)
        m = jnp.exp(a_re[..., None] * dt[..., None] * tau)
        ang = a_im[..., None] * dt[..., None] * tau
        return m * jnp.cos(ang), m * jnp.sin(ang)

    pw_re, pw_im = power(np.arange(lc + 1))
    lbb_re = pw_re[..., None] * bb_re[:, :, :, None, :] - pw_im[..., None] * bb_im[:, :, :, None, :]
    lbb_im = pw_re[..., None] * bb_im[:, :, :, None, :] + pw_im[..., None] * bb_re[:, :, :, None, :]
    kt = (jnp.einsum('dgop,dgpti->dgtoi', c_re, lbb_re, precision=HI)
          - jnp.einsum('dgop,dgpti->dgtoi', c_im, lbb_im, precision=HI))
    ti = np.arange(lc)[:, None]
    to = np.arange(lc)[None, :]
    toeps, bmats, cmats = [], [], []
    for d in range(2):
        lag = (to - ti) if d == 0 else (ti - to)
        valid = jnp.asarray(lag >= 0, F32)
        blk = kt[d][:, np.clip(lag, 0, lc - 1)] * valid[None, :, :, None, None]
        toeps.append(jnp.transpose(blk, (0, 1, 4, 2, 3)).reshape(-1, lc * hh, lc * hh))
        e_in = (lc - 1 - np.arange(lc)) if d == 0 else np.arange(lc)
        bm_re = jnp.transpose(lbb_re[d][:, :, e_in, :], (0, 2, 3, 1))
        bm_im = jnp.transpose(lbb_im[d][:, :, e_in, :], (0, 2, 3, 1))
        bmats.append(jnp.concatenate([bm_re, bm_im], axis=-1).reshape(-1, lc * hh, 2 * p))
        e_out = (np.arange(lc) + 1) if d == 0 else (lc - np.arange(lc))
        m_re = c_re[d][:, :, :, None] * pw_re[d][:, None, :, e_out] - c_im[d][:, :, :, None] * pw_im[d][:, None, :, e_out]
        m_im = c_re[d][:, :, :, None] * pw_im[d][:, None, :, e_out] + c_im[d][:, :, :, None] * pw_re[d][:, None, :, e_out]
        cm = jnp.concatenate([jnp.transpose(m_re, (0, 2, 3, 1)), -jnp.transpose(m_im, (0, 2, 3, 1))], axis=1)
        cmats.append(cm.reshape(-1, 2 * p, lc * hh))
    sq_re, sq_im = power(lc * 2.0 ** np.arange(levels))
    scan_a = jnp.transpose(jnp.concatenate([sq_re, sq_re], axis=2), (0, 1, 3, 2))
    scan_b = jnp.transpose(jnp.concatenate([-sq_im, sq_im], axis=2), (0, 1, 3, 2))
    return jnp.stack(toeps), jnp.stack(bmats), jnp.stack(cmats), scan_a, scan_b


def _s5_kernel(u_ref, toep_ref, bmat_ref, cmat_ref, sa_ref, sb_ref, y_ref, *, n_chunks):
    u = u_ref[...]
    n = u.shape[0]
    p2 = 2 * S5_STATE
    levels = sa_ref.shape[1]
    cidx = lax.broadcasted_iota(jnp.int32, (n, p2), 0) & (n_chunks - 1)
    y = None
    for d in range(2):
        x = _dot(u, bmat_ref[d])
        for lv in range(levels):
            sh = 1 << lv
            if d == 0:
                xs = jnp.where(cidx >= sh, pltpu.roll(x, sh, 0), 0.0)
            else:
                xs = jnp.where(cidx < n_chunks - sh, pltpu.roll(x, n - sh, 0), 0.0)
            x = x + sa_ref[d, lv:lv + 1, :] * xs + sb_ref[d, lv:lv + 1, :] * pltpu.roll(xs, S5_STATE, 1)
        if d == 0:
            x_in = jnp.where(cidx >= 1, pltpu.roll(x, 1, 0), 0.0)
        else:
            x_in = jnp.where(cidx < n_chunks - 1, pltpu.roll(x, n - 1, 0), 0.0)
        yd = _dot(u, toep_ref[d]) + _dot(x_in, cmat_ref[d])
        y = yd if y is None else y + yd
    y_ref[...] = y


def _s5_scan(u, tables):
    bsz, s, _ = u.shape
    lc, g, hh = S5_CHUNK, S5_GROUPS, S5_GROUP
    nc = s // lc
    toep, bmat, cmat, sa, sb = tables
    uc = jnp.transpose(u.reshape(bsz, nc, lc, g, hh), (3, 0, 1, 2, 4)).reshape(g, bsz * nc, lc * hh)
    levels = sa.shape[2]
    y = pl.pallas_call(
        functools.partial(_s5_kernel, n_chunks=nc),
        grid=(g,),
        in_specs=[pl.BlockSpec((None, bsz * nc, lc * hh), lambda i: (i, 0, 0)),
                  pl.BlockSpec((2, None, lc * hh, lc * hh), lambda i: (0, i, 0, 0)),
                  pl.BlockSpec((2, None, lc * hh, 2 * S5_STATE), lambda i: (0, i, 0, 0)),
                  pl.BlockSpec((2, None, 2 * S5_STATE, lc * hh), lambda i: (0, i, 0, 0)),
                  pl.BlockSpec((2, None, levels, 2 * S5_STATE), lambda i: (0, i, 0, 0)),
                  pl.BlockSpec((2, None, levels, 2 * S5_STATE), lambda i: (0, i, 0, 0))],
        out_specs=pl.BlockSpec((None, bsz * nc, lc * hh), lambda i: (i, 0, 0)),
        out_shape=jax.ShapeDtypeStruct((g, bsz * nc, lc * hh), F32),
        compiler_params=_params("parallel"),
        name="s5_scan",
    )(uc, toep, bmat, cmat, sa, sb)
    return jnp.transpose(y.reshape(g, bsz, nc, lc, hh), (1, 2, 3, 0, 4)).reshape(bsz, s, g * hh)


def _gelu_tanh(x):
    return 0.5 * x * (1.0 + jnp.tanh(math.sqrt(2.0 / math.pi) * (x + 0.044715 * (x * x * x))))


def _s5_post_kernel(u_ref, y_ref, d_ref, w_ref, b_ref, o_ref):
    y = _gelu_tanh(d_ref[...] * u_ref[...] + y_ref[...])
    o_ref[...] = y * _sigmoid(_dot(y, w_ref[...]) + b_ref[...])


def _s5_post(p0, y, d_skip, glu_w, glu_b, ts=512):
    bsz, s, w = y.shape
    u_blk = 4 * DN_WIDTH // w
    spec = pl.BlockSpec((None, ts, w), lambda b, i: (b, i, 0))
    row = pl.BlockSpec((1, w), lambda b, i: (0, 0))
    return pl.pallas_call(
        _s5_post_kernel,
        grid=(bsz, s // ts),
        in_specs=[pl.BlockSpec((None, ts, w), lambda b, i: (b, i, u_blk)), spec, row,
                  pl.BlockSpec((w, w), lambda b, i: (0, 0)), row],
        out_specs=spec,
        out_shape=jax.ShapeDtypeStruct((bsz, s, w), F32),
        compiler_params=_params("parallel", "parallel"),
        name="s5_post",
    )(p0, y, d_skip.reshape(1, w), glu_w, glu_b.reshape(1, w))


def _s5_mixer(p0, a_re, a_im, log_dt, b_re, b_im, c_re, c_im, d_skip, glu_w, glu_b):
    s = p0.shape[1]
    u = p0[..., 4 * DN_WIDTH:4 * DN_WIDTH + S5_WIDTH]
    tables = _s5_tables(a_re, a_im, log_dt, b_re, b_im, c_re, c_im, s // S5_CHUNK)
    return _s5_post(p0, _s5_scan(u, tables), d_skip, glu_w, glu_b)


def _mix_out_kernel(ya_ref, yb_ref, x_ref, w_ref, npost_ref, g1_ref, npre_ref, sh2_ref, sc2_ref, rwt_ref,
                    x1_ref, h2_ref, pt_ref):
    half = ya_ref.shape[-1]
    y = _dot(ya_ref[...].astype(BF16), w_ref[0:half, :]) + _dot(yb_ref[...].astype(BF16), w_ref[half:2 * half, :])
    x1 = x_ref[...] + g1_ref[...] * (_rms(y, NORM_EPS) * npost_ref[...])
    x1_ref[...] = x1
    h2 = _rms(x1, NORM_EPS) * npre_ref[...] * (1.0 + sc2_ref[...]) + sh2_ref[...]
    h2_ref[...] = h2.astype(BF16)
    logits = _dot_nt(rwt_ref[...], h2, HI)
    ex = jnp.exp(logits - jnp.max(logits, axis=0, keepdims=True))
    pt_ref[...] = ex / jnp.sum(ex, axis=0, keepdims=True)


def _mix_out(ya, yb, x, w_out_bf16, npost, mod, npre, router_w, tm=256):
    bsz, s, d = x.shape
    half = ya.shape[-1]
    ne = router_w.shape[-1]
    row = lambda a: a.reshape(1, d)
    vec = pl.BlockSpec((1, d), lambda b, i: (0, 0))
    modblk = lambda k: pl.BlockSpec((None, 1, d), lambda b, i: (b, 0, k))
    tile = pl.BlockSpec((None, tm, d), lambda b, i: (b, i, 0))
    halfspec = pl.BlockSpec((None, tm, half), lambda b, i: (b, i, 0))
    return pl.pallas_call(
        _mix_out_kernel,
        grid=(bsz, s // tm),
        in_specs=[halfspec, halfspec, tile,
                  pl.BlockSpec((d, d), lambda b, i: (0, 0)), vec, modblk(2), vec, modblk(3), modblk(4),
                  pl.BlockSpec((ne, d), lambda b, i: (0, 0))],
        out_specs=[tile, tile, pl.BlockSpec((None, ne, tm), lambda b, i: (b, 0, i))],
        out_shape=[jax.ShapeDtypeStruct((bsz, s, d), F32),
                   jax.ShapeDtypeStruct((bsz, s, d), BF16),
                   jax.ShapeDtypeStruct((bsz, ne, s), F32)],
        compiler_params=_params("parallel", "parallel"),
        name="mix_out",
    )(ya, yb, x, w_out_bf16, row(npost), mod, row(npre), mod, mod, router_w.T)


def _prefix_lanes(x):
    n = x.shape[1]
    lane = lax.broadcasted_iota(jnp.int32, x.shape, 1)
    sh = 1
    while sh < n:
        x = x + jnp.where(lane >= sh, pltpu.roll(x, sh, 1), 0.0)
        sh *= 2
    return x


def _topk_kernel(p_ref, slot_ref, count_ref, *, cap):
    p = p_ref[...]
    rows = p.shape[0]

    def halve(_, bounds):
        lo, hi = bounds
        mid = 0.5 * (lo + hi)
        ok = jnp.sum((p >= mid).astype(F32), axis=1, keepdims=True) >= cap
        return jnp.where(ok, mid, lo), jnp.where(ok, hi, mid)

    lo, _ = lax.fori_loop(0, TOPK_BISECTIONS, halve, (jnp.zeros((rows, 1), F32), jnp.full((rows, 1), 2.0, F32)))
    gt = p > lo
    eq = p == lo
    need = cap - jnp.sum(gt.astype(F32), axis=1, keepdims=True)
    sel = gt | (eq & (_prefix_lanes(eq.astype(F32)) <= need))
    count = _prefix_lanes(sel.astype(F32))
    slot_ref[...] = jnp.where(sel, count - 1.0, -1.0).astype(jnp.int32)
    count_ref[...] = count.astype(jnp.int32)


def _topk_slots(pt, cap):
    bsz, ne, s = pt.shape
    spec = pl.BlockSpec((None, ne, s), lambda b: (b, 0, 0))
    return pl.pallas_call(
        functools.partial(_topk_kernel, cap=cap),
        grid=(bsz,), in_specs=[spec], out_specs=[spec, spec],
        out_shape=[jax.ShapeDtypeStruct((bsz, ne, s), jnp.int32)] * 2,
        compiler_params=_params("parallel"),
        name="topk_slots",
    )(pt)


def _tile_starts(count, tile):
    bsz, ne, _ = count.shape
    ends = count[:, :, tile - 1::tile]
    return jnp.concatenate([jnp.zeros((bsz, ne, 1), jnp.int32), ends], axis=2).reshape(-1)


def _gather_kernel(starts_ref, slot_ref, p_ref, h_ref, xin_ref, gate_ref, acc_ref, gacc_ref, *, cap, tk, win):
    nk = h_ref.shape[0] // tk
    base = (pl.program_id(0) * pl.num_programs(1) + pl.program_id(1)) * (nk + 1)
    acc_ref[...] = jnp.zeros_like(acc_ref)
    gacc_ref[...] = jnp.zeros_like(gacc_ref)
    shift = win.bit_length() - 1
    for k in range(nk):
        first = (starts_ref[base + k] >> 3) << 3
        n_win = (starts_ref[base + k + 1] - first + win - 1) >> shift
        slot_k = slot_ref[:, k * tk:(k + 1) * tk]
        p_k = p_ref[:, k * tk:(k + 1) * tk]
        h_k = h_ref[k * tk:(k + 1) * tk, :]

        def window(w, carry):
            row0 = pl.multiple_of(first + w * win, SUBLANE)
            hit = (row0 + lax.broadcasted_iota(jnp.int32, (win, tk), 0)) == slot_k
            acc_ref[pl.ds(row0, win), :] += _dot(hit.astype(BF16), h_k)
            gacc_ref[pl.ds(row0, win), :] += jnp.sum(jnp.where(hit, p_k, 0.0), axis=1, keepdims=True)
            return carry

        lax.fori_loop(0, n_win, window, 0)
    xin_ref[...] = acc_ref[0:cap, :].astype(BF16)
    gate_ref[...] = gacc_ref[0:cap, :]


def _moe_gather(starts, slot, pt, h_bf16, cap, tk, win=64):
    bsz, ne, s = slot.shape
    d = h_bf16.shape[-1]
    rowspec = pl.BlockSpec((None, None, 1, s), lambda b, e, st: (b, e, 0, 0))
    return pl.pallas_call(
        functools.partial(_gather_kernel, cap=cap, tk=tk, win=win),
        grid_spec=pltpu.PrefetchScalarGridSpec(
            num_scalar_prefetch=1,
            grid=(bsz, ne),
            in_specs=[rowspec, rowspec, pl.BlockSpec((None, s, d), lambda b, e, st: (b, 0, 0))],
            out_specs=[pl.BlockSpec((None, None, cap, d), lambda b, e, st: (b, e, 0, 0)),
                       pl.BlockSpec((None, None, cap, 1), lambda b, e, st: (b, e, 0, 0))],
            scratch_shapes=[pltpu.VMEM((cap + win, d), F32), pltpu.VMEM((cap + win, 1), F32)]),
        out_shape=[jax.ShapeDtypeStruct((bsz, ne, cap, d), BF16),
                   jax.ShapeDtypeStruct((bsz, ne, cap, 1), F32)],
        compiler_params=_params("parallel", "arbitrary"),
        name="moe_gather",
    )(starts, slot.reshape(bsz, ne, 1, s), pt.reshape(bsz, ne, 1, s), h_bf16)


def _ffn_kernel(xin_ref, gate_ref, wg_ref, wu_ref, wd_ref, y_ref, acc_ref):
    f = pl.program_id(1)
    bsz, cap, d = xin_ref.shape
    x = xin_ref[...].reshape(bsz * cap, d)
    hid = _silu(_dot(x, wg_ref[...].astype(BF16))) * _dot(x, wu_ref[...].astype(BF16))
    part = _dot(hid.astype(BF16), wd_ref[...].astype(BF16))

    @pl.when(f == 0)
    def _():
        acc_ref[...] = part

    @pl.when(f > 0)
    def _():
        acc_ref[...] += part

    @pl.when(f == pl.num_programs(1) - 1)
    def _():
        y_ref[...] = (acc_ref[...] * gate_ref[...].reshape(bsz * cap, 1)).astype(BF16).reshape(bsz, cap, d)


def _moe_ffn(xin, gate, w_gate, w_up, w_down, layer, tf=256):
    bsz, ne, cap, d = xin.shape
    dexp = w_gate.shape[-1]
    return pl.pallas_call(
        _ffn_kernel,
        grid=(ne, dexp // tf),
        in_specs=[pl.BlockSpec((bsz, None, cap, d), lambda e, f: (0, e, 0, 0)),
                  pl.BlockSpec((bsz, None, cap, 1), lambda e, f: (0, e, 0, 0)),
                  pl.BlockSpec((None, None, d, tf), lambda e, f: (layer, e, 0, f)),
                  pl.BlockSpec((None, None, d, tf), lambda e, f: (layer, e, 0, f)),
                  pl.BlockSpec((None, None, tf, d), lambda e, f: (layer, e, f, 0))],
        out_specs=pl.BlockSpec((bsz, None, cap, d), lambda e, f: (0, e, 0, 0)),
        out_shape=jax.ShapeDtypeStruct((bsz, ne, cap, d), BF16),
        scratch_shapes=[pltpu.VMEM((bsz * cap, d), F32)],
        compiler_params=_params("parallel", "arbitrary"),
        name="moe_ffn",
    )(xin, gate, w_gate, w_up, w_down)


def _scatter_kernel(starts_ref, slot_ref, y_ref, x_ref, g2_ref, npost_ref, o_ref, ywin_ref, hit_ref, acc_ref,
                    *, cap, win):
    ne = y_ref.shape[0]
    ts = slot_ref.shape[1]
    nk = pl.num_programs(1)
    b = pl.program_id(0)
    i = pl.program_id(1)
    shift = win.bit_length() - 1
    pack = 2 * SUBLANE
    firsts, n_pass = [], 0
    for e in range(ne):
        base = (b * ne + e) * (nk + 1) + i
        first = (starts_ref[base] // pack) * pack
        firsts.append(first)
        n_pass = jnp.maximum(n_pass, (starts_ref[base + 1] - first + win - 1) >> shift)
    acc_ref[...] = jnp.zeros_like(acc_ref)

    def one_pass(w, carry):
        for e in range(ne):
            row0 = firsts[e] + w * win
            src = pl.multiple_of(jnp.minimum(row0, cap - win), pack)
            rows = src + lax.broadcasted_iota(jnp.int32, (ts, win), 1)
            hit_ref[:, e * win:(e + 1) * win] = ((slot_ref[e] == rows) & (rows >= row0)).astype(BF16)
            ywin_ref[e * win:(e + 1) * win, :] = y_ref[e, pl.ds(src, win), :]
        acc_ref[...] += _dot(hit_ref[...], ywin_ref[...])
        return carry

    lax.fori_loop(0, n_pass, one_pass, 0)
    o_ref[...] = x_ref[...] + g2_ref[...] * (_rms(acc_ref[...], NORM_EPS) * npost_ref[...])


def _moe_scatter(starts, slot, y, x, mod, npost, ts, win=128):
    bsz, ne, cap, d = y.shape
    s = x.shape[1]
    win = min(win, cap)
    tile = pl.BlockSpec((None, ts, d), lambda b, i, st: (b, i, 0))
    return pl.pallas_call(
        functools.partial(_scatter_kernel, cap=cap, win=win),
        grid_spec=pltpu.PrefetchScalarGridSpec(
            num_scalar_prefetch=1,
            grid=(bsz, s // ts),
            in_specs=[pl.BlockSpec((None, ne, ts, 1), lambda b, i, st: (b, 0, i, 0)),
                      pl.BlockSpec((None, ne, cap, d), lambda b, i, st: (b, 0, 0, 0)),
                      tile,
                      pl.BlockSpec((None, 1, d), lambda b, i, st: (b, 0, 5)),
                      pl.BlockSpec((1, d), lambda b, i, st: (0, 0))],
            out_specs=tile,
            scratch_shapes=[pltpu.VMEM((ne * win, d), BF16), pltpu.VMEM((ts, ne * win), BF16),
                            pltpu.VMEM((ts, d), F32)]),
        out_shape=jax.ShapeDtypeStruct((bsz, s, d), F32),
        compiler_params=_params("parallel", "arbitrary"),
        name="moe_scatter",
    )(starts, slot.reshape(bsz, ne, s, 1), y, x, mod, npost.reshape(1, d))


def _expert_choice_ffn(x1, h2, pt, mod, npost, w_gate, w_up, w_down, layer):
    s = x1.shape[1]
    ne = pt.shape[1]
    cap = EC_CAPACITY_FACTOR * s // ne
    tile = MOE_TOKEN_TILE
    slot, count = _topk_slots(pt, cap)
    starts = _tile_starts(count, tile)
    xin, gate = _moe_gather(starts, slot, pt, h2, cap, tile)
    y = _moe_ffn(xin, gate, w_gate, w_up, w_down, layer)
    return _moe_scatter(starts, slot, y, x1, mod, npost, tile)


def _rw_prep_kernel(cur_ref, prev_ref, next_ref, mu_ref, w0_ref, w2_ref, a0_ref, a2_ref, g2_ref, kkw_ref, ka_ref,
                    hsum_ref, r_ref, v_ref, kk_ref, g_ref, lw0_ref, lw1_ref, kd0_ref, kd1_ref, b0_ref, b1_ref,
                    ext_ref, *, ts):
    i = pl.program_id(1)
    n = pl.num_programs(1)
    halo = SUBLANE
    ext_ref[0:halo, :] = jnp.where(i > 0, prev_ref[...], 0.0)
    ext_ref[halo:halo + ts, :] = cur_ref[...]
    ext_ref[halo + ts:2 * halo + ts, :] = jnp.where(i < n - 1, next_ref[...], 0.0)

    def shifted(cols):
        pf = ext_ref[halo:halo + ts, cols]
        neigh = 0.5 * (ext_ref[halo - 1:halo - 1 + ts, cols] + ext_ref[halo + 1:halo + 1 + ts, cols])
        return pf + mu_ref[:, cols] * (neigh - pf)

    w = RW_WIDTH
    r = shifted(slice(0, w))
    k = shifted(slice(w, 2 * w))
    v = shifted(slice(2 * w, 3 * w))
    lora = shifted(slice(3 * w, 3 * w + 3 * LANE))
    wdec = -_softplus(-(w0_ref[...] + _dot(jnp.tanh(lora[:, 0:LANE]), w2_ref[...]))) - 0.5
    lw = -jnp.exp(wdec)
    a = _sigmoid(a0_ref[...] + _dot(lora[:, LANE:2 * LANE], a2_ref[...]))
    g_ref[...] = _dot(_sigmoid(lora[:, 2 * LANE:3 * LANE]), g2_ref[...])
    kk = k * kkw_ref[...]
    kk = kk * lax.rsqrt(_dot(kk * kk, hsum_ref[...], HI) + 1e-6)
    r_ref[...] = r
    v_ref[...] = v
    kk_ref[...] = kk
    for d, (lw_ref, kd_ref, b_ref) in enumerate(((lw0_ref, kd0_ref, b0_ref), (lw1_ref, kd1_ref, b1_ref))):
        ad = a[:, d * w:(d + 1) * w]
        lw_ref[...] = lw[:, d * w:(d + 1) * w]
        kd_ref[...] = k * (1.0 + (ad - 1.0) * ka_ref[...])
        b_ref[...] = kk * ad


def _block_diag2(m):
    z = jnp.zeros_like(m[0])
    return jnp.concatenate([jnp.concatenate([m[0], z], 1), jnp.concatenate([z, m[1]], 1)], 0)


def _head_sum_matrix(width, head_dim):
    idx = np.arange(width) // head_dim
    return jnp.asarray(idx[:, None] == idx[None, :], F32)


def _rw_prep(p1, mu, w0, w2, a0, a2, g2, k_k, k_a, ts=256):
    bsz, s, _ = p1.shape
    cin = RW_IN
    w = RW_WIDTH
    nb8 = s // SUBLANE
    r8 = ts // SUBLANE
    full = lambda shape: pl.BlockSpec(shape, lambda b, i: (0,) * len(shape))
    out_spec = pl.BlockSpec((None, ts, w), lambda b, i: (b, i, 0))
    return pl.pallas_call(
        functools.partial(_rw_prep_kernel, ts=ts),
        grid=(bsz, s // ts),
        in_specs=[pl.BlockSpec((None, ts, cin), lambda b, i: (b, i, 0)),
                  pl.BlockSpec((None, SUBLANE, cin), lambda b, i: (b, jnp.maximum(i * r8 - 1, 0), 0)),
                  pl.BlockSpec((None, SUBLANE, cin), lambda b, i: (b, jnp.minimum((i + 1) * r8, nb8 - 1), 0)),
                  full((1, cin)), full((1, 2 * w)), full((LANE, 2 * w)), full((1, 2 * w)), full((LANE, 2 * w)),
                  full((RW_GATE_LORA, w)), full((1, w)), full((1, w)), full((w, w))],
        out_specs=[out_spec] * 10,
        out_shape=[jax.ShapeDtypeStruct((bsz, s, w), F32)] * 10,
        scratch_shapes=[pltpu.VMEM((ts + 2 * SUBLANE, cin), F32)],
        compiler_params=_params("parallel", "parallel"),
        name="rw_prep",
    )(p1, p1, p1, mu.reshape(1, cin), w0.reshape(1, 2 * w), _block_diag2(w2), a0.reshape(1, 2 * w),
      _block_diag2(a2), g2, k_k.reshape(1, w), k_a.reshape(1, w), _head_sum_matrix(w, RW_HEAD_DIM))


def _rw_direction(r_ref, v_ref, kk_ref, lw_ref, kd_ref, b_ref, o_ref, state_ref, *, sc, reverse):
    chunk = RW_CHUNK
    nck = sc // chunk
    hd = RW_HEAD_DIM
    masks = _ChunkMasks(sc, chunk, reverse)
    lw = lw_ref[...]
    lg = _dot(masks.incl.astype(F32), lw, HI)
    lt = _dot(masks.same.astype(F32), lw, HI)
    e_neg = jnp.exp(-lg)
    kkt = kk_ref[...] * jnp.exp(lg - lw)
    khat = kd_ref[...] * e_neg
    bhat = b_ref[...] * e_neg
    rt = r_ref[...] * jnp.exp(lg)
    e_rest = jnp.exp(lt - lg)
    kbar = kd_ref[...] * e_rest
    bbar = b_ref[...] * e_rest
    e_tot = jnp.exp(lt)
    order = range(nck - 1, -1, -1) if reverse else range(nck)
    steps = []
    for h in range(RW_HEADS):
        cols = slice(h * hd, (h + 1) * hd)
        v = v_ref[:, cols]
        kkt_h = kkt[:, cols]
        khat_h = khat[:, cols]
        bhat_h = bhat[:, cols]
        rt_h = rt[:, cols]
        l_a = jnp.where(masks.strict, _dot_nt(kkt_h, khat_h), 0.0)
        a_qb = jnp.where(masks.incl, _dot_nt(rt_h, bhat_h), 0.0)
        steps.append(dict(
            l_b=jnp.where(masks.strict, _dot_nt(kkt_h, bhat_h), 0.0), kkt=kkt_h, la_v=_dot(l_a, v), rt=rt_h,
            y0=_dot(jnp.where(masks.incl, _dot_nt(rt_h, khat_h), 0.0), v),
            a_qb=[a_qb[c * chunk:(c + 1) * chunk, c * chunk:(c + 1) * chunk] for c in range(nck)],
            v=v, kbar=kbar[:, cols], bbar=bbar[:, cols], gl=e_tot[:, cols],
            cols=cols, o_ref=o_ref, state_ref=state_ref, h=h, order=order))
    for st, t in zip(steps, _unit_tri_inverses([st.pop('l_b') for st in steps], masks)):
        st['wm'] = _dot(t, st.pop('kkt'))
        st['u'] = _dot(t, st.pop('la_v'))
    return steps


def _rw_recurrence(steps, chunk):
    states = [st['state_ref'][st['h']] for st in steps]
    for pos in range(len(steps[0]['order'])):
        for n, st in enumerate(steps):
            c = st['order'][pos]
            rows = slice(c * chunk, (c + 1) * chunk)
            p = _dot_nt(st['wm'][rows], states[n]) + st['u'][rows]
            st['o_ref'][rows, st['cols']] = (st['y0'][rows] + _dot_nt(st['rt'][rows], states[n])
                                             - _dot(st['a_qb'][c], p))
            states[n] = (states[n] * st['gl'][c * chunk:c * chunk + 1] + _dot_tn(st['v'][rows], st['kbar'][rows])
                         - _dot_tn(p, st['bbar'][rows]))
    for n, st in enumerate(steps):
        st['state_ref'][st['h']] = states[n]


def _rw_chunk_kernel(rf, vf, kkf, lwf, kdf, bf, rr, vr, kkr, lwr, kdr, br, of_ref, or_ref, sf_ref, sr_ref, *, sc):
    @pl.when(pl.program_id(1) == 0)
    def _():
        sf_ref[...] = jnp.zeros_like(sf_ref)
        sr_ref[...] = jnp.zeros_like(sr_ref)

    steps = (_rw_direction(rf, vf, kkf, lwf, kdf, bf, of_ref, sf_ref, sc=sc, reverse=False)
             + _rw_direction(rr, vr, kkr, lwr, kdr, br, or_ref, sr_ref, sc=sc, reverse=True))
    _rw_recurrence(steps, RW_CHUNK)


def _rw_chunked(r, v, kk, lw0, lw1, kd0, kd1, b0, b1, sc=256):
    bsz, s, w = r.shape
    n = s // sc
    fwd = pl.BlockSpec((None, sc, w), lambda b, i: (b, i, 0))
    rev = pl.BlockSpec((None, sc, w), lambda b, i: (b, n - 1 - i, 0))
    state = pltpu.VMEM((RW_HEADS, RW_HEAD_DIM, RW_HEAD_DIM), F32)
    return pl.pallas_call(
        functools.partial(_rw_chunk_kernel, sc=sc),
        grid=(bsz, n),
        in_specs=[fwd] * 6 + [rev] * 6,
        out_specs=[fwd, rev],
        out_shape=[jax.ShapeDtypeStruct((bsz, s, w), F32)] * 2,
        scratch_shapes=[state, state],
        compiler_params=_params("parallel", "arbitrary"),
        name="rw_chunked",
    )(r, v, kk, lw0, kd0, b0, r, v, kk, lw1, kd1, b1)


def _rw_post_kernel(yf_ref, yr_ref, r_ref, v_ref, g_ref, kd0_ref, kd1_ref, rk_ref, lnw_ref, lnb_ref, hsum_ref, o_ref):
    inv = 1.0 / RW_HEAD_DIM
    y = yf_ref[...] + yr_ref[...]
    mean = _dot(y, hsum_ref[...], HI) * inv
    cen = y - mean
    var = _dot(cen * cen, hsum_ref[...], HI) * inv
    y = cen * lax.rsqrt(var + RW_GN_EPS) * lnw_ref[...] + lnb_ref[...]
    k_bonus = 0.5 * (kd0_ref[...] + kd1_ref[...])
    y = y + _dot(r_ref[...] * k_bonus * rk_ref[...], hsum_ref[...], HI) * v_ref[...]
    o_ref[...] = y * g_ref[...]


def _rw_post(y_f, y_r, r, v, g, kd0, kd1, r_k, ln_w, ln_b, ts=512):
    bsz, s, w = r.shape
    spec = pl.BlockSpec((None, ts, w), lambda b, i: (b, i, 0))
    row = pl.BlockSpec((1, w), lambda b, i: (0, 0))
    return pl.pallas_call(
        _rw_post_kernel,
        grid=(bsz, s // ts),
        in_specs=[spec] * 7 + [row, row, row, pl.BlockSpec((w, w), lambda b, i: (0, 0))],
        out_specs=spec,
        out_shape=jax.ShapeDtypeStruct((bsz, s, w), F32),
        compiler_params=_params("parallel", "parallel"),
        name="rw_post",
    )(y_f, y_r, r, v, g, kd0, kd1, r_k.reshape(1, w), ln_w.reshape(1, w), ln_b.reshape(1, w),
      _head_sum_matrix(w, RW_HEAD_DIM))


def _rwkv7_mixer(p1, mu, w0, w2, a0, a2, g2, k_k, k_a, r_k, ln_w, ln_b):
    r, v, kk, g, lw0, lw1, kd0, kd1, b0, b1 = _rw_prep(p1, mu, w0, w2, a0, a2, g2, k_k, k_a)
    y_f, y_r = _rw_chunked(r, v, kk, lw0, lw1, kd0, kd1, b0, b1)
    return _rw_post(y_f, y_r, r, v, g, kd0, kd1, r_k, ln_w, ln_b)


def _diff_attn_kernel(q_ref, k_ref, v_ref, lam_ref, sub_ref, o_ref, *, tq, lambda_init):
    h = pl.program_id(1)
    i = pl.program_id(2)
    s = k_ref.shape[0]
    hd = DF_HEAD_DIM
    slope = jnp.where(h == 0, 2.0 ** -2, jnp.where(h == 1, 2.0 ** -4, jnp.where(h == 2, 2.0 ** -6, 2.0 ** -8)))
    lp = lam_ref[...]
    lam = (jnp.exp(jnp.sum(lp[0:1] * lp[1:2], axis=1, keepdims=True))
           - jnp.exp(jnp.sum(lp[2:3] * lp[3:4], axis=1, keepdims=True)) + lambda_init)
    qpos = i * tq + lax.broadcasted_iota(jnp.int32, (tq, s), 0)
    kpos = lax.broadcasted_iota(jnp.int32, (tq, s), 1)
    bias = slope.astype(F32) * jnp.abs(qpos - kpos).astype(F32)
    v = v_ref[...].astype(BF16)
    outs = []
    for m in range(2):
        q = q_ref[:, m * hd:(m + 1) * hd] * (hd ** -0.5)
        sc = _dot_nt(q, k_ref[:, m * hd:(m + 1) * hd]) - bias
        ex = jnp.exp(sc - jnp.max(sc, axis=1, keepdims=True))
        outs.append(_dot(ex, v) / jnp.sum(ex, axis=1, keepdims=True))
    o = outs[0] - lam * outs[1]
    o_ref[...] = _rms(o, 1e-5) * sub_ref[...] * (1.0 - lambda_init)


def _diff_attention(p1, lam_params, subln_w, lambda_init, tq=256):
    bsz, s, _ = p1.shape
    hw = 2 * DF_HEAD_DIM
    q0 = RW_IN // hw
    return pl.pallas_call(
        functools.partial(_diff_attn_kernel, tq=tq, lambda_init=lambda_init),
        grid=(bsz, DF_HEADS, s // tq),
        in_specs=[pl.BlockSpec((None, tq, hw), lambda b, h, i: (b, i, q0 + h)),
                  pl.BlockSpec((None, s, hw), lambda b, h, i: (b, 0, q0 + DF_HEADS + h)),
                  pl.BlockSpec((None, s, hw), lambda b, h, i: (b, 0, q0 + 2 * DF_HEADS + h)),
                  pl.BlockSpec((4, DF_HEAD_DIM), lambda b, h, i: (0, 0)),
                  pl.BlockSpec((1, hw), lambda b, h, i: (0, 0))],
        out_specs=pl.BlockSpec((None, tq, hw), lambda b, h, i: (b, i, h)),
        out_shape=jax.ShapeDtypeStruct((bsz, s, DF_HEADS * hw), F32),
        compiler_params=_params("parallel", "parallel", "parallel"),
        name="diff_attn",
    )(p1, p1, p1, lam_params, subln_w.reshape(1, hw))


def _layer0_weights(ab_w_in):
    d = ab_w_in.shape[0]
    main = 4 * DN_WIDTH
    gates = 4 * DN_HEADS
    w = jnp.concatenate([ab_w_in[:, :main], ab_w_in[:, main + gates:], ab_w_in[:, main:main + gates],
                         jnp.zeros((d, LANE - gates), ab_w_in.dtype)], axis=1)
    return w.astype(BF16)


def kernel(x, c, ada_w, ada_b, norm_mix_pre, norm_mix_post, norm_ffn_pre, norm_ffn_post, router_w, exp_w_gate, exp_w_up, exp_w_down, ab_w_in, ab_w_out, dn_conv, dn_a_log, dn_dt_bias, dn_norm, s5_a_re, s5_a_im, s5_log_dt, s5_b_re, s5_b_im, s5_c_re, s5_c_im, s5_d, s5_glu_w, s5_glu_b, cd_w_in, cd_w_out, rw_mu, rw_w0, rw_w2, rw_a0, rw_a2, rw_g2, rw_k_k, rw_k_a, rw_r_k, rw_ln_w, rw_ln_b, df_lambda, df_subln):
    depth = ada_w.shape[0]
    mod_all = _ada_mod(c, ada_w, ada_b)
    for layer in range(depth):
        mod = mod_all[layer][:, None, :]
        i = layer // 2
        if layer % 2 == 0:
            p = _in_proj(x, norm_mix_pre[layer], mod, _layer0_weights(ab_w_in[i]))
            ya = _gated_deltanet(p, dn_conv[i], dn_a_log[i], dn_dt_bias[i], dn_norm[i])
            yb = _s5_mixer(p, s5_a_re[i], s5_a_im[i], s5_log_dt[i], s5_b_re[i], s5_b_im[i], s5_c_re[i], s5_c_im[i],
                           s5_d[i], s5_glu_w[i], s5_glu_b[i])
            w_out = ab_w_out[i]
        else:
            lambda_init = 0.8 - 0.6 * math.exp(-0.3 * layer)
            p = _in_proj(x, norm_mix_pre[layer], mod, cd_w_in[i].astype(BF16))
            ya = _rwkv7_mixer(p, rw_mu[i], rw_w0[i], rw_w2[i], rw_a0[i], rw_a2[i], rw_g2[i], rw_k_k[i], rw_k_a[i],
                              rw_r_k[i].reshape(-1), rw_ln_w[i], rw_ln_b[i])
            yb = _diff_attention(p, df_lambda[i], df_subln[i], lambda_init)
            w_out = cd_w_out[i]
        x1, h2, pt = _mix_out(ya, yb, x, w_out.astype(BF16), norm_mix_post[layer], mod, norm_ffn_pre[layer],
                              router_w[layer])
        x = _expert_choice_ffn(x1, h2, pt, mod, norm_ffn_post[layer], exp_w_gate, exp_w_up, exp_w_down, layer)
    return x
```

```python
import functools
import math

import jax
import jax.numpy as jnp
import numpy as np
from jax import lax
from jax.experimental import pallas as pl
from jax.experimental.pallas import tpu as pltpu

F32 = jnp.float32
BF16 = jnp.bfloat16
HI = lax.Precision.HIGHEST

LANE = 128
SUBLANE = 8
VMEM_LIMIT = 56 * 1024 * 1024

NORM_EPS = 1e-6
DN_HEADS = 4
DN_HEAD_DIM = 128
DN_WIDTH = 512
DN_CONV = 5
DN_CHUNK = 64
S5_WIDTH = 512
S5_GROUP = 16
S5_GROUPS = 32
S5_STATE = 64
S5_CHUNK = 16
RW_WIDTH = 512
RW_HEADS = 8
RW_HEAD_DIM = 64
RW_LORA = 64
RW_GATE_LORA = 128
RW_GN_EPS = 64e-5
RW_CHUNK = 64
RW_IN = 1920
DF_WIDTH = 512
DF_HEADS = 4
DF_HEAD_DIM = 64
N_EXPERTS = 16
EC_CAPACITY_FACTOR = 2
NEG_BIG = -1e30
MOE_TOKEN_TILE = 256
TOPK_BISECTIONS = 152


def _params(*sem):
    return pltpu.CompilerParams(dimension_semantics=sem, vmem_limit_bytes=VMEM_LIMIT)


def _sigmoid(x):
    return 1.0 / (1.0 + jnp.exp(-x))


def _silu(x):
    return x * _sigmoid(x)


def _softplus(x):
    return jnp.maximum(x, 0.0) + jnp.log(1.0 + jnp.exp(-jnp.abs(x)))


def _rms(x, eps):
    return x * lax.rsqrt(jnp.mean(x * x, axis=-1, keepdims=True) + eps)


def _mxu_operands(a, b, precision):
    if precision is None:
        return a.astype(BF16), b.astype(BF16)
    return a, b


def _dot(a, b, precision=None):
    a, b = _mxu_operands(a, b, precision)
    return jnp.dot(a, b, preferred_element_type=F32, precision=precision)


def _dot_nt(a, b, precision=None):
    a, b = _mxu_operands(a, b, precision)
    return lax.dot_general(a, b, (((1,), (1,)), ((), ())), preferred_element_type=F32, precision=precision)


def _dot_tn(a, b, precision=None):
    a, b = _mxu_operands(a, b, precision)
    return lax.dot_general(a, b, (((0,), (0,)), ((), ())), preferred_element_type=F32, precision=precision)


def _ada_kernel(c_ref, w_ref, b_ref, o_ref):
    o_ref[...] = _dot(_silu(c_ref[...]), w_ref[...], HI) + b_ref[...]


def _ada_mod(c, ada_w, ada_b):
    depth, d, n = ada_w.shape
    bsz = c.shape[0]
    rows = -(-bsz // SUBLANE) * SUBLANE
    c_pad = jnp.zeros((rows, d), F32).at[:bsz].set(c)
    out = pl.pallas_call(
        _ada_kernel,
        grid=(depth, n // d),
        in_specs=[pl.BlockSpec((rows, d), lambda l, j: (0, 0)),
                  pl.BlockSpec((None, d, d), lambda l, j: (l, 0, j)),
                  pl.BlockSpec((None, 1, d), lambda l, j: (l, 0, j))],
        out_specs=pl.BlockSpec((None, rows, d), lambda l, j: (l, 0, j)),
        out_shape=jax.ShapeDtypeStruct((depth, rows, n), F32),
        compiler_params=_params("parallel", "parallel"),
        name="ada_mod",
    )(c_pad, ada_w, ada_b.reshape(depth, 1, n))
    return out[:, :bsz]


def _in_proj_kernel(x_ref, gain_ref, sh_ref, sc_ref, w_ref, o_ref):
    h = _rms(x_ref[...], NORM_EPS) * gain_ref[...] * (1.0 + sc_ref[...]) + sh_ref[...]
    o_ref[...] = _dot(h.astype(BF16), w_ref[...])


def _in_proj(x, gain, mod, w_bf16, tm=256):
    bsz, s, d = x.shape
    n = w_bf16.shape[1]
    return pl.pallas_call(
        _in_proj_kernel,
        grid=(bsz, s // tm),
        in_specs=[pl.BlockSpec((None, tm, d), lambda b, i: (b, i, 0)),
                  pl.BlockSpec((1, d), lambda b, i: (0, 0)),
                  pl.BlockSpec((None, 1, d), lambda b, i: (b, 0, 0)),
                  pl.BlockSpec((None, 1, d), lambda b, i: (b, 0, 1)),
                  pl.BlockSpec((d, n), lambda b, i: (0, 0))],
        out_specs=pl.BlockSpec((None, tm, n), lambda b, i: (b, i, 0)),
        out_shape=jax.ShapeDtypeStruct((bsz, s, n), F32),
        compiler_params=_params("parallel", "parallel"),
        name="in_proj",
    )(x, gain.reshape(1, d), mod, mod, w_bf16)


class _ChunkMasks:
    def __init__(self, n, chunk, reverse):
        ii = lax.broadcasted_iota(jnp.int32, (n, n), 0)
        jj = lax.broadcasted_iota(jnp.int32, (n, n), 1)
        same = (ii // chunk) == (jj // chunk)
        tri = (ii <= jj) if reverse else (ii >= jj)
        self.same = same
        self.incl = same & tri
        self.strict = self.incl & (ii != jj)
        self.eye = (ii == jj).astype(F32)
        m16 = (ii // 16) == (jj // 16)
        m32 = (ii // 32) == (jj // 32)
        self.m16 = m16.astype(F32)
        self.m32_only = (m32 & ~m16).astype(F32)
        self.m64_only = (same & ~m32).astype(F32)


def _unit_tri_inverses(lms, masks):
    ps = [-(lm * masks.m16) for lm in lms]
    ts = [masks.eye + p for p in ps]
    for _ in range(3):
        ps = [_dot(p, p) for p in ps]
        ts = [t + _dot(t, p) for t, p in zip(ts, ps)]
    for level in (masks.m32_only, masks.m64_only):
        cs = [_dot(lm * level, t) for lm, t in zip(lms, ts)]
        ts = [t - _dot(t, c) for t, c in zip(ts, cs)]
    return ts


def _dn_prep_kernel(cur_ref, prev_ref, next_ref, gates_ref, convw_ref, alog_ref, dtb_ref,
                    qkv_ref, gb_ref, ext_ref, *, ts):
    i = pl.program_id(1)
    n = pl.num_programs(1)
    halo = SUBLANE
    ext_ref[0:halo, :] = jnp.where(i > 0, prev_ref[...], 0.0)
    ext_ref[halo:halo + ts, :] = cur_ref[...]
    ext_ref[halo + ts:2 * halo + ts, :] = jnp.where(i < n - 1, next_ref[...], 0.0)
    pad = DN_CONV // 2
    for blk in range(3 * DN_HEADS):
        cols = slice(blk * DN_HEAD_DIM, (blk + 1) * DN_HEAD_DIM)
        acc = jnp.zeros((ts, DN_HEAD_DIM), F32)
        for tap in range(DN_CONV):
            acc = acc + ext_ref[halo - pad + tap:halo - pad + tap + ts, cols] * convw_ref[tap:tap + 1, cols]
        val = _silu(acc)
        if blk < 2 * DN_HEADS:
            val = val * lax.rsqrt(jnp.sum(val * val, axis=-1, keepdims=True) + 1e-6)
            if blk < DN_HEADS:
                val = val * (DN_HEAD_DIM ** -0.5)
        qkv_ref[:, cols] = val
    gin = gates_ref[...]
    g = -jnp.exp(alog_ref[...]) * _softplus(gin + dtb_ref[...])
    lane = lax.broadcasted_iota(jnp.int32, gin.shape, 1)
    gb_ref[...] = jnp.where(lane < 2 * DN_HEADS, g, _sigmoid(gin))


def _dn_prep(p0, conv_w, a_log, dt_bias, ts=512):
    bsz, s, _ = p0.shape
    c3 = 3 * DN_WIDTH
    gate_blk = (4 * DN_WIDTH + S5_WIDTH) // LANE
    nb8 = s // SUBLANE
    r8 = ts // SUBLANE
    alog = jnp.zeros((1, LANE), F32).at[0, :2 * DN_HEADS].set(a_log.reshape(-1))
    dtb = jnp.zeros((1, LANE), F32).at[0, :2 * DN_HEADS].set(dt_bias.reshape(-1))
    return pl.pallas_call(
        functools.partial(_dn_prep_kernel, ts=ts),
        grid=(bsz, s // ts),
        in_specs=[pl.BlockSpec((None, ts, c3), lambda b, i: (b, i, 0)),
                  pl.BlockSpec((None, SUBLANE, c3), lambda b, i: (b, jnp.maximum(i * r8 - 1, 0), 0)),
                  pl.BlockSpec((None, SUBLANE, c3), lambda b, i: (b, jnp.minimum((i + 1) * r8, nb8 - 1), 0)),
                  pl.BlockSpec((None, ts, LANE), lambda b, i: (b, i, gate_blk)),
                  pl.BlockSpec((DN_CONV, c3), lambda b, i: (0, 0)),
                  pl.BlockSpec((1, LANE), lambda b, i: (0, 0)),
                  pl.BlockSpec((1, LANE), lambda b, i: (0, 0))],
        out_specs=[pl.BlockSpec((None, ts, c3), lambda b, i: (b, i, 0)),
                   pl.BlockSpec((None, ts, LANE), lambda b, i: (b, i, 0))],
        out_shape=[jax.ShapeDtypeStruct((bsz, s, c3), F32),
                   jax.ShapeDtypeStruct((bsz, s, LANE), F32)],
        scratch_shapes=[pltpu.VMEM((ts + 2 * SUBLANE, c3), F32)],
        compiler_params=_params("parallel", "parallel"),
        name="dn_prep",
    )(p0, p0, p0, p0, conv_w, alog, dtb)


def _dn_direction(q_ref, k_ref, v_ref, gb_ref, gbt_ref, o_ref, state_ref, *, sc, reverse, direction):
    chunk = DN_CHUNK
    nck = sc // chunk
    masks = _ChunkMasks(sc, chunk, reverse)
    bd_incl = masks.incl.astype(F32)
    gcols = gb_ref[...]
    gc_cols = _dot(bd_incl, gcols, HI)
    gt_cols = _dot(masks.same.astype(F32), gcols, HI)
    gc_rows = _dot_nt(gbt_ref[...], bd_incl, HI)
    e_gc = jnp.exp(gc_cols)
    e_rest = jnp.exp(gt_cols - gc_cols)
    e_tot = jnp.exp(gt_cols)
    order = range(nck - 1, -1, -1) if reverse else range(nck)
    steps = []
    for h in range(DN_HEADS):
        gi = direction * DN_HEADS + h
        bi = 2 * DN_HEADS + gi
        cols = slice(h * DN_HEAD_DIM, (h + 1) * DN_HEAD_DIM)
        q = q_ref[:, cols]
        k = k_ref[:, cols]
        beta = gcols[:, bi:bi + 1]
        eg = e_gc[:, gi:gi + 1]
        decay = jnp.exp(jnp.where(masks.incl, gc_cols[:, gi:gi + 1] - gc_rows[gi:gi + 1, :], NEG_BIG))
        kb = k * beta
        attn = _dot_nt(q, k) * decay
        steps.append(dict(
            lm=jnp.where(masks.strict, _dot_nt(kb, k) * decay, 0.0), vb=v_ref[:, cols] * beta, kbg=kb * eg,
            attn=[attn[c * chunk:(c + 1) * chunk, c * chunk:(c + 1) * chunk] for c in range(nck)], qd=q * eg,
            kd=k * e_rest[:, gi:gi + 1], gl=e_tot[:, gi:gi + 1], cols=cols, o_ref=o_ref, state_ref=state_ref, h=h,
            order=order))
    for st, t in zip(steps, _unit_tri_inverses([st.pop('lm') for st in steps], masks)):
        st['u'] = _dot(t, st.pop('vb'))
        st['w'] = _dot(t, st.pop('kbg'))
    return steps


def _dn_recurrence(steps, chunk):
    states = [st['state_ref'][st['h']] for st in steps]
    for pos in range(len(steps[0]['order'])):
        for n, st in enumerate(steps):
            c = st['order'][pos]
            rows = slice(c * chunk, (c + 1) * chunk)
            v_new = st['u'][rows] - _dot(st['w'][rows], states[n])
            st['o_ref'][rows, st['cols']] = _dot(st['qd'][rows], states[n]) + _dot(st['attn'][c], v_new)
            states[n] = states[n] * st['gl'][c * chunk:c * chunk + 1] + _dot_tn(st['kd'][rows], v_new)
    for n, st in enumerate(steps):
        st['state_ref'][st['h']] = states[n]


def _dn_chunk_kernel(qf, kf, vf, gbf, gbtf, qr, kr, vr, gbr, gbtr, of_ref, or_ref, sf_ref, sr_ref, *, sc):
    @pl.when(pl.program_id(1) == 0)
    def _():
        sf_ref[...] = jnp.zeros_like(sf_ref)
        sr_ref[...] = jnp.zeros_like(sr_ref)

    steps = (_dn_direction(qf, kf, vf, gbf, gbtf, of_ref, sf_ref, sc=sc, reverse=False, direction=0)
             + _dn_direction(qr, kr, vr, gbr, gbtr, or_ref, sr_ref, sc=sc, reverse=True, direction=1))
    _dn_recurrence(steps, DN_CHUNK)


def _dn_chunked(qkv, gb, sc=256):
    bsz, s, _ = qkv.shape
    n = s // sc
    gbt = jnp.swapaxes(gb[..., :4 * DN_HEADS], 1, 2)
    fwd = lambda col: pl.BlockSpec((None, sc, DN_WIDTH), lambda b, i: (b, i, col))
    rev = lambda col: pl.BlockSpec((None, sc, DN_WIDTH), lambda b, i: (b, n - 1 - i, col))
    in_specs = ([fwd(0), fwd(1), fwd(2),
                 pl.BlockSpec((None, sc, LANE), lambda b, i: (b, i, 0)),
                 pl.BlockSpec((None, 4 * DN_HEADS, sc), lambda b, i: (b, 0, i))]
                + [rev(0), rev(1), rev(2),
                   pl.BlockSpec((None, sc, LANE), lambda b, i: (b, n - 1 - i, 0)),
                   pl.BlockSpec((None, 4 * DN_HEADS, sc), lambda b, i: (b, 0, n - 1 - i))])
    state = pltpu.VMEM((DN_HEADS, DN_HEAD_DIM, DN_HEAD_DIM), F32)
    return pl.pallas_call(
        functools.partial(_dn_chunk_kernel, sc=sc),
        grid=(bsz, n),
        in_specs=in_specs,
        out_specs=[pl.BlockSpec((None, sc, DN_WIDTH), lambda b, i: (b, i, 0)),
                   pl.BlockSpec((None, sc, DN_WIDTH), lambda b, i: (b, n - 1 - i, 0))],
        out_shape=[jax.ShapeDtypeStruct((bsz, s, DN_WIDTH), F32)] * 2,
        scratch_shapes=[state, state],
        compiler_params=_params("parallel", "arbitrary"),
        name="dn_chunked",
    )(qkv, qkv, qkv, gb, gbt, qkv, qkv, qkv, gb, gbt)


def _dn_post_kernel(of_ref, or_ref, z_ref, nw_ref, o_ref):
    for h in range(DN_HEADS):
        cols = slice(h * DN_HEAD_DIM, (h + 1) * DN_HEAD_DIM)
        o = of_ref[:, cols] + or_ref[:, cols]
        o_ref[:, cols] = _rms(o, NORM_EPS) * nw_ref[...] * _silu(z_ref[:, cols])


def _dn_post(o_f, o_r, p0, norm_w, ts=512):
    bsz, s, w = o_f.shape
    z_blk = 3 * DN_WIDTH // w
    spec = pl.BlockSpec((None, ts, w), lambda b, i: (b, i, 0))
    return pl.pallas_call(
        _dn_post_kernel,
        grid=(bsz, s // ts),
        in_specs=[spec, spec,
                  pl.BlockSpec((None, ts, w), lambda b, i: (b, i, z_blk)),
                  pl.BlockSpec((1, DN_HEAD_DIM), lambda b, i: (0, 0))],
        out_specs=spec,
        out_shape=jax.ShapeDtypeStruct((bsz, s, w), F32),
        compiler_params=_params("parallel", "parallel"),
        name="dn_post",
    )(o_f, o_r, p0, norm_w.reshape(1, DN_HEAD_DIM))


def _gated_deltanet(p0, conv_w, a_log, dt_bias, norm_w):
    qkv, gb = _dn_prep(p0, conv_w, a_log, dt_bias)
    o_f, o_r = _dn_chunked(qkv, gb)
    return _dn_post(o_f, o_r, p0, norm_w)


def _s5_tables(a_re, a_im, log_dt, b_re, b_im, c_re, c_im, n_chunks):
    lc, p, hh = S5_CHUNK, S5_STATE, S5_GROUP
    levels = int(math.log2(n_chunks))
    dt = jnp.exp(log_dt)[..., None]
    mag = jnp.exp(a_re * dt)
    lb_re, lb_im = mag * jnp.cos(a_im * dt), mag * jnp.sin(a_im * dt)
    den = a_re * a_re + a_im * a_im
    nr, ni = lb_re - 1.0, lb_im
    f_re = (nr * a_re + ni * a_im) / den
    f_im = (ni * a_re - nr * a_im) / den
    bb_re = f_re[..., None] * b_re - f_im[..., None] * b_im
    bb_im = f_re[..., None] * b_im + f_im[..., None] * b_re

    def power(tau):
        tau = jnp.asarray(tau, F32)
        m = jnp.exp(a_re[..., None] * dt[..., None] * tau)
        ang = a_im[..., None] * dt[..., None] * tau
        return m * jnp.cos(ang), m * jnp.sin(ang)

    pw_re, pw_im = power(np.arange(lc + 1))
    lbb_re = pw_re[..., None] * bb_re[:, :, :, None, :] - pw_im[..., None] * bb_im[:, :, :, None, :]
    lbb_im = pw_re[..., None] * bb_im[:, :, :, None, :] + pw_im[..., None] * bb_re[:, :, :, None, :]
    kt = (jnp.einsum('dgop,dgpti->dgtoi', c_re, lbb_re, precision=HI)
          - jnp.einsum('dgop,dgpti->dgtoi', c_im, lbb_im, precision=HI))
    ti = np.arange(lc)[:, None]
    to = np.arange(lc)[None, :]
    toeps, bmats, cmats = [], [], []
    for d in range(2):
        lag = (to - ti) if d == 0 else (ti - to)
        valid = jnp.asarray(lag >= 0, F32)
        blk = kt[d][:, np.clip(lag, 0, lc - 1)] * valid[None, :, :, None, None]
        toeps.append(jnp.transpose(blk, (0, 1, 4, 2, 3)).reshape(-1, lc * hh, lc * hh))
        e_in = (lc - 1 - np.arange(lc)) if d == 0 else np.arange(lc)
        bm_re = jnp.transpose(lbb_re[d][:, :, e_in, :], (0, 2, 3, 1))
        bm_im = jnp.transpose(lbb_im[d][:, :, e_in, :], (0, 2, 3, 1))
        bmats.append(jnp.concatenate([bm_re, bm_im], axis=-1).reshape(-1, lc * hh, 2 * p))
        e_out = (np.arange(lc) + 1) if d == 0 else (lc - np.arange(lc))
        m_re = c_re[d][:, :, :, None] * pw_re[d][:, None, :, e_out] - c_im[d][:, :, :, None] * pw_im[d][:, None, :, e_out]
        m_im = c_re[d][:, :, :, None] * pw_im[d][:, None, :, e_out] + c_im[d][:, :, :, None] * pw_re[d][:, None, :, e_out]
        cm = jnp.concatenate([jnp.transpose(m_re, (0, 2, 3, 1)), -jnp.transpose(m_im, (0, 2, 3, 1))], axis=1)
        cmats.append(cm.reshape(-1, 2 * p, lc * hh))
    sq_re, sq_im = power(lc * 2.0 ** np.arange(levels))
    scan_a = jnp.transpose(jnp.concatenate([sq_re, sq_re], axis=2), (0, 1, 3, 2))
    scan_b = jnp.transpose(jnp.concatenate([-sq_im, sq_im], axis=2), (0, 1, 3, 2))
    return jnp.stack(toeps), jnp.stack(bmats), jnp.stack(cmats), scan_a, scan_b


def _s5_kernel(u_ref, toep_ref, bmat_ref, cmat_ref, sa_ref, sb_ref, y_ref, *, n_chunks):
    u = u_ref[...]
    n = u.shape[0]
    p2 = 2 * S5_STATE
    levels = sa_ref.shape[1]
    cidx = lax.broadcasted_iota(jnp.int32, (n, p2), 0) & (n_chunks - 1)
    y = None
    for d in range(2):
        x = _dot(u, bmat_ref[d])
        for lv in range(levels):
            sh = 1 << lv
            if d == 0:
                xs = jnp.where(cidx >= sh, pltpu.roll(x, sh, 0), 0.0)
            else:
                xs = jnp.where(cidx < n_chunks - sh, pltpu.roll(x, n - sh, 0), 0.0)
            x = x + sa_ref[d, lv:lv + 1, :] * xs + sb_ref[d, lv:lv + 1, :] * pltpu.roll(xs, S5_STATE, 1)
        if d == 0:
            x_in = jnp.where(cidx >= 1, pltpu.roll(x, 1, 0), 0.0)
        else:
            x_in = jnp.where(cidx < n_chunks - 1, pltpu.roll(x, n - 1, 0), 0.0)
        yd = _dot(u, toep_ref[d]) + _dot(x_in, cmat_ref[d])
        y = yd if y is None else y + yd
    y_ref[...] = y


def _s5_scan(u, tables):
    bsz, s, _ = u.shape
    lc, g, hh = S5_CHUNK, S5_GROUPS, S5_GROUP
    nc = s // lc
    toep, bmat, cmat, sa, sb = tables
    uc = jnp.transpose(u.reshape(bsz, nc, lc, g, hh), (3, 0, 1, 2, 4)).reshape(g, bsz * nc, lc * hh)
    levels = sa.shape[2]
    y = pl.pallas_call(
        functools.partial(_s5_kernel, n_chunks=nc),
        grid=(g,),
        in_specs=[pl.BlockSpec((None, bsz * nc, lc * hh), lambda i: (i, 0, 0)),
                  pl.BlockSpec((2, None, lc * hh, lc * hh), lambda i: (0, i, 0, 0)),
                  pl.BlockSpec((2, None, lc * hh, 2 * S5_STATE), lambda i: (0, i, 0, 0)),
                  pl.BlockSpec((2, None, 2 * S5_STATE, lc * hh), lambda i: (0, i, 0, 0)),
                  pl.BlockSpec((2, None, levels, 2 * S5_STATE), lambda i: (0, i, 0, 0)),
                  pl.BlockSpec((2, None, levels, 2 * S5_STATE), lambda i: (0, i, 0, 0))],
        out_specs=pl.BlockSpec((None, bsz * nc, lc * hh), lambda i: (i, 0, 0)),
        out_shape=jax.ShapeDtypeStruct((g, bsz * nc, lc * hh), F32),
        compiler_params=_params("parallel"),
        name="s5_scan",
    )(uc, toep, bmat, cmat, sa, sb)
    return jnp.transpose(y.reshape(g, bsz, nc, lc, hh), (1, 2, 3, 0, 4)).reshape(bsz, s, g * hh)


def _gelu_tanh(x):
    return 0.5 * x * (1.0 + jnp.tanh(math.sqrt(2.0 / math.pi) * (x + 0.044715 * (x * x * x))))


def _s5_post_kernel(u_ref, y_ref, d_ref, w_ref, b_ref, o_ref):
    y = _gelu_tanh(d_ref[...] * u_ref[...] + y_ref[...])
    o_ref[...] = y * _sigmoid(_dot(y, w_ref[...]) + b_ref[...])


def _s5_post(p0, y, d_skip, glu_w, glu_b, ts=512):
    bsz, s, w = y.shape
    u_blk = 4 * DN_WIDTH // w
    spec = pl.BlockSpec((None, ts, w), lambda b, i: (b, i, 0))
    row = pl.BlockSpec((1, w), lambda b, i: (0, 0))
    return pl.pallas_call(
        _s5_post_kernel,
        grid=(bsz, s // ts),
        in_specs=[pl.BlockSpec((None, ts, w), lambda b, i: (b, i, u_blk)), spec, row,
                  pl.BlockSpec((w, w), lambda b, i: (0, 0)), row],
        out_specs=spec,
        out_shape=jax.ShapeDtypeStruct((bsz, s, w), F32),
        compiler_params=_params("parallel", "parallel"),
        name="s5_post",
    )(p0, y, d_skip.reshape(1, w), glu_w, glu_b.reshape(1, w))


def _s5_mixer(p0, a_re, a_im, log_dt, b_re, b_im, c_re, c_im, d_skip, glu_w, glu_b):
    s = p0.shape[1]
    u = p0[..., 4 * DN_WIDTH:4 * DN_WIDTH + S5_WIDTH]
    tables = _s5_tables(a_re, a_im, log_dt, b_re, b_im, c_re, c_im, s // S5_CHUNK)
    return _s5_post(p0, _s5_scan(u, tables), d_skip, glu_w, glu_b)


def _mix_out_kernel(ya_ref, yb_ref, x_ref, w_ref, npost_ref, g1_ref, npre_ref, sh2_ref, sc2_ref, rwt_ref,
                    x1_ref, h2_ref, pt_ref):
    half = ya_ref.shape[-1]
    y = _dot(ya_ref[...].astype(BF16), w_ref[0:half, :]) + _dot(yb_ref[...].astype(BF16), w_ref[half:2 * half, :])
    x1 = x_ref[...] + g1_ref[...] * (_rms(y, NORM_EPS) * npost_ref[...])
    x1_ref[...] = x1
    h2 = _rms(x1, NORM_EPS) * npre_ref[...] * (1.0 + sc2_ref[...]) + sh2_ref[...]
    h2_ref[...] = h2.astype(BF16)
    logits = _dot_nt(rwt_ref[...], h2, HI)
    ex = jnp.exp(logits - jnp.max(logits, axis=0, keepdims=True))
    pt_ref[...] = ex / jnp.sum(ex, axis=0, keepdims=True)


def _mix_out(ya, yb, x, w_out_bf16, npost, mod, npre, router_w, tm=256):
    bsz, s, d = x.shape
    half = ya.shape[-1]
    ne = router_w.shape[-1]
    row = lambda a: a.reshape(1, d)
    vec = pl.BlockSpec((1, d), lambda b, i: (0, 0))
    modblk = lambda k: pl.BlockSpec((None, 1, d), lambda b, i: (b, 0, k))
    tile = pl.BlockSpec((None, tm, d), lambda b, i: (b, i, 0))
    halfspec = pl.BlockSpec((None, tm, half), lambda b, i: (b, i, 0))
    return pl.pallas_call(
        _mix_out_kernel,
        grid=(bsz, s // tm),
        in_specs=[halfspec, halfspec, tile,
                  pl.BlockSpec((d, d), lambda b, i: (0, 0)), vec, modblk(2), vec, modblk(3), modblk(4),
                  pl.BlockSpec((ne, d), lambda b, i: (0, 0))],
        out_specs=[tile, tile, pl.BlockSpec((None, ne, tm), lambda b, i: (b, 0, i))],
        out_shape=[jax.ShapeDtypeStruct((bsz, s, d), F32),
                   jax.ShapeDtypeStruct((bsz, s, d), BF16),
                   jax.ShapeDtypeStruct((bsz, ne, s), F32)],
        compiler_params=_params("parallel", "parallel"),
        name="mix_out",
    )(ya, yb, x, w_out_bf16, row(npost), mod, row(npre), mod, mod, router_w.T)


def _prefix_lanes(x):
    n = x.shape[1]
    lane = lax.broadcasted_iota(jnp.int32, x.shape, 1)
    sh = 1
    while sh < n:
        x = x + jnp.where(lane >= sh, pltpu.roll(x, sh, 1), 0.0)
        sh *= 2
    return x


def _topk_kernel(p_ref, slot_ref, count_ref, *, cap):
    p = p_ref[...]
    rows = p.shape[0]

    def halve(_, bounds):
        lo, hi = bounds
        mid = 0.5 * (lo + hi)
        ok = jnp.sum((p >= mid).astype(F32), axis=1, keepdims=True) >= cap
        return jnp.where(ok, mid, lo), jnp.where(ok, hi, mid)

    lo, _ = lax.fori_loop(0, TOPK_BISECTIONS, halve, (jnp.zeros((rows, 1), F32), jnp.full((rows, 1), 2.0, F32)))
    gt = p > lo
    eq = p == lo
    need = cap - jnp.sum(gt.astype(F32), axis=1, keepdims=True)
    sel = gt | (eq & (_prefix_lanes(eq.astype(F32)) <= need))
    count = _prefix_lanes(sel.astype(F32))
    slot_ref[...] = jnp.where(sel, count - 1.0, -1.0).astype(jnp.int32)
    count_ref[...] = count.astype(jnp.int32)


def _topk_slots(pt, cap):
    bsz, ne, s = pt.shape
    spec = pl.BlockSpec((None, ne, s), lambda b: (b, 0, 0))
    return pl.pallas_call(
        functools.partial(_topk_kernel, cap=cap),
        grid=(bsz,), in_specs=[spec], out_specs=[spec, spec],
        out_shape=[jax.ShapeDtypeStruct((bsz, ne, s), jnp.int32)] * 2,
        compiler_params=_params("parallel"),
        name="topk_slots",
    )(pt)


def _tile_starts(count, tile):
    bsz, ne, _ = count.shape
    ends = count[:, :, tile - 1::tile]
    return jnp.concatenate([jnp.zeros((bsz, ne, 1), jnp.int32), ends], axis=2).reshape(-1)


BF16_ROWS = 2 * SUBLANE


def _expert_windows(starts_ref, ne, win):
    n_tiles = pl.num_programs(1)
    shift = win.bit_length() - 1
    firsts, n_pass = [], 0
    for e in range(ne):
        base = (pl.program_id(0) * ne + e) * (n_tiles + 1) + pl.program_id(1)
        first = (starts_ref[base] // BF16_ROWS) * BF16_ROWS
        firsts.append(first)
        n_pass = jnp.maximum(n_pass, (starts_ref[base + 1] - first + win - 1) >> shift)
    return firsts, n_pass


def _window_hits(slot_row, first, w, cap, win):
    row0 = first + w * win
    src = pl.multiple_of(jnp.minimum(row0, cap - win), BF16_ROWS)
    rows = src + lax.broadcasted_iota(jnp.int32, (win, slot_row.shape[1]), 0)
    return src, (rows == slot_row) & (rows >= row0)


def _gather_kernel(starts_ref, slot_ref, p_ref, h_ref, xin_ref, gate_ref, hit_ref, *, cap, win):
    ne = slot_ref.shape[0]

    @pl.when(pl.program_id(1) == 0)
    def _():
        xin_ref[...] = jnp.zeros_like(xin_ref)
        gate_ref[...] = jnp.zeros_like(gate_ref)

    firsts, n_pass = _expert_windows(starts_ref, ne, win)

    def one_pass(w, carry):
        srcs = []
        for e in range(ne):
            src, hit = _window_hits(slot_ref[e:e + 1, :], firsts[e], w, cap, win)
            srcs.append(src)
            hit_ref[e * win:(e + 1) * win, :] = hit.astype(BF16)
            gate_ref[e, pl.ds(src, win), :] += jnp.sum(jnp.where(hit, p_ref[e:e + 1, :], 0.0), axis=1, keepdims=True)
        rows = _dot(hit_ref[...], h_ref[...])
        for e in range(ne):
            dst = xin_ref.at[e, pl.ds(srcs[e], win), :]
            dst[...] = (dst[...].astype(F32) + rows[e * win:(e + 1) * win, :]).astype(BF16)
        return carry

    lax.fori_loop(0, n_pass, one_pass, 0)


def _moe_gather(starts, slot, pt, h_bf16, cap, tk, win=64):
    bsz, ne, s = slot.shape
    d = h_bf16.shape[-1]
    rowspec = pl.BlockSpec((None, ne, tk), lambda b, k, st: (b, 0, k))
    return pl.pallas_call(
        functools.partial(_gather_kernel, cap=cap, win=win),
        grid_spec=pltpu.PrefetchScalarGridSpec(
            num_scalar_prefetch=1,
            grid=(bsz, s // tk),
            in_specs=[rowspec, rowspec, pl.BlockSpec((None, tk, d), lambda b, k, st: (b, k, 0))],
            out_specs=[pl.BlockSpec((None, ne, cap, d), lambda b, k, st: (b, 0, 0, 0)),
                       pl.BlockSpec((None, ne, cap, 1), lambda b, k, st: (b, 0, 0, 0))],
            scratch_shapes=[pltpu.VMEM((ne * win, tk), BF16)]),
        out_shape=[jax.ShapeDtypeStruct((bsz, ne, cap, d), BF16),
                   jax.ShapeDtypeStruct((bsz, ne, cap, 1), F32)],
        compiler_params=_params("parallel", "arbitrary"),
        name="moe_gather",
    )(starts, slot, pt, h_bf16)


def _ffn_kernel(xin_ref, gate_ref, wg_ref, wu_ref, wd_ref, y_ref, acc_ref):
    f = pl.program_id(1)
    bsz, cap, d = xin_ref.shape
    x = xin_ref[...].reshape(bsz * cap, d)
    @pl.when(f == 0)
    def _():
        acc_ref[...] = jnp.zeros_like(acc_ref)

    hid = _silu(_dot(x, wg_ref[...].astype(BF16))) * _dot(x, wu_ref[...].astype(BF16))
    acc_ref[...] += _dot(hid.astype(BF16), wd_ref[...].astype(BF16))

    @pl.when(f == pl.num_programs(1) - 1)
    def _():
        y_ref[...] = (acc_ref[...] * gate_ref[...].reshape(bsz * cap, 1)).astype(BF16).reshape(bsz, cap, d)


def _moe_ffn(xin, gate, w_gate, w_up, w_down, layer, tf=256):
    bsz, ne, cap, d = xin.shape
    dexp = w_gate.shape[-1]
    return pl.pallas_call(
        _ffn_kernel,
        grid=(ne, dexp // tf),
        in_specs=[pl.BlockSpec((bsz, None, cap, d), lambda e, f: (0, e, 0, 0)),
                  pl.BlockSpec((bsz, None, cap, 1), lambda e, f: (0, e, 0, 0)),
                  pl.BlockSpec((None, None, d, tf), lambda e, f: (layer, e, 0, f)),
                  pl.BlockSpec((None, None, d, tf), lambda e, f: (layer, e, 0, f)),
                  pl.BlockSpec((None, None, tf, d), lambda e, f: (layer, e, f, 0))],
        out_specs=pl.BlockSpec((bsz, None, cap, d), lambda e, f: (0, e, 0, 0)),
        out_shape=jax.ShapeDtypeStruct((bsz, ne, cap, d), BF16),
        scratch_shapes=[pltpu.VMEM((bsz * cap, d), F32)],
        compiler_params=_params("parallel", "arbitrary"),
        name="moe_ffn",
    )(xin, gate, w_gate, w_up, w_down)


def _scatter_kernel(starts_ref, slot_ref, y_ref, x_ref, g2_ref, npost_ref, o_ref, ywin_ref, hit_ref, acc_ref,
                    *, cap, win):
    ne = y_ref.shape[0]
    firsts, n_pass = _expert_windows(starts_ref, ne, win)
    acc_ref[...] = jnp.zeros_like(acc_ref)

    def one_pass(w, carry):
        for e in range(ne):
            src, hit = _window_hits(slot_ref[e:e + 1, :], firsts[e], w, cap, win)
            hit_ref[e * win:(e + 1) * win, :] = hit.astype(BF16)
            ywin_ref[e * win:(e + 1) * win, :] = y_ref[e, pl.ds(src, win), :]
        acc_ref[...] += _dot_tn(hit_ref[...], ywin_ref[...])
        return carry

    lax.fori_loop(0, n_pass, one_pass, 0)
    o_ref[...] = x_ref[...] + g2_ref[...] * (_rms(acc_ref[...], NORM_EPS) * npost_ref[...])


def _moe_scatter(starts, slot, y, x, mod, npost, ts, win=128):
    bsz, ne, cap, d = y.shape
    s = x.shape[1]
    win = min(win, cap)
    tile = pl.BlockSpec((None, ts, d), lambda b, i, st: (b, i, 0))
    return pl.pallas_call(
        functools.partial(_scatter_kernel, cap=cap, win=win),
        grid_spec=pltpu.PrefetchScalarGridSpec(
            num_scalar_prefetch=1,
            grid=(bsz, s // ts),
            in_specs=[pl.BlockSpec((None, ne, ts), lambda b, i, st: (b, 0, i)),
                      pl.BlockSpec((None, ne, cap, d), lambda b, i, st: (b, 0, 0, 0)),
                      tile,
                      pl.BlockSpec((None, 1, d), lambda b, i, st: (b, 0, 5)),
                      pl.BlockSpec((1, d), lambda b, i, st: (0, 0))],
            out_specs=tile,
            scratch_shapes=[pltpu.VMEM((ne * win, d), BF16), pltpu.VMEM((ne * win, ts), BF16),
                            pltpu.VMEM((ts, d), F32)]),
        out_shape=jax.ShapeDtypeStruct((bsz, s, d), F32),
        compiler_params=_params("parallel", "arbitrary"),
        name="moe_scatter",
    )(starts, slot, y, x, mod, npost.reshape(1, d))


def _expert_choice_ffn(x1, h2, pt, mod, npost, w_gate, w_up, w_down, layer):
    s = x1.shape[1]
    ne = pt.shape[1]
    cap = EC_CAPACITY_FACTOR * s // ne
    tile = MOE_TOKEN_TILE
    slot, count = _topk_slots(pt, cap)
    starts = _tile_starts(count, tile)
    xin, gate = _moe_gather(starts, slot, pt, h2, cap, tile)
    y = _moe_ffn(xin, gate, w_gate, w_up, w_down, layer)
    return _moe_scatter(starts, slot, y, x1, mod, npost, tile)


def _rw_prep_kernel(cur_ref, prev_ref, next_ref, mu_ref, w0_ref, w2_ref, a0_ref, a2_ref, g2_ref, kkw_ref, ka_ref,
                    hsum_ref, r_ref, v_ref, kk_ref, g_ref, lw0_ref, lw1_ref, kd0_ref, kd1_ref, b0_ref, b1_ref,
                    ext_ref, *, ts):
    i = pl.program_id(1)
    n = pl.num_programs(1)
    halo = SUBLANE
    ext_ref[0:halo, :] = jnp.where(i > 0, prev_ref[...], 0.0)
    ext_ref[halo:halo + ts, :] = cur_ref[...]
    ext_ref[halo + ts:2 * halo + ts, :] = jnp.where(i < n - 1, next_ref[...], 0.0)

    def shifted(cols):
        pf = ext_ref[halo:halo + ts, cols]
        neigh = 0.5 * (ext_ref[halo - 1:halo - 1 + ts, cols] + ext_ref[halo + 1:halo + 1 + ts, cols])
        return pf + mu_ref[:, cols] * (neigh - pf)

    w = RW_WIDTH
    r = shifted(slice(0, w))
    k = shifted(slice(w, 2 * w))
    v = shifted(slice(2 * w, 3 * w))
    lora = shifted(slice(3 * w, 3 * w + 3 * LANE))
    wdec = -_softplus(-(w0_ref[...] + _dot(jnp.tanh(lora[:, 0:LANE]), w2_ref[...]))) - 0.5
    lw = -jnp.exp(wdec)
    a = _sigmoid(a0_ref[...] + _dot(lora[:, LANE:2 * LANE], a2_ref[...]))
    g_ref[...] = _dot(_sigmoid(lora[:, 2 * LANE:3 * LANE]), g2_ref[...])
    kk = k * kkw_ref[...]
    kk = kk * lax.rsqrt(_dot(kk * kk, hsum_ref[...], HI) + 1e-6)
    r_ref[...] = r
    v_ref[...] = v
    kk_ref[...] = kk
    for d, (lw_ref, kd_ref, b_ref) in enumerate(((lw0_ref, kd0_ref, b0_ref), (lw1_ref, kd1_ref, b1_ref))):
        ad = a[:, d * w:(d + 1) * w]
        lw_ref[...] = lw[:, d * w:(d + 1) * w]
        kd_ref[...] = k * (1.0 + (ad - 1.0) * ka_ref[...])
        b_ref[...] = kk * ad


def _block_diag2(m):
    z = jnp.zeros_like(m[0])
    return jnp.concatenate([jnp.concatenate([m[0], z], 1), jnp.concatenate([z, m[1]], 1)], 0)


def _head_sum_matrix(width, head_dim):
    idx = np.arange(width) // head_dim
    return jnp.asarray(idx[:, None] == idx[None, :], F32)


def _rw_prep(p1, mu, w0, w2, a0, a2, g2, k_k, k_a, ts=256):
    bsz, s, _ = p1.shape
    cin = RW_IN
    w = RW_WIDTH
    nb8 = s // SUBLANE
    r8 = ts // SUBLANE
    full = lambda shape: pl.BlockSpec(shape, lambda b, i: (0,) * len(shape))
    out_spec = pl.BlockSpec((None, ts, w), lambda b, i: (b, i, 0))
    return pl.pallas_call(
        functools.partial(_rw_prep_kernel, ts=ts),
        grid=(bsz, s // ts),
        in_specs=[pl.BlockSpec((None, ts, cin), lambda b, i: (b, i, 0)),
                  pl.BlockSpec((None, SUBLANE, cin), lambda b, i: (b, jnp.maximum(i * r8 - 1, 0), 0)),
                  pl.BlockSpec((None, SUBLANE, cin), lambda b, i: (b, jnp.minimum((i + 1) * r8, nb8 - 1), 0)),
                  full((1, cin)), full((1, 2 * w)), full((LANE, 2 * w)), full((1, 2 * w)), full((LANE, 2 * w)),
                  full((RW_GATE_LORA, w)), full((1, w)), full((1, w)), full((w, w))],
        out_specs=[out_spec] * 10,
        out_shape=[jax.ShapeDtypeStruct((bsz, s, w), F32)] * 10,
        scratch_shapes=[pltpu.VMEM((ts + 2 * SUBLANE, cin), F32)],
        compiler_params=_params("parallel", "parallel"),
        name="rw_prep",
    )(p1, p1, p1, mu.reshape(1, cin), w0.reshape(1, 2 * w), _block_diag2(w2), a0.reshape(1, 2 * w),
      _block_diag2(a2), g2, k_k.reshape(1, w), k_a.reshape(1, w), _head_sum_matrix(w, RW_HEAD_DIM))


def _rw_direction(r_ref, v_ref, kk_ref, lw_ref, kd_ref, b_ref, o_ref, state_ref, *, sc, reverse):
    chunk = RW_CHUNK
    nck = sc // chunk
    hd = RW_HEAD_DIM
    masks = _ChunkMasks(sc, chunk, reverse)
    lw = lw_ref[...]
    lg = _dot(masks.incl.astype(F32), lw, HI)
    lt = _dot(masks.same.astype(F32), lw, HI)
    e_neg = jnp.exp(-lg)
    kkt = kk_ref[...] * jnp.exp(lg - lw)
    khat = kd_ref[...] * e_neg
    bhat = b_ref[...] * e_neg
    rt = r_ref[...] * jnp.exp(lg)
    e_rest = jnp.exp(lt - lg)
    kbar = kd_ref[...] * e_rest
    bbar = b_ref[...] * e_rest
    e_tot = jnp.exp(lt)
    order = range(nck - 1, -1, -1) if reverse else range(nck)
    steps = []
    for h in range(RW_HEADS):
        cols = slice(h * hd, (h + 1) * hd)
        v = v_ref[:, cols]
        kkt_h = kkt[:, cols]
        khat_h = khat[:, cols]
        bhat_h = bhat[:, cols]
        rt_h = rt[:, cols]
        l_a = jnp.where(masks.strict, _dot_nt(kkt_h, khat_h), 0.0)
        a_qb = jnp.where(masks.incl, _dot_nt(rt_h, bhat_h), 0.0)
        steps.append(dict(
            l_b=jnp.where(masks.strict, _dot_nt(kkt_h, bhat_h), 0.0), kkt=kkt_h, la_v=_dot(l_a, v), rt=rt_h,
            y0=_dot(jnp.where(masks.incl, _dot_nt(rt_h, khat_h), 0.0), v),
            a_qb=[a_qb[c * chunk:(c + 1) * chunk, c * chunk:(c + 1) * chunk] for c in range(nck)],
            v=v, kbar=kbar[:, cols], bbar=bbar[:, cols], gl=e_tot[:, cols],
            cols=cols, o_ref=o_ref, state_ref=state_ref, h=h, order=order))
    for st, t in zip(steps, _unit_tri_inverses([st.pop('l_b') for st in steps], masks)):
        st['wm'] = _dot(t, st.pop('kkt'))
        st['u'] = _dot(t, st.pop('la_v'))
    return steps


def _rw_recurrence(steps, chunk):
    states = [st['state_ref'][st['h']] for st in steps]
    for pos in range(len(steps[0]['order'])):
        for n, st in enumerate(steps):
            c = st['order'][pos]
            rows = slice(c * chunk, (c + 1) * chunk)
            p = _dot_nt(st['wm'][rows], states[n]) + st['u'][rows]
            st['o_ref'][rows, st['cols']] = (st['y0'][rows] + _dot_nt(st['rt'][rows], states[n])
                                             - _dot(st['a_qb'][c], p))
            states[n] = (states[n] * st['gl'][c * chunk:c * chunk + 1] + _dot_tn(st['v'][rows], st['kbar'][rows])
                         - _dot_tn(p, st['bbar'][rows]))
    for n, st in enumerate(steps):
        st['state_ref'][st['h']] = states[n]


def _rw_chunk_kernel(rf, vf, kkf, lwf, kdf, bf, rr, vr, kkr, lwr, kdr, br, of_ref, or_ref, sf_ref, sr_ref, *, sc):
    @pl.when(pl.program_id(1) == 0)
    def _():
        sf_ref[...] = jnp.zeros_like(sf_ref)
        sr_ref[...] = jnp.zeros_like(sr_ref)

    steps = (_rw_direction(rf, vf, kkf, lwf, kdf, bf, of_ref, sf_ref, sc=sc, reverse=False)
             + _rw_direction(rr, vr, kkr, lwr, kdr, br, or_ref, sr_ref, sc=sc, reverse=True))
    _rw_recurrence(steps, RW_CHUNK)


def _rw_chunked(r, v, kk, lw0, lw1, kd0, kd1, b0, b1, sc=256):
    bsz, s, w = r.shape
    n = s // sc
    fwd = pl.BlockSpec((None, sc, w), lambda b, i: (b, i, 0))
    rev = pl.BlockSpec((None, sc, w), lambda b, i: (b, n - 1 - i, 0))
    state = pltpu.VMEM((RW_HEADS, RW_HEAD_DIM, RW_HEAD_DIM), F32)
    return pl.pallas_call(
        functools.partial(_rw_chunk_kernel, sc=sc),
        grid=(bsz, n),
        in_specs=[fwd] * 6 + [rev] * 6,
        out_specs=[fwd, rev],
        out_shape=[jax.ShapeDtypeStruct((bsz, s, w), F32)] * 2,
        scratch_shapes=[state, state],
        compiler_params=_params("parallel", "arbitrary"),
        name="rw_chunked",
    )(r, v, kk, lw0, kd0, b0, r, v, kk, lw1, kd1, b1)


def _rw_post_kernel(yf_ref, yr_ref, r_ref, v_ref, g_ref, kd0_ref, kd1_ref, rk_ref, lnw_ref, lnb_ref, hsum_ref, o_ref):
    inv = 1.0 / RW_HEAD_DIM
    y = yf_ref[...] + yr_ref[...]
    mean = _dot(y, hsum_ref[...], HI) * inv
    cen = y - mean
    var = _dot(cen * cen, hsum_ref[...], HI) * inv
    y = cen * lax.rsqrt(var + RW_GN_EPS) * lnw_ref[...] + lnb_ref[...]
    k_bonus = 0.5 * (kd0_ref[...] + kd1_ref[...])
    y = y + _dot(r_ref[...] * k_bonus * rk_ref[...], hsum_ref[...], HI) * v_ref[...]
    o_ref[...] = y * g_ref[...]


def _rw_post(y_f, y_r, r, v, g, kd0, kd1, r_k, ln_w, ln_b, ts=512):
    bsz, s, w = r.shape
    spec = pl.BlockSpec((None, ts, w), lambda b, i: (b, i, 0))
    row = pl.BlockSpec((1, w), lambda b, i: (0, 0))
    return pl.pallas_call(
        _rw_post_kernel,
        grid=(bsz, s // ts),
        in_specs=[spec] * 7 + [row, row, row, pl.BlockSpec((w, w), lambda b, i: (0, 0))],
        out_specs=spec,
        out_shape=jax.ShapeDtypeStruct((bsz, s, w), F32),
        compiler_params=_params("parallel", "parallel"),
        name="rw_post",
    )(y_f, y_r, r, v, g, kd0, kd1, r_k.reshape(1, w), ln_w.reshape(1, w), ln_b.reshape(1, w),
      _head_sum_matrix(w, RW_HEAD_DIM))


def _rwkv7_mixer(p1, mu, w0, w2, a0, a2, g2, k_k, k_a, r_k, ln_w, ln_b):
    r, v, kk, g, lw0, lw1, kd0, kd1, b0, b1 = _rw_prep(p1, mu, w0, w2, a0, a2, g2, k_k, k_a)
    y_f, y_r = _rw_chunked(r, v, kk, lw0, lw1, kd0, kd1, b0, b1)
    return _rw_post(y_f, y_r, r, v, g, kd0, kd1, r_k, ln_w, ln_b)


def _diff_attn_kernel(q_ref, k_ref, v_ref, lam_ref, sub_ref, o_ref, *, tq, lambda_init):
    h = pl.program_id(1)
    i = pl.program_id(2)
    s = k_ref.shape[0]
    hd = DF_HEAD_DIM
    slope = jnp.where(h == 0, 2.0 ** -2, jnp.where(h == 1, 2.0 ** -4, jnp.where(h == 2, 2.0 ** -6, 2.0 ** -8)))
    lp = lam_ref[...]
    lam = (jnp.exp(jnp.sum(lp[0:1] * lp[1:2], axis=1, keepdims=True))
           - jnp.exp(jnp.sum(lp[2:3] * lp[3:4], axis=1, keepdims=True)) + lambda_init)
    qpos = i * tq + lax.broadcasted_iota(jnp.int32, (tq, s), 0)
    kpos = lax.broadcasted_iota(jnp.int32, (tq, s), 1)
    bias = slope.astype(F32) * jnp.abs(qpos - kpos).astype(F32)
    v = v_ref[...].astype(BF16)
    outs = []
    for m in range(2):
        q = q_ref[:, m * hd:(m + 1) * hd] * (hd ** -0.5)
        sc = _dot_nt(q, k_ref[:, m * hd:(m + 1) * hd]) - bias
        ex = jnp.exp(sc - jnp.max(sc, axis=1, keepdims=True))
        outs.append(_dot(ex, v) / jnp.sum(ex, axis=1, keepdims=True))
    o = outs[0] - lam * outs[1]
    o_ref[...] = _rms(o, 1e-5) * sub_ref[...] * (1.0 - lambda_init)


def _diff_attention(p1, lam_params, subln_w, lambda_init, tq=256):
    bsz, s, _ = p1.shape
    hw = 2 * DF_HEAD_DIM
    q0 = RW_IN // hw
    return pl.pallas_call(
        functools.partial(_diff_attn_kernel, tq=tq, lambda_init=lambda_init),
        grid=(bsz, DF_HEADS, s // tq),
        in_specs=[pl.BlockSpec((None, tq, hw), lambda b, h, i: (b, i, q0 + h)),
                  pl.BlockSpec((None, s, hw), lambda b, h, i: (b, 0, q0 + DF_HEADS + h)),
                  pl.BlockSpec((None, s, hw), lambda b, h, i: (b, 0, q0 + 2 * DF_HEADS + h)),
                  pl.BlockSpec((4, DF_HEAD_DIM), lambda b, h, i: (0, 0)),
                  pl.BlockSpec((1, hw), lambda b, h, i: (0, 0))],
        out_specs=pl.BlockSpec((None, tq, hw), lambda b, h, i: (b, i, h)),
        out_shape=jax.ShapeDtypeStruct((bsz, s, DF_HEADS * hw), F32),
        compiler_params=_params("parallel", "parallel", "parallel"),
        name="diff_attn",
    )(p1, p1, p1, lam_params, subln_w.reshape(1, hw))


def _layer0_weights(ab_w_in):
    d = ab_w_in.shape[0]
    main = 4 * DN_WIDTH
    gates = 4 * DN_HEADS
    w = jnp.concatenate([ab_w_in[:, :main], ab_w_in[:, main + gates:], ab_w_in[:, main:main + gates],
                         jnp.zeros((d, LANE - gates), ab_w_in.dtype)], axis=1)
    return w.astype(BF16)


def kernel(x, c, ada_w, ada_b, norm_mix_pre, norm_mix_post, norm_ffn_pre, norm_ffn_post, router_w, exp_w_gate, exp_w_up, exp_w_down, ab_w_in, ab_w_out, dn_conv, dn_a_log, dn_dt_bias, dn_norm, s5_a_re, s5_a_im, s5_log_dt, s5_b_re, s5_b_im, s5_c_re, s5_c_im, s5_d, s5_glu_w, s5_glu_b, cd_w_in, cd_w_out, rw_mu, rw_w0, rw_w2, rw_a0, rw_a2, rw_g2, rw_k_k, rw_k_a, rw_r_k, rw_ln_w, rw_ln_b, df_lambda, df_subln):
    depth = ada_w.shape[0]
    mod_all = _ada_mod(c, ada_w, ada_b)
    for layer in range(depth):
        mod = mod_all[layer][:, None, :]
        i = layer // 2
        if layer % 2 == 0:
            p = _in_proj(x, norm_mix_pre[layer], mod, _layer0_weights(ab_w_in[i]))
            ya = _gated_deltanet(p, dn_conv[i], dn_a_log[i], dn_dt_bias[i], dn_norm[i])
            yb = _s5_mixer(p, s5_a_re[i], s5_a_im[i], s5_log_dt[i], s5_b_re[i], s5_b_im[i], s5_c_re[i], s5_c_im[i],
                           s5_d[i], s5_glu_w[i], s5_glu_b[i])
            w_out = ab_w_out[i]
        else:
            lambda_init = 0.8 - 0.6 * math.exp(-0.3 * layer)
            p = _in_proj(x, norm_mix_pre[layer], mod, cd_w_in[i].astype(BF16))
            ya = _rwkv7_mixer(p, rw_mu[i], rw_w0[i], rw_w2[i], rw_a0[i], rw_a2[i], rw_g2[i], rw_k_k[i], rw_k_a[i],
                              rw_r_k[i].reshape(-1), rw_ln_w[i], rw_ln_b[i])
            yb = _diff_attention(p, df_lambda[i], df_subln[i], lambda_init)
            w_out = cd_w_out[i]
        x1, h2, pt = _mix_out(ya, yb, x, w_out.astype(BF16), norm_mix_post[layer], mod, norm_ffn_pre[layer],
                              router_w[layer])
        x = _expert_choice_ffn(x1, h2, pt, mod, norm_ffn_post[layer], exp_w_gate, exp_w_up, exp_w_down, layer)
    return x
```

```python
import functools
import math

import jax
import jax.numpy as jnp
import numpy as np
from jax import lax
from jax.experimental import pallas as pl
from jax.experimental.pallas import tpu as pltpu

F32 = jnp.float32
BF16 = jnp.bfloat16
HI = lax.Precision.HIGHEST

LANE = 128
SUBLANE = 8
VMEM_LIMIT = 56 * 1024 * 1024

NORM_EPS = 1e-6
DN_HEADS = 4
DN_HEAD_DIM = 128
DN_WIDTH = 512
DN_CONV = 5
DN_CHUNK = 64
S5_WIDTH = 512
S5_GROUP = 16
S5_GROUPS = 32
S5_STATE = 64
S5_CHUNK = 16
RW_WIDTH = 512
RW_HEADS = 8
RW_HEAD_DIM = 64
RW_LORA = 64
RW_GATE_LORA = 128
RW_GN_EPS = 64e-5
RW_CHUNK = 64
RW_IN = 1920
DF_WIDTH = 512
DF_HEADS = 4
DF_HEAD_DIM = 64
N_EXPERTS = 16
EC_CAPACITY_FACTOR = 2
NEG_BIG = -1e30
LOG2_E = 1.4426950408889634
MOE_TOKEN_TILE = 256
TOPK_BISECTIONS = 152


def _params(*sem):
    return pltpu.CompilerParams(dimension_semantics=sem, vmem_limit_bytes=VMEM_LIMIT)


def _sigmoid(x):
    return 1.0 / (1.0 + jnp.exp(-x))


def _silu(x):
    return x * _sigmoid(x)


def _softplus(x):
    return jnp.maximum(x, 0.0) + jnp.log(1.0 + jnp.exp(-jnp.abs(x)))


def _rms(x, eps):
    return x * lax.rsqrt(jnp.mean(x * x, axis=-1, keepdims=True) + eps)


def _mxu_operands(a, b, precision):
    if precision is None:
        return a.astype(BF16), b.astype(BF16)
    return a, b


def _dot(a, b, precision=None):
    a, b = _mxu_operands(a, b, precision)
    return jnp.dot(a, b, preferred_element_type=F32, precision=precision)


def _dot_nt(a, b, precision=None):
    a, b = _mxu_operands(a, b, precision)
    return lax.dot_general(a, b, (((1,), (1,)), ((), ())), preferred_element_type=F32, precision=precision)


def _dot_tn(a, b, precision=None):
    a, b = _mxu_operands(a, b, precision)
    return lax.dot_general(a, b, (((0,), (0,)), ((), ())), preferred_element_type=F32, precision=precision)


def _ada_kernel(c_ref, w_ref, b_ref, o_ref):
    o_ref[...] = _dot(_silu(c_ref[...]), w_ref[...], HI) + b_ref[...]


def _ada_mod(c, ada_w, ada_b):
    depth, d, n = ada_w.shape
    bsz = c.shape[0]
    rows = -(-bsz // SUBLANE) * SUBLANE
    c_pad = jnp.zeros((rows, d), F32).at[:bsz].set(c)
    out = pl.pallas_call(
        _ada_kernel,
        grid=(depth, n // d),
        in_specs=[pl.BlockSpec((rows, d), lambda l, j: (0, 0)),
                  pl.BlockSpec((None, d, d), lambda l, j: (l, 0, j)),
                  pl.BlockSpec((None, 1, d), lambda l, j: (l, 0, j))],
        out_specs=pl.BlockSpec((None, rows, d), lambda l, j: (l, 0, j)),
        out_shape=jax.ShapeDtypeStruct((depth, rows, n), F32),
        compiler_params=_params("parallel", "parallel"),
        name="ada_mod",
    )(c_pad, ada_w, ada_b.reshape(depth, 1, n))
    return out[:, :bsz]


def _in_proj_kernel(x_ref, gain_ref, sh_ref, sc_ref, w_ref, o_ref):
    h = _rms(x_ref[...], NORM_EPS) * gain_ref[...] * (1.0 + sc_ref[...]) + sh_ref[...]
    o_ref[...] = _dot(h.astype(BF16), w_ref[...])


def _in_proj(x, gain, mod, w_bf16, tm=256):
    bsz, s, d = x.shape
    n = w_bf16.shape[1]
    return pl.pallas_call(
        _in_proj_kernel,
        grid=(bsz, s // tm),
        in_specs=[pl.BlockSpec((None, tm, d), lambda b, i: (b, i, 0)),
                  pl.BlockSpec((1, d), lambda b, i: (0, 0)),
                  pl.BlockSpec((None, 1, d), lambda b, i: (b, 0, 0)),
                  pl.BlockSpec((None, 1, d), lambda b, i: (b, 0, 1)),
                  pl.BlockSpec((d, n), lambda b, i: (0, 0))],
        out_specs=pl.BlockSpec((None, tm, n), lambda b, i: (b, i, 0)),
        out_shape=jax.ShapeDtypeStruct((bsz, s, n), F32),
        compiler_params=_params("parallel", "parallel"),
        name="in_proj",
    )(x, gain.reshape(1, d), mod, mod, w_bf16)


class _ChunkMasks:
    def __init__(self, n, chunk, reverse):
        ii = lax.broadcasted_iota(jnp.int32, (n, n), 0)
        jj = lax.broadcasted_iota(jnp.int32, (n, n), 1)
        same = (ii // chunk) == (jj // chunk)
        tri = (ii <= jj) if reverse else (ii >= jj)
        self.same = same
        self.incl = same & tri
        self.strict = self.incl & (ii != jj)
        self.eye = (ii == jj).astype(F32)
        m16 = (ii // 16) == (jj // 16)
        m32 = (ii // 32) == (jj // 32)
        self.m16 = m16.astype(F32)
        self.m32_only = (m32 & ~m16).astype(F32)
        self.m64_only = (same & ~m32).astype(F32)


def _unit_tri_inverses(lms, masks):
    ps = [-(lm * masks.m16) for lm in lms]
    ts = [masks.eye + p for p in ps]
    for _ in range(3):
        ps = [_dot(p, p) for p in ps]
        ts = [t + _dot(t, p) for t, p in zip(ts, ps)]
    for level in (masks.m32_only, masks.m64_only):
        cs = [_dot(lm * level, t) for lm, t in zip(lms, ts)]
        ts = [t - _dot(t, c) for t, c in zip(ts, cs)]
    return ts


def _dn_prep_kernel(cur_ref, prev_ref, next_ref, gates_ref, convw_ref, alog_ref, dtb_ref,
                    qkv_ref, gb_ref, ext_ref, *, ts):
    i = pl.program_id(1)
    n = pl.num_programs(1)
    halo = SUBLANE
    ext_ref[0:halo, :] = jnp.where(i > 0, prev_ref[...], 0.0)
    ext_ref[halo:halo + ts, :] = cur_ref[...]
    ext_ref[halo + ts:2 * halo + ts, :] = jnp.where(i < n - 1, next_ref[...], 0.0)
    pad = DN_CONV // 2
    for blk in range(3 * DN_HEADS):
        cols = slice(blk * DN_HEAD_DIM, (blk + 1) * DN_HEAD_DIM)
        acc = jnp.zeros((ts, DN_HEAD_DIM), F32)
        for tap in range(DN_CONV):
            acc = acc + ext_ref[halo - pad + tap:halo - pad + tap + ts, cols] * convw_ref[tap:tap + 1, cols]
        val = _silu(acc)
        if blk < 2 * DN_HEADS:
            val = val * lax.rsqrt(jnp.sum(val * val, axis=-1, keepdims=True) + 1e-6)
            if blk < DN_HEADS:
                val = val * (DN_HEAD_DIM ** -0.5)
        qkv_ref[:, cols] = val
    gin = gates_ref[...]
    g = -jnp.exp(alog_ref[...]) * _softplus(gin + dtb_ref[...])
    lane = lax.broadcasted_iota(jnp.int32, gin.shape, 1)
    gb_ref[...] = jnp.where(lane < 2 * DN_HEADS, g, _sigmoid(gin))


def _dn_prep(p0, conv_w, a_log, dt_bias, ts=512):
    bsz, s, _ = p0.shape
    c3 = 3 * DN_WIDTH
    gate_blk = (4 * DN_WIDTH + S5_WIDTH) // LANE
    nb8 = s // SUBLANE
    r8 = ts // SUBLANE
    alog = jnp.zeros((1, LANE), F32).at[0, :2 * DN_HEADS].set(a_log.reshape(-1))
    dtb = jnp.zeros((1, LANE), F32).at[0, :2 * DN_HEADS].set(dt_bias.reshape(-1))
    return pl.pallas_call(
        functools.partial(_dn_prep_kernel, ts=ts),
        grid=(bsz, s // ts),
        in_specs=[pl.BlockSpec((None, ts, c3), lambda b, i: (b, i, 0)),
                  pl.BlockSpec((None, SUBLANE, c3), lambda b, i: (b, jnp.maximum(i * r8 - 1, 0), 0)),
                  pl.BlockSpec((None, SUBLANE, c3), lambda b, i: (b, jnp.minimum((i + 1) * r8, nb8 - 1), 0)),
                  pl.BlockSpec((None, ts, LANE), lambda b, i: (b, i, gate_blk)),
                  pl.BlockSpec((DN_CONV, c3), lambda b, i: (0, 0)),
                  pl.BlockSpec((1, LANE), lambda b, i: (0, 0)),
                  pl.BlockSpec((1, LANE), lambda b, i: (0, 0))],
        out_specs=[pl.BlockSpec((None, ts, c3), lambda b, i: (b, i, 0)),
                   pl.BlockSpec((None, ts, LANE), lambda b, i: (b, i, 0))],
        out_shape=[jax.ShapeDtypeStruct((bsz, s, c3), F32),
                   jax.ShapeDtypeStruct((bsz, s, LANE), F32)],
        scratch_shapes=[pltpu.VMEM((ts + 2 * SUBLANE, c3), F32)],
        compiler_params=_params("parallel", "parallel"),
        name="dn_prep",
    )(p0, p0, p0, p0, conv_w, alog, dtb)


def _dn_direction(q_ref, k_ref, v_ref, gb_ref, gbt_ref, o_ref, state_ref, *, sc, reverse, direction):
    chunk = DN_CHUNK
    nck = sc // chunk
    masks = _ChunkMasks(sc, chunk, reverse)
    bd_incl = masks.incl.astype(F32)
    gcols = gb_ref[...]
    gc_cols = _dot(bd_incl, gcols, HI)
    gt_cols = _dot(masks.same.astype(F32), gcols, HI)
    gc_rows = _dot_nt(gbt_ref[...], bd_incl, HI)
    e_gc = jnp.exp(gc_cols)
    e_rest = jnp.exp(gt_cols - gc_cols)
    e_tot = jnp.exp(gt_cols)
    order = range(nck - 1, -1, -1) if reverse else range(nck)
    steps = []
    for h in range(DN_HEADS):
        gi = direction * DN_HEADS + h
        bi = 2 * DN_HEADS + gi
        cols = slice(h * DN_HEAD_DIM, (h + 1) * DN_HEAD_DIM)
        q = q_ref[:, cols]
        k = k_ref[:, cols]
        beta = gcols[:, bi:bi + 1]
        eg = e_gc[:, gi:gi + 1]
        decay = jnp.exp(jnp.where(masks.incl, gc_cols[:, gi:gi + 1] - gc_rows[gi:gi + 1, :], NEG_BIG))
        kb = k * beta
        attn = _dot_nt(q, k) * decay
        steps.append(dict(
            lm=jnp.where(masks.strict, _dot_nt(kb, k) * decay, 0.0), vb=v_ref[:, cols] * beta, kbg=kb * eg,
            attn=[attn[c * chunk:(c + 1) * chunk, c * chunk:(c + 1) * chunk] for c in range(nck)], qd=q * eg,
            kd=k * e_rest[:, gi:gi + 1], gl=e_tot[:, gi:gi + 1], cols=cols, o_ref=o_ref, state_ref=state_ref, h=h,
            order=order))
    for st, t in zip(steps, _unit_tri_inverses([st.pop('lm') for st in steps], masks)):
        st['u'] = _dot(t, st.pop('vb'))
        st['w'] = _dot(t, st.pop('kbg'))
    return steps


def _dn_recurrence(steps, chunk):
    states = [st['state_ref'][st['h']] for st in steps]
    for pos in range(len(steps[0]['order'])):
        for n, st in enumerate(steps):
            c = st['order'][pos]
            rows = slice(c * chunk, (c + 1) * chunk)
            v_new = st['u'][rows] - _dot(st['w'][rows], states[n])
            st['o_ref'][rows, st['cols']] = _dot(st['qd'][rows], states[n]) + _dot(st['attn'][c], v_new)
            states[n] = states[n] * st['gl'][c * chunk:c * chunk + 1] + _dot_tn(st['kd'][rows], v_new)
    for n, st in enumerate(steps):
        st['state_ref'][st['h']] = states[n]


def _dn_chunk_kernel(qf, kf, vf, gbf, gbtf, qr, kr, vr, gbr, gbtr, of_ref, or_ref, sf_ref, sr_ref, *, sc):
    @pl.when(pl.program_id(1) == 0)
    def _():
        sf_ref[...] = jnp.zeros_like(sf_ref)
        sr_ref[...] = jnp.zeros_like(sr_ref)

    steps = (_dn_direction(qf, kf, vf, gbf, gbtf, of_ref, sf_ref, sc=sc, reverse=False, direction=0)
             + _dn_direction(qr, kr, vr, gbr, gbtr, or_ref, sr_ref, sc=sc, reverse=True, direction=1))
    _dn_recurrence(steps, DN_CHUNK)


def _dn_chunked(qkv, gb, sc=256):
    bsz, s, _ = qkv.shape
    n = s // sc
    gbt = jnp.swapaxes(gb[..., :4 * DN_HEADS], 1, 2)
    fwd = lambda col: pl.BlockSpec((None, sc, DN_WIDTH), lambda b, i: (b, i, col))
    rev = lambda col: pl.BlockSpec((None, sc, DN_WIDTH), lambda b, i: (b, n - 1 - i, col))
    in_specs = ([fwd(0), fwd(1), fwd(2),
                 pl.BlockSpec((None, sc, LANE), lambda b, i: (b, i, 0)),
                 pl.BlockSpec((None, 4 * DN_HEADS, sc), lambda b, i: (b, 0, i))]
                + [rev(0), rev(1), rev(2),
                   pl.BlockSpec((None, sc, LANE), lambda b, i: (b, n - 1 - i, 0)),
                   pl.BlockSpec((None, 4 * DN_HEADS, sc), lambda b, i: (b, 0, n - 1 - i))])
    state = pltpu.VMEM((DN_HEADS, DN_HEAD_DIM, DN_HEAD_DIM), F32)
    return pl.pallas_call(
        functools.partial(_dn_chunk_kernel, sc=sc),
        grid=(bsz, n),
        in_specs=in_specs,
        out_specs=[pl.BlockSpec((None, sc, DN_WIDTH), lambda b, i: (b, i, 0)),
                   pl.BlockSpec((None, sc, DN_WIDTH), lambda b, i: (b, n - 1 - i, 0))],
        out_shape=[jax.ShapeDtypeStruct((bsz, s, DN_WIDTH), F32)] * 2,
        scratch_shapes=[state, state],
        compiler_params=_params("parallel", "arbitrary"),
        name="dn_chunked",
    )(qkv, qkv, qkv, gb, gbt, qkv, qkv, qkv, gb, gbt)


def _dn_post_kernel(of_ref, or_ref, z_ref, nw_ref, o_ref):
    for h in range(DN_HEADS):
        cols = slice(h * DN_HEAD_DIM, (h + 1) * DN_HEAD_DIM)
        o = of_ref[:, cols] + or_ref[:, cols]
        o_ref[:, cols] = _rms(o, NORM_EPS) * nw_ref[...] * _silu(z_ref[:, cols])


def _dn_post(o_f, o_r, p0, norm_w, ts=512):
    bsz, s, w = o_f.shape
    z_blk = 3 * DN_WIDTH // w
    spec = pl.BlockSpec((None, ts, w), lambda b, i: (b, i, 0))
    return pl.pallas_call(
        _dn_post_kernel,
        grid=(bsz, s // ts),
        in_specs=[spec, spec,
                  pl.BlockSpec((None, ts, w), lambda b, i: (b, i, z_blk)),
                  pl.BlockSpec((1, DN_HEAD_DIM), lambda b, i: (0, 0))],
        out_specs=spec,
        out_shape=jax.ShapeDtypeStruct((bsz, s, w), F32),
        compiler_params=_params("parallel", "parallel"),
        name="dn_post",
    )(o_f, o_r, p0, norm_w.reshape(1, DN_HEAD_DIM))


def _gated_deltanet(p0, conv_w, a_log, dt_bias, norm_w):
    qkv, gb = _dn_prep(p0, conv_w, a_log, dt_bias)
    o_f, o_r = _dn_chunked(qkv, gb)
    return _dn_post(o_f, o_r, p0, norm_w)


def _s5_tables(a_re, a_im, log_dt, b_re, b_im, c_re, c_im, n_chunks):
    lc, p, hh = S5_CHUNK, S5_STATE, S5_GROUP
    levels = int(math.log2(n_chunks))
    dt = jnp.exp(log_dt)[..., None]
    mag = jnp.exp(a_re * dt)
    lb_re, lb_im = mag * jnp.cos(a_im * dt), mag * jnp.sin(a_im * dt)
    den = a_re * a_re + a_im * a_im
    nr, ni = lb_re - 1.0, lb_im
    f_re = (nr * a_re + ni * a_im) / den
    f_im = (ni * a_re - nr * a_im) / den
    bb_re = f_re[..., None] * b_re - f_im[..., None] * b_im
    bb_im = f_re[..., None] * b_im + f_im[..., None] * b_re

    def power(tau):
        tau = jnp.asarray(tau, F32)
        m = jnp.exp(a_re[..., None] * dt[..., None] * tau)
        ang = a_im[..., None] * dt[..., None] * tau
        return m * jnp.cos(ang), m * jnp.sin(ang)

    pw_re, pw_im = power(np.arange(lc + 1))
    lbb_re = pw_re[..., None] * bb_re[:, :, :, None, :] - pw_im[..., None] * bb_im[:, :, :, None, :]
    lbb_im = pw_re[..., None] * bb_im[:, :, :, None, :] + pw_im[..., None] * bb_re[:, :, :, None, :]
    kt = (jnp.einsum('dgop,dgpti->dgtoi', c_re, lbb_re, precision=HI)
          - jnp.einsum('dgop,dgpti->dgtoi', c_im, lbb_im, precision=HI))
    ti = np.arange(lc)[:, None]
    to = np.arange(lc)[None, :]
    toeps, bmats, cmats = [], [], []
    for d in range(2):
        lag = (to - ti) if d == 0 else (ti - to)
        valid = jnp.asarray(lag >= 0, F32)
        blk = kt[d][:, np.clip(lag, 0, lc - 1)] * valid[None, :, :, None, None]
        toeps.append(jnp.transpose(blk, (0, 1, 4, 2, 3)).reshape(-1, lc * hh, lc * hh))
        e_in = (lc - 1 - np.arange(lc)) if d == 0 else np.arange(lc)
        bm_re = jnp.transpose(lbb_re[d][:, :, e_in, :], (0, 2, 3, 1))
        bm_im = jnp.transpose(lbb_im[d][:, :, e_in, :], (0, 2, 3, 1))
        bmats.append(jnp.concatenate([bm_re, bm_im], axis=-1).reshape(-1, lc * hh, 2 * p))
        e_out = (np.arange(lc) + 1) if d == 0 else (lc - np.arange(lc))
        m_re = c_re[d][:, :, :, None] * pw_re[d][:, None, :, e_out] - c_im[d][:, :, :, None] * pw_im[d][:, None, :, e_out]
        m_im = c_re[d][:, :, :, None] * pw_im[d][:, None, :, e_out] + c_im[d][:, :, :, None] * pw_re[d][:, None, :, e_out]
        cm = jnp.concatenate([jnp.transpose(m_re, (0, 2, 3, 1)), -jnp.transpose(m_im, (0, 2, 3, 1))], axis=1)
        cmats.append(cm.reshape(-1, 2 * p, lc * hh))
    sq_re, sq_im = power(lc * 2.0 ** np.arange(levels))
    scan_a = jnp.transpose(jnp.concatenate([sq_re, sq_re], axis=2), (0, 1, 3, 2))
    scan_b = jnp.transpose(jnp.concatenate([-sq_im, sq_im], axis=2), (0, 1, 3, 2))
    return jnp.stack(toeps), jnp.stack(bmats), jnp.stack(cmats), scan_a, scan_b


def _s5_kernel(u_ref, toep_ref, bmat_ref, cmat_ref, sa_ref, sb_ref, y_ref, *, n_chunks):
    u = u_ref[...]
    n = u.shape[0]
    p2 = 2 * S5_STATE
    levels = sa_ref.shape[1]
    cidx = lax.broadcasted_iota(jnp.int32, (n, p2), 0) & (n_chunks - 1)
    y = None
    for d in range(2):
        x = _dot(u, bmat_ref[d])
        for lv in range(levels):
            sh = 1 << lv
            if d == 0:
                xs = jnp.where(cidx >= sh, pltpu.roll(x, sh, 0), 0.0)
            else:
                xs = jnp.where(cidx < n_chunks - sh, pltpu.roll(x, n - sh, 0), 0.0)
            x = x + sa_ref[d, lv:lv + 1, :] * xs + sb_ref[d, lv:lv + 1, :] * pltpu.roll(xs, S5_STATE, 1)
        if d == 0:
            x_in = jnp.where(cidx >= 1, pltpu.roll(x, 1, 0), 0.0)
        else:
            x_in = jnp.where(cidx < n_chunks - 1, pltpu.roll(x, n - 1, 0), 0.0)
        yd = _dot(u, toep_ref[d]) + _dot(x_in, cmat_ref[d])
        y = yd if y is None else y + yd
    y_ref[...] = y


def _s5_scan(u, tables):
    bsz, s, _ = u.shape
    lc, g, hh = S5_CHUNK, S5_GROUPS, S5_GROUP
    nc = s // lc
    toep, bmat, cmat, sa, sb = tables
    uc = jnp.transpose(u.reshape(bsz, nc, lc, g, hh), (3, 0, 1, 2, 4)).reshape(g, bsz * nc, lc * hh)
    levels = sa.shape[2]
    y = pl.pallas_call(
        functools.partial(_s5_kernel, n_chunks=nc),
        grid=(g,),
        in_specs=[pl.BlockSpec((None, bsz * nc, lc * hh), lambda i: (i, 0, 0)),
                  pl.BlockSpec((2, None, lc * hh, lc * hh), lambda i: (0, i, 0, 0)),
                  pl.BlockSpec((2, None, lc * hh, 2 * S5_STATE), lambda i: (0, i, 0, 0)),
                  pl.BlockSpec((2, None, 2 * S5_STATE, lc * hh), lambda i: (0, i, 0, 0)),
                  pl.BlockSpec((2, None, levels, 2 * S5_STATE), lambda i: (0, i, 0, 0)),
                  pl.BlockSpec((2, None, levels, 2 * S5_STATE), lambda i: (0, i, 0, 0))],
        out_specs=pl.BlockSpec((None, bsz * nc, lc * hh), lambda i: (i, 0, 0)),
        out_shape=jax.ShapeDtypeStruct((g, bsz * nc, lc * hh), F32),
        compiler_params=_params("parallel"),
        name="s5_scan",
    )(uc, toep, bmat, cmat, sa, sb)
    return jnp.transpose(y.reshape(g, bsz, nc, lc, hh), (1, 2, 3, 0, 4)).reshape(bsz, s, g * hh)


def _gelu_tanh(x):
    return 0.5 * x * (1.0 + jnp.tanh(math.sqrt(2.0 / math.pi) * (x + 0.044715 * (x * x * x))))


def _s5_post_kernel(u_ref, y_ref, d_ref, w_ref, b_ref, o_ref):
    y = _gelu_tanh(d_ref[...] * u_ref[...] + y_ref[...])
    o_ref[...] = y * _sigmoid(_dot(y, w_ref[...]) + b_ref[...])


def _s5_post(p0, y, d_skip, glu_w, glu_b, ts=512):
    bsz, s, w = y.shape
    u_blk = 4 * DN_WIDTH // w
    spec = pl.BlockSpec((None, ts, w), lambda b, i: (b, i, 0))
    row = pl.BlockSpec((1, w), lambda b, i: (0, 0))
    return pl.pallas_call(
        _s5_post_kernel,
        grid=(bsz, s // ts),
        in_specs=[pl.BlockSpec((None, ts, w), lambda b, i: (b, i, u_blk)), spec, row,
                  pl.BlockSpec((w, w), lambda b, i: (0, 0)), row],
        out_specs=spec,
        out_shape=jax.ShapeDtypeStruct((bsz, s, w), F32),
        compiler_params=_params("parallel", "parallel"),
        name="s5_post",
    )(p0, y, d_skip.reshape(1, w), glu_w, glu_b.reshape(1, w))


def _s5_mixer(p0, a_re, a_im, log_dt, b_re, b_im, c_re, c_im, d_skip, glu_w, glu_b):
    s = p0.shape[1]
    u = p0[..., 4 * DN_WIDTH:4 * DN_WIDTH + S5_WIDTH]
    tables = _s5_tables(a_re, a_im, log_dt, b_re, b_im, c_re, c_im, s // S5_CHUNK)
    return _s5_post(p0, _s5_scan(u, tables), d_skip, glu_w, glu_b)


def _mix_out_kernel(ya_ref, yb_ref, x_ref, w_ref, npost_ref, g1_ref, npre_ref, sh2_ref, sc2_ref, rwt_ref,
                    x1_ref, h2_ref, pt_ref):
    half = ya_ref.shape[-1]
    y = _dot(ya_ref[...].astype(BF16), w_ref[0:half, :]) + _dot(yb_ref[...].astype(BF16), w_ref[half:2 * half, :])
    x1 = x_ref[...] + g1_ref[...] * (_rms(y, NORM_EPS) * npost_ref[...])
    x1_ref[...] = x1
    h2 = _rms(x1, NORM_EPS) * npre_ref[...] * (1.0 + sc2_ref[...]) + sh2_ref[...]
    h2_ref[...] = h2.astype(BF16)
    logits = _dot_nt(rwt_ref[...], h2, HI)
    ex = jnp.exp(logits - jnp.max(logits, axis=0, keepdims=True))
    pt_ref[...] = ex / jnp.sum(ex, axis=0, keepdims=True)


def _mix_out(ya, yb, x, w_out_bf16, npost, mod, npre, router_w, tm=256):
    bsz, s, d = x.shape
    half = ya.shape[-1]
    ne = router_w.shape[-1]
    row = lambda a: a.reshape(1, d)
    vec = pl.BlockSpec((1, d), lambda b, i: (0, 0))
    modblk = lambda k: pl.BlockSpec((None, 1, d), lambda b, i: (b, 0, k))
    tile = pl.BlockSpec((None, tm, d), lambda b, i: (b, i, 0))
    halfspec = pl.BlockSpec((None, tm, half), lambda b, i: (b, i, 0))
    return pl.pallas_call(
        _mix_out_kernel,
        grid=(bsz, s // tm),
        in_specs=[halfspec, halfspec, tile,
                  pl.BlockSpec((d, d), lambda b, i: (0, 0)), vec, modblk(2), vec, modblk(3), modblk(4),
                  pl.BlockSpec((ne, d), lambda b, i: (0, 0))],
        out_specs=[tile, tile, pl.BlockSpec((None, ne, tm), lambda b, i: (b, 0, i))],
        out_shape=[jax.ShapeDtypeStruct((bsz, s, d), F32),
                   jax.ShapeDtypeStruct((bsz, s, d), BF16),
                   jax.ShapeDtypeStruct((bsz, ne, s), F32)],
        compiler_params=_params("parallel", "parallel"),
        name="mix_out",
    )(ya, yb, x, w_out_bf16, row(npost), mod, row(npre), mod, mod, router_w.T)


def _prefix_lanes(x):
    n = x.shape[1]
    lane = lax.broadcasted_iota(jnp.int32, x.shape, 1)
    sh = 1
    while sh < n:
        x = x + jnp.where(lane >= sh, pltpu.roll(x, sh, 1), 0.0)
        sh *= 2
    return x


def _topk_kernel(p_ref, slot_ref, count_ref, *, cap):
    p = p_ref[...]
    rows = p.shape[0]

    def halve(_, bounds):
        lo, hi = bounds
        mid = 0.5 * (lo + hi)
        ok = jnp.sum((p >= mid).astype(F32), axis=1, keepdims=True) >= cap
        return jnp.where(ok, mid, lo), jnp.where(ok, hi, mid)

    lo, _ = lax.fori_loop(0, TOPK_BISECTIONS, halve, (jnp.zeros((rows, 1), F32), jnp.full((rows, 1), 2.0, F32)))
    gt = p > lo
    eq = p == lo
    need = cap - jnp.sum(gt.astype(F32), axis=1, keepdims=True)
    sel = gt | (eq & (_prefix_lanes(eq.astype(F32)) <= need))
    count = _prefix_lanes(sel.astype(F32))
    slot_ref[...] = jnp.where(sel, count - 1.0, -1.0).astype(jnp.int32)
    count_ref[...] = count.astype(jnp.int32)


def _topk_slots(pt, cap):
    bsz, ne, s = pt.shape
    spec = pl.BlockSpec((bsz * ne, s), lambda i: (0, 0))
    slot, count = pl.pallas_call(
        functools.partial(_topk_kernel, cap=cap),
        grid=(1,), in_specs=[spec], out_specs=[spec, spec],
        out_shape=[jax.ShapeDtypeStruct((bsz * ne, s), jnp.int32)] * 2,
        compiler_params=_params("arbitrary"),
        name="topk_slots",
    )(pt.reshape(bsz * ne, s))
    return slot.reshape(bsz, ne, s), count.reshape(bsz, ne, s)


def _tile_starts(count, tile):
    bsz, ne, _ = count.shape
    ends = count[:, :, tile - 1::tile]
    return jnp.concatenate([jnp.zeros((bsz, ne, 1), jnp.int32), ends], axis=2).reshape(-1)


BF16_ROWS = 2 * SUBLANE


def _expert_windows(starts_ref, ne, win):
    n_tiles = pl.num_programs(1)
    shift = win.bit_length() - 1
    firsts, n_pass = [], 0
    for e in range(ne):
        base = (pl.program_id(0) * ne + e) * (n_tiles + 1) + pl.program_id(1)
        first = (starts_ref[base] // BF16_ROWS) * BF16_ROWS
        firsts.append(first)
        n_pass = jnp.maximum(n_pass, (starts_ref[base + 1] - first + win - 1) >> shift)
    return firsts, n_pass


def _window_hits(slot_row, first, w, cap, win):
    row0 = first + w * win
    src = pl.multiple_of(jnp.minimum(row0, cap - win), BF16_ROWS)
    rows = src + lax.broadcasted_iota(jnp.int32, (win, slot_row.shape[1]), 0)
    return src, (rows == slot_row) & (rows >= row0)


def _gather_kernel(starts_ref, slot_ref, p_ref, h_ref, xin_ref, gate_ref, hit_ref, *, cap, win):
    ne = slot_ref.shape[0]

    @pl.when(pl.program_id(1) == 0)
    def _():
        xin_ref[...] = jnp.zeros_like(xin_ref)
        gate_ref[...] = jnp.zeros_like(gate_ref)

    firsts, n_pass = _expert_windows(starts_ref, ne, win)

    def one_pass(w, carry):
        srcs = []
        for e in range(ne):
            src, hit = _window_hits(slot_ref[e:e + 1, :], firsts[e], w, cap, win)
            srcs.append(src)
            hit_ref[e * win:(e + 1) * win, :] = hit.astype(BF16)
            gate_ref[e, pl.ds(src, win), :] += jnp.sum(jnp.where(hit, p_ref[e:e + 1, :], 0.0), axis=1, keepdims=True)
        rows = _dot(hit_ref[...], h_ref[...])
        for e in range(ne):
            dst = xin_ref.at[e, pl.ds(srcs[e], win), :]
            dst[...] = (dst[...].astype(F32) + rows[e * win:(e + 1) * win, :]).astype(BF16)
        return carry

    lax.fori_loop(0, n_pass, one_pass, 0)


def _moe_gather(starts, slot, pt, h_bf16, cap, tk, win=64):
    bsz, ne, s = slot.shape
    d = h_bf16.shape[-1]
    rowspec = pl.BlockSpec((None, ne, tk), lambda b, k, st: (b, 0, k))
    return pl.pallas_call(
        functools.partial(_gather_kernel, cap=cap, win=win),
        grid_spec=pltpu.PrefetchScalarGridSpec(
            num_scalar_prefetch=1,
            grid=(bsz, s // tk),
            in_specs=[rowspec, rowspec, pl.BlockSpec((None, tk, d), lambda b, k, st: (b, k, 0))],
            out_specs=[pl.BlockSpec((None, ne, cap, d), lambda b, k, st: (b, 0, 0, 0)),
                       pl.BlockSpec((None, ne, cap, 1), lambda b, k, st: (b, 0, 0, 0))],
            scratch_shapes=[pltpu.VMEM((ne * win, tk), BF16)]),
        out_shape=[jax.ShapeDtypeStruct((bsz, ne, cap, d), BF16),
                   jax.ShapeDtypeStruct((bsz, ne, cap, 1), F32)],
        compiler_params=_params("parallel", "arbitrary"),
        name="moe_gather",
    )(starts, slot, pt, h_bf16)


def _ffn_kernel(xin_ref, gate_ref, wg_ref, wu_ref, wd_ref, y_ref, acc_ref):
    f = pl.program_id(1)
    bsz, cap, d = xin_ref.shape
    x = xin_ref[...].reshape(bsz * cap, d)
    @pl.when(f == 0)
    def _():
        acc_ref[...] = jnp.zeros_like(acc_ref)

    hid = _silu(_dot(x, wg_ref[...].astype(BF16))) * _dot(x, wu_ref[...].astype(BF16))
    acc_ref[...] += _dot(hid.astype(BF16), wd_ref[...].astype(BF16))

    @pl.when(f == pl.num_programs(1) - 1)
    def _():
        y_ref[...] = (acc_ref[...] * gate_ref[...].reshape(bsz * cap, 1)).astype(BF16).reshape(bsz, cap, d)


def _moe_ffn(xin, gate, w_gate, w_up, w_down, layer, tf=256):
    bsz, ne, cap, d = xin.shape
    dexp = w_gate.shape[-1]
    return pl.pallas_call(
        _ffn_kernel,
        grid=(ne, dexp // tf),
        in_specs=[pl.BlockSpec((bsz, None, cap, d), lambda e, f: (0, e, 0, 0)),
                  pl.BlockSpec((bsz, None, cap, 1), lambda e, f: (0, e, 0, 0)),
                  pl.BlockSpec((None, None, d, tf), lambda e, f: (layer, e, 0, f)),
                  pl.BlockSpec((None, None, d, tf), lambda e, f: (layer, e, 0, f)),
                  pl.BlockSpec((None, None, tf, d), lambda e, f: (layer, e, f, 0))],
        out_specs=pl.BlockSpec((bsz, None, cap, d), lambda e, f: (0, e, 0, 0)),
        out_shape=jax.ShapeDtypeStruct((bsz, ne, cap, d), BF16),
        scratch_shapes=[pltpu.VMEM((bsz * cap, d), F32)],
        compiler_params=_params("parallel", "arbitrary"),
        name="moe_ffn",
    )(xin, gate, w_gate, w_up, w_down)


def _scatter_kernel(starts_ref, slot_ref, y_ref, x_ref, g2_ref, npost_ref, o_ref, ywin_ref, hit_ref, acc_ref,
                    *, cap, win):
    ne = y_ref.shape[0]
    firsts, n_pass = _expert_windows(starts_ref, ne, win)
    acc_ref[...] = jnp.zeros_like(acc_ref)

    def one_pass(w, carry):
        for e in range(ne):
            src, hit = _window_hits(slot_ref[e:e + 1, :], firsts[e], w, cap, win)
            hit_ref[e * win:(e + 1) * win, :] = hit.astype(BF16)
            ywin_ref[e * win:(e + 1) * win, :] = y_ref[e, pl.ds(src, win), :]
        acc_ref[...] += _dot_tn(hit_ref[...], ywin_ref[...])
        return carry

    lax.fori_loop(0, n_pass, one_pass, 0)
    o_ref[...] = x_ref[...] + g2_ref[...] * (_rms(acc_ref[...], NORM_EPS) * npost_ref[...])


def _moe_scatter(starts, slot, y, x, mod, npost, ts, win=128):
    bsz, ne, cap, d = y.shape
    s = x.shape[1]
    win = min(win, cap)
    tile = pl.BlockSpec((None, ts, d), lambda b, i, st: (b, i, 0))
    return pl.pallas_call(
        functools.partial(_scatter_kernel, cap=cap, win=win),
        grid_spec=pltpu.PrefetchScalarGridSpec(
            num_scalar_prefetch=1,
            grid=(bsz, s // ts),
            in_specs=[pl.BlockSpec((None, ne, ts), lambda b, i, st: (b, 0, i)),
                      pl.BlockSpec((None, ne, cap, d), lambda b, i, st: (b, 0, 0, 0)),
                      tile,
                      pl.BlockSpec((None, 1, d), lambda b, i, st: (b, 0, 5)),
                      pl.BlockSpec((1, d), lambda b, i, st: (0, 0))],
            out_specs=tile,
            scratch_shapes=[pltpu.VMEM((ne * win, d), BF16), pltpu.VMEM((ne * win, ts), BF16),
                            pltpu.VMEM((ts, d), F32)]),
        out_shape=jax.ShapeDtypeStruct((bsz, s, d), F32),
        compiler_params=_params("parallel", "arbitrary"),
        name="moe_scatter",
    )(starts, slot, y, x, mod, npost.reshape(1, d))


def _expert_choice_ffn(x1, h2, pt, mod, npost, w_gate, w_up, w_down, layer):
    s = x1.shape[1]
    ne = pt.shape[1]
    cap = EC_CAPACITY_FACTOR * s // ne
    tile = MOE_TOKEN_TILE
    slot, count = _topk_slots(pt, cap)
    starts = _tile_starts(count, tile)
    xin, gate = _moe_gather(starts, slot, pt, h2, cap, tile)
    y = _moe_ffn(xin, gate, w_gate, w_up, w_down, layer)
    return _moe_scatter(starts, slot, y, x1, mod, npost, tile)


def _rw_prep_kernel(cur_ref, prev_ref, next_ref, mu_ref, w0_ref, w2_ref, a0_ref, a2_ref, g2_ref, kkw_ref, ka_ref,
                    hsum_ref, r_ref, v_ref, kk_ref, g_ref, lw0_ref, lw1_ref, kd0_ref, kd1_ref, b0_ref, b1_ref,
                    ext_ref, *, ts):
    i = pl.program_id(1)
    n = pl.num_programs(1)
    halo = SUBLANE
    ext_ref[0:halo, :] = jnp.where(i > 0, prev_ref[...], 0.0)
    ext_ref[halo:halo + ts, :] = cur_ref[...]
    ext_ref[halo + ts:2 * halo + ts, :] = jnp.where(i < n - 1, next_ref[...], 0.0)

    def shifted(cols):
        pf = ext_ref[halo:halo + ts, cols]
        neigh = 0.5 * (ext_ref[halo - 1:halo - 1 + ts, cols] + ext_ref[halo + 1:halo + 1 + ts, cols])
        return pf + mu_ref[:, cols] * (neigh - pf)

    w = RW_WIDTH
    r = shifted(slice(0, w))
    k = shifted(slice(w, 2 * w))
    v = shifted(slice(2 * w, 3 * w))
    lora = shifted(slice(3 * w, 3 * w + 3 * LANE))
    wdec = -_softplus(-(w0_ref[...] + _dot(jnp.tanh(lora[:, 0:LANE]), w2_ref[...]))) - 0.5
    lw = -jnp.exp(wdec)
    a = _sigmoid(a0_ref[...] + _dot(lora[:, LANE:2 * LANE], a2_ref[...]))
    g_ref[...] = _dot(_sigmoid(lora[:, 2 * LANE:3 * LANE]), g2_ref[...])
    kk = k * kkw_ref[...]
    kk = kk * lax.rsqrt(_dot(kk * kk, hsum_ref[...], HI) + 1e-6)
    r_ref[...] = r
    v_ref[...] = v
    kk_ref[...] = kk
    for d, (lw_ref, kd_ref, b_ref) in enumerate(((lw0_ref, kd0_ref, b0_ref), (lw1_ref, kd1_ref, b1_ref))):
        ad = a[:, d * w:(d + 1) * w]
        lw_ref[...] = lw[:, d * w:(d + 1) * w]
        kd_ref[...] = k * (1.0 + (ad - 1.0) * ka_ref[...])
        b_ref[...] = kk * ad


def _block_diag2(m):
    z = jnp.zeros_like(m[0])
    return jnp.concatenate([jnp.concatenate([m[0], z], 1), jnp.concatenate([z, m[1]], 1)], 0)


def _head_sum_matrix(width, head_dim):
    idx = np.arange(width) // head_dim
    return jnp.asarray(idx[:, None] == idx[None, :], F32)


def _rw_prep(p1, mu, w0, w2, a0, a2, g2, k_k, k_a, ts=256):
    bsz, s, _ = p1.shape
    cin = RW_IN
    w = RW_WIDTH
    nb8 = s // SUBLANE
    r8 = ts // SUBLANE
    full = lambda shape: pl.BlockSpec(shape, lambda b, i: (0,) * len(shape))
    out_spec = pl.BlockSpec((None, ts, w), lambda b, i: (b, i, 0))
    return pl.pallas_call(
        functools.partial(_rw_prep_kernel, ts=ts),
        grid=(bsz, s // ts),
        in_specs=[pl.BlockSpec((None, ts, cin), lambda b, i: (b, i, 0)),
                  pl.BlockSpec((None, SUBLANE, cin), lambda b, i: (b, jnp.maximum(i * r8 - 1, 0), 0)),
                  pl.BlockSpec((None, SUBLANE, cin), lambda b, i: (b, jnp.minimum((i + 1) * r8, nb8 - 1), 0)),
                  full((1, cin)), full((1, 2 * w)), full((LANE, 2 * w)), full((1, 2 * w)), full((LANE, 2 * w)),
                  full((RW_GATE_LORA, w)), full((1, w)), full((1, w)), full((w, w))],
        out_specs=[out_spec] * 10,
        out_shape=[jax.ShapeDtypeStruct((bsz, s, w), F32)] * 10,
        scratch_shapes=[pltpu.VMEM((ts + 2 * SUBLANE, cin), F32)],
        compiler_params=_params("parallel", "parallel"),
        name="rw_prep",
    )(p1, p1, p1, mu.reshape(1, cin), w0.reshape(1, 2 * w), _block_diag2(w2), a0.reshape(1, 2 * w),
      _block_diag2(a2), g2, k_k.reshape(1, w), k_a.reshape(1, w), _head_sum_matrix(w, RW_HEAD_DIM))


def _rw_direction(r_ref, v_ref, kk_ref, lw_ref, kd_ref, b_ref, o_ref, state_ref, *, sc, reverse):
    chunk = RW_CHUNK
    nck = sc // chunk
    hd = RW_HEAD_DIM
    masks = _ChunkMasks(sc, chunk, reverse)
    lw = lw_ref[...]
    lg = _dot(masks.incl.astype(F32), lw, HI)
    lt = _dot(masks.same.astype(F32), lw, HI)
    e_neg = jnp.exp(-lg)
    kkt = kk_ref[...] * jnp.exp(lg - lw)
    khat = kd_ref[...] * e_neg
    bhat = b_ref[...] * e_neg
    rt = r_ref[...] * jnp.exp(lg)
    e_rest = jnp.exp(lt - lg)
    kbar = kd_ref[...] * e_rest
    bbar = b_ref[...] * e_rest
    e_tot = jnp.exp(lt)
    order = range(nck - 1, -1, -1) if reverse else range(nck)
    steps = []
    for h in range(RW_HEADS):
        cols = slice(h * hd, (h + 1) * hd)
        v = v_ref[:, cols]
        kkt_h = kkt[:, cols]
        khat_h = khat[:, cols]
        bhat_h = bhat[:, cols]
        rt_h = rt[:, cols]
        l_a = jnp.where(masks.strict, _dot_nt(kkt_h, khat_h), 0.0)
        a_qb = jnp.where(masks.incl, _dot_nt(rt_h, bhat_h), 0.0)
        steps.append(dict(
            l_b=jnp.where(masks.strict, _dot_nt(kkt_h, bhat_h), 0.0), kkt=kkt_h, la_v=_dot(l_a, v), rt=rt_h,
            y0=_dot(jnp.where(masks.incl, _dot_nt(rt_h, khat_h), 0.0), v),
            a_qb=[a_qb[c * chunk:(c + 1) * chunk, c * chunk:(c + 1) * chunk] for c in range(nck)],
            v=v, kbar=kbar[:, cols], bbar=bbar[:, cols], gl=e_tot[:, cols],
            cols=cols, o_ref=o_ref, state_ref=state_ref, h=h, order=order))
    for st, t in zip(steps, _unit_tri_inverses([st.pop('l_b') for st in steps], masks)):
        st['wm'] = _dot(t, st.pop('kkt'))
        st['u'] = _dot(t, st.pop('la_v'))
    return steps


def _rw_recurrence(steps, chunk):
    states = [st['state_ref'][st['h']] for st in steps]
    for pos in range(len(steps[0]['order'])):
        for n, st in enumerate(steps):
            c = st['order'][pos]
            rows = slice(c * chunk, (c + 1) * chunk)
            p = _dot_nt(st['wm'][rows], states[n]) + st['u'][rows]
            st['o_ref'][rows, st['cols']] = (st['y0'][rows] + _dot_nt(st['rt'][rows], states[n])
                                             - _dot(st['a_qb'][c], p))
            states[n] = (states[n] * st['gl'][c * chunk:c * chunk + 1] + _dot_tn(st['v'][rows], st['kbar'][rows])
                         - _dot_tn(p, st['bbar'][rows]))
    for n, st in enumerate(steps):
        st['state_ref'][st['h']] = states[n]


def _rw_chunk_kernel(rf, vf, kkf, lwf, kdf, bf, rr, vr, kkr, lwr, kdr, br, of_ref, or_ref, sf_ref, sr_ref, *, sc):
    @pl.when(pl.program_id(1) == 0)
    def _():
        sf_ref[...] = jnp.zeros_like(sf_ref)
        sr_ref[...] = jnp.zeros_like(sr_ref)

    steps = (_rw_direction(rf, vf, kkf, lwf, kdf, bf, of_ref, sf_ref, sc=sc, reverse=False)
             + _rw_direction(rr, vr, kkr, lwr, kdr, br, or_ref, sr_ref, sc=sc, reverse=True))
    _rw_recurrence(steps, RW_CHUNK)


def _rw_chunked(r, v, kk, lw0, lw1, kd0, kd1, b0, b1, sc=256):
    bsz, s, w = r.shape
    n = s // sc
    fwd = pl.BlockSpec((None, sc, w), lambda b, i: (b, i, 0))
    rev = pl.BlockSpec((None, sc, w), lambda b, i: (b, n - 1 - i, 0))
    state = pltpu.VMEM((RW_HEADS, RW_HEAD_DIM, RW_HEAD_DIM), F32)
    return pl.pallas_call(
        functools.partial(_rw_chunk_kernel, sc=sc),
        grid=(bsz, n),
        in_specs=[fwd] * 6 + [rev] * 6,
        out_specs=[fwd, rev],
        out_shape=[jax.ShapeDtypeStruct((bsz, s, w), F32)] * 2,
        scratch_shapes=[state, state],
        compiler_params=_params("parallel", "arbitrary"),
        name="rw_chunked",
    )(r, v, kk, lw0, kd0, b0, r, v, kk, lw1, kd1, b1)


def _rw_post_kernel(yf_ref, yr_ref, r_ref, v_ref, g_ref, kd0_ref, kd1_ref, rk_ref, lnw_ref, lnb_ref, hsum_ref, o_ref):
    inv = 1.0 / RW_HEAD_DIM
    y = yf_ref[...] + yr_ref[...]
    mean = _dot(y, hsum_ref[...], HI) * inv
    cen = y - mean
    var = _dot(cen * cen, hsum_ref[...], HI) * inv
    y = cen * lax.rsqrt(var + RW_GN_EPS) * lnw_ref[...] + lnb_ref[...]
    k_bonus = 0.5 * (kd0_ref[...] + kd1_ref[...])
    y = y + _dot(r_ref[...] * k_bonus * rk_ref[...], hsum_ref[...], HI) * v_ref[...]
    o_ref[...] = y * g_ref[...]


def _rw_post(y_f, y_r, r, v, g, kd0, kd1, r_k, ln_w, ln_b, ts=512):
    bsz, s, w = r.shape
    spec = pl.BlockSpec((None, ts, w), lambda b, i: (b, i, 0))
    row = pl.BlockSpec((1, w), lambda b, i: (0, 0))
    return pl.pallas_call(
        _rw_post_kernel,
        grid=(bsz, s // ts),
        in_specs=[spec] * 7 + [row, row, row, pl.BlockSpec((w, w), lambda b, i: (0, 0))],
        out_specs=spec,
        out_shape=jax.ShapeDtypeStruct((bsz, s, w), F32),
        compiler_params=_params("parallel", "parallel"),
        name="rw_post",
    )(y_f, y_r, r, v, g, kd0, kd1, r_k.reshape(1, w), ln_w.reshape(1, w), ln_b.reshape(1, w),
      _head_sum_matrix(w, RW_HEAD_DIM))


def _rwkv7_mixer(p1, mu, w0, w2, a0, a2, g2, k_k, k_a, r_k, ln_w, ln_b):
    r, v, kk, g, lw0, lw1, kd0, kd1, b0, b1 = _rw_prep(p1, mu, w0, w2, a0, a2, g2, k_k, k_a)
    y_f, y_r = _rw_chunked(r, v, kk, lw0, lw1, kd0, kd1, b0, b1)
    return _rw_post(y_f, y_r, r, v, g, kd0, kd1, r_k, ln_w, ln_b)


def _diff_attn_kernel(q_ref, k_ref, v_ref, lam_ref, sub_ref, o_ref, *, tq, lambda_init):
    h = pl.program_id(1)
    i = pl.program_id(2)
    s = k_ref.shape[0]
    hd = DF_HEAD_DIM
    slope = jnp.where(h == 0, 2.0 ** -2, jnp.where(h == 1, 2.0 ** -4, jnp.where(h == 2, 2.0 ** -6, 2.0 ** -8)))
    lp = lam_ref[...]
    lam = (jnp.exp(jnp.sum(lp[0:1] * lp[1:2], axis=1, keepdims=True))
           - jnp.exp(jnp.sum(lp[2:3] * lp[3:4], axis=1, keepdims=True)) + lambda_init)
    qpos = (i * tq + lax.broadcasted_iota(jnp.int32, (tq, 1), 0)).astype(F32)
    kpos = lax.broadcasted_iota(jnp.int32, (1, s), 1).astype(F32)
    bias = (slope.astype(F32) * LOG2_E) * jnp.abs(qpos - kpos)
    v = v_ref[...].astype(BF16)
    outs = []
    for m in range(2):
        q = q_ref[:, m * hd:(m + 1) * hd] * (hd ** -0.5 * LOG2_E)
        sc = _dot_nt(q, k_ref[:, m * hd:(m + 1) * hd]) - bias
        ex = jnp.exp2(sc - jnp.max(sc, axis=1, keepdims=True))
        outs.append(_dot(ex, v) / jnp.sum(ex, axis=1, keepdims=True))
    o = outs[0] - lam * outs[1]
    o_ref[...] = _rms(o, 1e-5) * sub_ref[...] * (1.0 - lambda_init)


def _diff_attention(p1, lam_params, subln_w, lambda_init, tq=256):
    bsz, s, _ = p1.shape
    hw = 2 * DF_HEAD_DIM
    q0 = RW_IN // hw
    return pl.pallas_call(
        functools.partial(_diff_attn_kernel, tq=tq, lambda_init=lambda_init),
        grid=(bsz, DF_HEADS, s // tq),
        in_specs=[pl.BlockSpec((None, tq, hw), lambda b, h, i: (b, i, q0 + h)),
                  pl.BlockSpec((None, s, hw), lambda b, h, i: (b, 0, q0 + DF_HEADS + h)),
                  pl.BlockSpec((None, s, hw), lambda b, h, i: (b, 0, q0 + 2 * DF_HEADS + h)),
                  pl.BlockSpec((4, DF_HEAD_DIM), lambda b, h, i: (0, 0)),
                  pl.BlockSpec((1, hw), lambda b, h, i: (0, 0))],
        out_specs=pl.BlockSpec((None, tq, hw), lambda b, h, i: (b, i, h)),
        out_shape=jax.ShapeDtypeStruct((bsz, s, DF_HEADS * hw), F32),
        compiler_params=_params("parallel", "parallel", "parallel"),
        name="diff_attn",
    )(p1, p1, p1, lam_params, subln_w.reshape(1, hw))


def _layer0_weights(ab_w_in):
    d = ab_w_in.shape[0]
    main = 4 * DN_WIDTH
    gates = 4 * DN_HEADS
    w = jnp.concatenate([ab_w_in[:, :main], ab_w_in[:, main + gates:], ab_w_in[:, main:main + gates],
                         jnp.zeros((d, LANE - gates), ab_w_in.dtype)], axis=1)
    return w.astype(BF16)


def kernel(x, c, ada_w, ada_b, norm_mix_pre, norm_mix_post, norm_ffn_pre, norm_ffn_post, router_w, exp_w_gate, exp_w_up, exp_w_down, ab_w_in, ab_w_out, dn_conv, dn_a_log, dn_dt_bias, dn_norm, s5_a_re, s5_a_im, s5_log_dt, s5_b_re, s5_b_im, s5_c_re, s5_c_im, s5_d, s5_glu_w, s5_glu_b, cd_w_in, cd_w_out, rw_mu, rw_w0, rw_w2, rw_a0, rw_a2, rw_g2, rw_k_k, rw_k_a, rw_r_k, rw_ln_w, rw_ln_b, df_lambda, df_subln):
    depth = ada_w.shape[0]
    mod_all = _ada_mod(c, ada_w, ada_b)
    for layer in range(depth):
        mod = mod_all[layer][:, None, :]
        i = layer // 2
        if layer % 2 == 0:
            p = _in_proj(x, norm_mix_pre[layer], mod, _layer0_weights(ab_w_in[i]))
            ya = _gated_deltanet(p, dn_conv[i], dn_a_log[i], dn_dt_bias[i], dn_norm[i])
            yb = _s5_mixer(p, s5_a_re[i], s5_a_im[i], s5_log_dt[i], s5_b_re[i], s5_b_im[i], s5_c_re[i], s5_c_im[i],
                           s5_d[i], s5_glu_w[i], s5_glu_b[i])
            w_out = ab_w_out[i]
        else:
            lambda_init = 0.8 - 0.6 * math.exp(-0.3 * layer)
            p = _in_proj(x, norm_mix_pre[layer], mod, cd_w_in[i].astype(BF16))
            ya = _rwkv7_mixer(p, rw_mu[i], rw_w0[i], rw_w2[i], rw_a0[i], rw_a2[i], rw_g2[i], rw_k_k[i], rw_k_a[i],
                              rw_r_k[i].reshape(-1), rw_ln_w[i], rw_ln_b[i])
            yb = _diff_attention(p, df_lambda[i], df_subln[i], lambda_init)
            w_out = cd_w_out[i]
        x1, h2, pt = _mix_out(ya, yb, x, w_out.astype(BF16), norm_mix_post[layer], mod, norm_ffn_pre[layer],
                              router_w[layer])
        x = _expert_choice_ffn(x1, h2, pt, mod, norm_ffn_post[layer], exp_w_gate, exp_w_up, exp_w_down, layer)
    return x
```

```python
import functools
import math

import jax
import jax.numpy as jnp
import numpy as np
from jax import lax
from jax.experimental import pallas as pl
from jax.experimental.pallas import tpu as pltpu

F32 = jnp.float32
BF16 = jnp.bfloat16
HI = lax.Precision.HIGHEST

LANE = 128
SUBLANE = 8
VMEM_LIMIT = 56 * 1024 * 1024

NORM_EPS = 1e-6
DN_HEADS = 4
DN_HEAD_DIM = 128
DN_WIDTH = 512
DN_CONV = 5
DN_CHUNK = 64
S5_WIDTH = 512
S5_GROUP = 16
S5_GROUPS = 32
S5_STATE = 64
S5_CHUNK = 16
RW_WIDTH = 512
RW_HEADS = 8
RW_HEAD_DIM = 64
RW_LORA = 64
RW_GATE_LORA = 128
RW_GN_EPS = 64e-5
RW_CHUNK = 64
RW_IN = 1920
DF_WIDTH = 512
DF_HEADS = 4
DF_HEAD_DIM = 64
N_EXPERTS = 16
EC_CAPACITY_FACTOR = 2
NEG_BIG = -1e30
LOG2_E = 1.4426950408889634
MOE_TOKEN_TILE = 256
TOPK_BISECTIONS = 152


def _params(*sem):
    return pltpu.CompilerParams(dimension_semantics=sem, vmem_limit_bytes=VMEM_LIMIT)


def _sigmoid(x):
    return 1.0 / (1.0 + jnp.exp(-x))


def _silu(x):
    return x * _sigmoid(x)


def _softplus(x):
    return jnp.maximum(x, 0.0) + jnp.log(1.0 + jnp.exp(-jnp.abs(x)))


def _rms(x, eps):
    return x * lax.rsqrt(jnp.mean(x * x, axis=-1, keepdims=True) + eps)


def _mxu_operands(a, b, precision):
    if precision is None:
        return a.astype(BF16), b.astype(BF16)
    return a, b


def _dot(a, b, precision=None):
    a, b = _mxu_operands(a, b, precision)
    return jnp.dot(a, b, preferred_element_type=F32, precision=precision)


def _dot_nt(a, b, precision=None):
    a, b = _mxu_operands(a, b, precision)
    return lax.dot_general(a, b, (((1,), (1,)), ((), ())), preferred_element_type=F32, precision=precision)


def _dot_tn(a, b, precision=None):
    a, b = _mxu_operands(a, b, precision)
    return lax.dot_general(a, b, (((0,), (0,)), ((), ())), preferred_element_type=F32, precision=precision)


def _ada_kernel(c_ref, w_ref, b_ref, o_ref):
    o_ref[...] = _dot(_silu(c_ref[...]), w_ref[...], HI) + b_ref[...]


def _ada_mod(c, ada_w, ada_b):
    depth, d, n = ada_w.shape
    bsz = c.shape[0]
    rows = -(-bsz // SUBLANE) * SUBLANE
    c_pad = jnp.zeros((rows, d), F32).at[:bsz].set(c)
    out = pl.pallas_call(
        _ada_kernel,
        grid=(depth, n // d),
        in_specs=[pl.BlockSpec((rows, d), lambda l, j: (0, 0)),
                  pl.BlockSpec((None, d, d), lambda l, j: (l, 0, j)),
                  pl.BlockSpec((None, 1, d), lambda l, j: (l, 0, j))],
        out_specs=pl.BlockSpec((None, rows, d), lambda l, j: (l, 0, j)),
        out_shape=jax.ShapeDtypeStruct((depth, rows, n), F32),
        compiler_params=_params("parallel", "parallel"),
        name="ada_mod",
    )(c_pad, ada_w, ada_b.reshape(depth, 1, n))
    return out[:, :bsz]


def _in_proj_kernel(x_ref, gain_ref, sh_ref, sc_ref, w_ref, o_ref):
    h = _rms(x_ref[...], NORM_EPS) * gain_ref[...] * (1.0 + sc_ref[...]) + sh_ref[...]
    o_ref[...] = _dot(h.astype(BF16), w_ref[...])


def _in_proj(x, gain, mod, w_bf16, tm=256):
    bsz, s, d = x.shape
    n = w_bf16.shape[1]
    return pl.pallas_call(
        _in_proj_kernel,
        grid=(bsz, s // tm),
        in_specs=[pl.BlockSpec((None, tm, d), lambda b, i: (b, i, 0)),
                  pl.BlockSpec((1, d), lambda b, i: (0, 0)),
                  pl.BlockSpec((None, 1, d), lambda b, i: (b, 0, 0)),
                  pl.BlockSpec((None, 1, d), lambda b, i: (b, 0, 1)),
                  pl.BlockSpec((d, n), lambda b, i: (0, 0))],
        out_specs=pl.BlockSpec((None, tm, n), lambda b, i: (b, i, 0)),
        out_shape=jax.ShapeDtypeStruct((bsz, s, n), F32),
        compiler_params=_params("parallel", "parallel"),
        name="in_proj",
    )(x, gain.reshape(1, d), mod, mod, w_bf16)


def _split3(x):
    hi = x.astype(BF16)
    rest = x - hi.astype(F32)
    mid = rest.astype(BF16)
    return hi, mid, (rest - mid.astype(F32)).astype(BF16)


def _mask_dot(mask, x):
    m = mask.astype(BF16)
    return sum(jnp.dot(m, part, preferred_element_type=F32) for part in _split3(x))


def _dot_mask(x, mask):
    m = mask.astype(BF16)
    return sum(jnp.dot(part, m, preferred_element_type=F32) for part in _split3(x))


def _mask_dot_nt(x, mask):
    m = mask.astype(BF16)
    return sum(lax.dot_general(part, m, (((1,), (1,)), ((), ())), preferred_element_type=F32)
               for part in _split3(x))


class _ChunkMasks:
    def __init__(self, n, chunk, reverse):
        ii = lax.broadcasted_iota(jnp.int32, (n, n), 0)
        jj = lax.broadcasted_iota(jnp.int32, (n, n), 1)
        same = (ii // chunk) == (jj // chunk)
        tri = (ii <= jj) if reverse else (ii >= jj)
        self.same = same
        self.incl = same & tri
        self.strict = self.incl & (ii != jj)
        self.eye = (ii == jj).astype(F32)
        m16 = (ii // 16) == (jj // 16)
        m32 = (ii // 32) == (jj // 32)
        self.m16 = m16.astype(F32)
        self.m32_only = (m32 & ~m16).astype(F32)
        self.m64_only = (same & ~m32).astype(F32)


def _unit_tri_inverses(lms, masks):
    ps = [-(lm * masks.m16) for lm in lms]
    ts = [masks.eye + p for p in ps]
    for _ in range(3):
        ps = [_dot(p, p) for p in ps]
        ts = [t + _dot(t, p) for t, p in zip(ts, ps)]
    for level in (masks.m32_only, masks.m64_only):
        cs = [_dot(lm * level, t) for lm, t in zip(lms, ts)]
        ts = [t - _dot(t, c) for t, c in zip(ts, cs)]
    return ts


def _dn_prep_kernel(cur_ref, prev_ref, next_ref, gates_ref, convw_ref, alog_ref, dtb_ref,
                    qkv_ref, gb_ref, ext_ref, *, ts):
    i = pl.program_id(1)
    n = pl.num_programs(1)
    halo = SUBLANE
    ext_ref[0:halo, :] = jnp.where(i > 0, prev_ref[...], 0.0)
    ext_ref[halo:halo + ts, :] = cur_ref[...]
    ext_ref[halo + ts:2 * halo + ts, :] = jnp.where(i < n - 1, next_ref[...], 0.0)
    pad = DN_CONV // 2
    for blk in range(3 * DN_HEADS):
        cols = slice(blk * DN_HEAD_DIM, (blk + 1) * DN_HEAD_DIM)
        acc = jnp.zeros((ts, DN_HEAD_DIM), F32)
        for tap in range(DN_CONV):
            acc = acc + ext_ref[halo - pad + tap:halo - pad + tap + ts, cols] * convw_ref[tap:tap + 1, cols]
        val = _silu(acc)
        if blk < 2 * DN_HEADS:
            val = val * lax.rsqrt(jnp.sum(val * val, axis=-1, keepdims=True) + 1e-6)
            if blk < DN_HEADS:
                val = val * (DN_HEAD_DIM ** -0.5)
        qkv_ref[:, cols] = val
    gin = gates_ref[...]
    g = -jnp.exp(alog_ref[...]) * _softplus(gin + dtb_ref[...])
    lane = lax.broadcasted_iota(jnp.int32, gin.shape, 1)
    gb_ref[...] = jnp.where(lane < 2 * DN_HEADS, g, _sigmoid(gin))


def _dn_prep(p0, conv_w, a_log, dt_bias, ts=512):
    bsz, s, _ = p0.shape
    c3 = 3 * DN_WIDTH
    gate_blk = (4 * DN_WIDTH + S5_WIDTH) // LANE
    nb8 = s // SUBLANE
    r8 = ts // SUBLANE
    alog = jnp.zeros((1, LANE), F32).at[0, :2 * DN_HEADS].set(a_log.reshape(-1))
    dtb = jnp.zeros((1, LANE), F32).at[0, :2 * DN_HEADS].set(dt_bias.reshape(-1))
    return pl.pallas_call(
        functools.partial(_dn_prep_kernel, ts=ts),
        grid=(bsz, s // ts),
        in_specs=[pl.BlockSpec((None, ts, c3), lambda b, i: (b, i, 0)),
                  pl.BlockSpec((None, SUBLANE, c3), lambda b, i: (b, jnp.maximum(i * r8 - 1, 0), 0)),
                  pl.BlockSpec((None, SUBLANE, c3), lambda b, i: (b, jnp.minimum((i + 1) * r8, nb8 - 1), 0)),
                  pl.BlockSpec((None, ts, LANE), lambda b, i: (b, i, gate_blk)),
                  pl.BlockSpec((DN_CONV, c3), lambda b, i: (0, 0)),
                  pl.BlockSpec((1, LANE), lambda b, i: (0, 0)),
                  pl.BlockSpec((1, LANE), lambda b, i: (0, 0))],
        out_specs=[pl.BlockSpec((None, ts, c3), lambda b, i: (b, i, 0)),
                   pl.BlockSpec((None, ts, LANE), lambda b, i: (b, i, 0))],
        out_shape=[jax.ShapeDtypeStruct((bsz, s, c3), F32),
                   jax.ShapeDtypeStruct((bsz, s, LANE), F32)],
        scratch_shapes=[pltpu.VMEM((ts + 2 * SUBLANE, c3), F32)],
        compiler_params=_params("parallel", "parallel"),
        name="dn_prep",
    )(p0, p0, p0, p0, conv_w, alog, dtb)


def _dn_direction(q_ref, k_ref, v_ref, gb_ref, gbt_ref, o_ref, state_ref, *, sc, reverse, direction):
    chunk = DN_CHUNK
    nck = sc // chunk
    masks = _ChunkMasks(sc, chunk, reverse)
    gcols = gb_ref[...]
    gc_cols = _mask_dot(masks.incl, gcols)
    gt_cols = _mask_dot(masks.same, gcols)
    gc_rows = _mask_dot_nt(gbt_ref[...], masks.incl)
    e_gc = jnp.exp(gc_cols)
    e_rest = jnp.exp(gt_cols - gc_cols)
    e_tot = jnp.exp(gt_cols)
    order = range(nck - 1, -1, -1) if reverse else range(nck)
    steps = []
    for h in range(DN_HEADS):
        gi = direction * DN_HEADS + h
        bi = 2 * DN_HEADS + gi
        cols = slice(h * DN_HEAD_DIM, (h + 1) * DN_HEAD_DIM)
        q = q_ref[:, cols]
        k = k_ref[:, cols]
        beta = gcols[:, bi:bi + 1]
        eg = e_gc[:, gi:gi + 1]
        decay = jnp.exp(jnp.where(masks.incl, gc_cols[:, gi:gi + 1] - gc_rows[gi:gi + 1, :], NEG_BIG))
        kb = k * beta
        attn = _dot_nt(q, k) * decay
        steps.append(dict(
            lm=jnp.where(masks.strict, _dot_nt(kb, k) * decay, 0.0), vb=v_ref[:, cols] * beta, kbg=kb * eg,
            attn=[attn[c * chunk:(c + 1) * chunk, c * chunk:(c + 1) * chunk] for c in range(nck)], qd=q * eg,
            kd=k * e_rest[:, gi:gi + 1], gl=e_tot[:, gi:gi + 1], cols=cols, o_ref=o_ref, state_ref=state_ref, h=h,
            order=order))
    for st, t in zip(steps, _unit_tri_inverses([st.pop('lm') for st in steps], masks)):
        st['u'] = _dot(t, st.pop('vb'))
        st['w'] = _dot(t, st.pop('kbg'))
    return steps


def _dn_recurrence(steps, chunk):
    states = [st['state_ref'][st['h']] for st in steps]
    for pos in range(len(steps[0]['order'])):
        for n, st in enumerate(steps):
            c = st['order'][pos]
            rows = slice(c * chunk, (c + 1) * chunk)
            v_new = st['u'][rows] - _dot(st['w'][rows], states[n])
            st['o_ref'][rows, st['cols']] = _dot(st['qd'][rows], states[n]) + _dot(st['attn'][c], v_new)
            states[n] = states[n] * st['gl'][c * chunk:c * chunk + 1] + _dot_tn(st['kd'][rows], v_new)
    for n, st in enumerate(steps):
        st['state_ref'][st['h']] = states[n]


def _dn_chunk_kernel(qf, kf, vf, gbf, gbtf, qr, kr, vr, gbr, gbtr, of_ref, or_ref, sf_ref, sr_ref, *, sc):
    @pl.when(pl.program_id(1) == 0)
    def _():
        sf_ref[...] = jnp.zeros_like(sf_ref)
        sr_ref[...] = jnp.zeros_like(sr_ref)

    steps = (_dn_direction(qf, kf, vf, gbf, gbtf, of_ref, sf_ref, sc=sc, reverse=False, direction=0)
             + _dn_direction(qr, kr, vr, gbr, gbtr, or_ref, sr_ref, sc=sc, reverse=True, direction=1))
    _dn_recurrence(steps, DN_CHUNK)


def _dn_chunked(qkv, gb, sc=256):
    bsz, s, _ = qkv.shape
    n = s // sc
    gbt = jnp.swapaxes(gb[..., :4 * DN_HEADS], 1, 2)
    fwd = lambda col: pl.BlockSpec((None, sc, DN_WIDTH), lambda b, i: (b, i, col))
    rev = lambda col: pl.BlockSpec((None, sc, DN_WIDTH), lambda b, i: (b, n - 1 - i, col))
    in_specs = ([fwd(0), fwd(1), fwd(2),
                 pl.BlockSpec((None, sc, LANE), lambda b, i: (b, i, 0)),
                 pl.BlockSpec((None, 4 * DN_HEADS, sc), lambda b, i: (b, 0, i))]
                + [rev(0), rev(1), rev(2),
                   pl.BlockSpec((None, sc, LANE), lambda b, i: (b, n - 1 - i, 0)),
                   pl.BlockSpec((None, 4 * DN_HEADS, sc), lambda b, i: (b, 0, n - 1 - i))])
    state = pltpu.VMEM((DN_HEADS, DN_HEAD_DIM, DN_HEAD_DIM), F32)
    return pl.pallas_call(
        functools.partial(_dn_chunk_kernel, sc=sc),
        grid=(bsz, n),
        in_specs=in_specs,
        out_specs=[pl.BlockSpec((None, sc, DN_WIDTH), lambda b, i: (b, i, 0)),
                   pl.BlockSpec((None, sc, DN_WIDTH), lambda b, i: (b, n - 1 - i, 0))],
        out_shape=[jax.ShapeDtypeStruct((bsz, s, DN_WIDTH), F32)] * 2,
        scratch_shapes=[state, state],
        compiler_params=_params("parallel", "arbitrary"),
        name="dn_chunked",
    )(qkv, qkv, qkv, gb, gbt, qkv, qkv, qkv, gb, gbt)


def _dn_post_kernel(of_ref, or_ref, z_ref, nw_ref, o_ref):
    for h in range(DN_HEADS):
        cols = slice(h * DN_HEAD_DIM, (h + 1) * DN_HEAD_DIM)
        o = of_ref[:, cols] + or_ref[:, cols]
        o_ref[:, cols] = _rms(o, NORM_EPS) * nw_ref[...] * _silu(z_ref[:, cols])


def _dn_post(o_f, o_r, p0, norm_w, ts=512):
    bsz, s, w = o_f.shape
    z_blk = 3 * DN_WIDTH // w
    spec = pl.BlockSpec((None, ts, w), lambda b, i: (b, i, 0))
    return pl.pallas_call(
        _dn_post_kernel,
        grid=(bsz, s // ts),
        in_specs=[spec, spec,
                  pl.BlockSpec((None, ts, w), lambda b, i: (b, i, z_blk)),
                  pl.BlockSpec((1, DN_HEAD_DIM), lambda b, i: (0, 0))],
        out_specs=spec,
        out_shape=jax.ShapeDtypeStruct((bsz, s, w), F32),
        compiler_params=_params("parallel", "parallel"),
        name="dn_post",
    )(o_f, o_r, p0, norm_w.reshape(1, DN_HEAD_DIM))


def _gated_deltanet(p0, conv_w, a_log, dt_bias, norm_w):
    qkv, gb = _dn_prep(p0, conv_w, a_log, dt_bias)
    o_f, o_r = _dn_chunked(qkv, gb)
    return _dn_post(o_f, o_r, p0, norm_w)


def _s5_tables(a_re, a_im, log_dt, b_re, b_im, c_re, c_im, n_chunks):
    lc, p, hh = S5_CHUNK, S5_STATE, S5_GROUP
    levels = int(math.log2(n_chunks))
    dt = jnp.exp(log_dt)[..., None]
    mag = jnp.exp(a_re * dt)
    lb_re, lb_im = mag * jnp.cos(a_im * dt), mag * jnp.sin(a_im * dt)
    den = a_re * a_re + a_im * a_im
    nr, ni = lb_re - 1.0, lb_im
    f_re = (nr * a_re + ni * a_im) / den
    f_im = (ni * a_re - nr * a_im) / den
    bb_re = f_re[..., None] * b_re - f_im[..., None] * b_im
    bb_im = f_re[..., None] * b_im + f_im[..., None] * b_re

    def power(tau):
        tau = jnp.asarray(tau, F32)[:, None]
        m = jnp.exp((a_re * dt)[:, :, None, :] * tau)
        ang = (a_im * dt)[:, :, None, :] * tau
        return m * jnp.cos(ang), m * jnp.sin(ang)

    pw_re, pw_im = power(np.arange(lc + 1))
    bbt_re, bbt_im = jnp.swapaxes(bb_re, 2, 3), jnp.swapaxes(bb_im, 2, 3)
    lbb_re = pw_re[:, :, :, None, :] * bbt_re[:, :, None, :, :] - pw_im[:, :, :, None, :] * bbt_im[:, :, None, :, :]
    lbb_im = pw_re[:, :, :, None, :] * bbt_im[:, :, None, :, :] + pw_im[:, :, :, None, :] * bbt_re[:, :, None, :, :]
    kt = (jnp.einsum('dgop,dgtip->dgito', c_re, lbb_re[:, :, :lc], precision=HI)
          - jnp.einsum('dgop,dgtip->dgito', c_im, lbb_im[:, :, :lc], precision=HI))
    strips = jnp.stack([kt[0], kt[1][:, :, ::-1, :]]).reshape(2, -1, hh, lc * hh)
    bmats, cmats = [], []
    for d in range(2):
        e_in = (lc - 1 - np.arange(lc)) if d == 0 else np.arange(lc)
        bmats.append(jnp.concatenate([lbb_re[d][:, e_in], lbb_im[d][:, e_in]], axis=-1).reshape(-1, lc * hh, 2 * p))
        e_out = (np.arange(lc) + 1) if d == 0 else (lc - np.arange(lc))
        sel_re, sel_im = pw_re[d][:, e_out, None, :], pw_im[d][:, e_out, None, :]
        m_re = c_re[d][:, None] * sel_re - c_im[d][:, None] * sel_im
        m_im = c_re[d][:, None] * sel_im + c_im[d][:, None] * sel_re
        cmats.append(jnp.concatenate([m_re, -m_im], axis=-1).reshape(-1, lc * hh, 2 * p))
    sq_re, sq_im = power(lc * 2.0 ** np.arange(levels))
    scan_a = jnp.concatenate([sq_re, sq_re], axis=-1)
    scan_b = jnp.concatenate([-sq_im, sq_im], axis=-1)
    return strips, jnp.stack(bmats), jnp.stack(cmats), scan_a, scan_b


def _s5_kernel(u_ref, strip_ref, bmat_ref, cmat_ref, sa_ref, sb_ref, y_ref, toep_ref, *, n_chunks):
    u = u_ref[...]
    n = u.shape[0]
    p2 = 2 * S5_STATE
    lc, hh = S5_CHUNK, S5_GROUP
    levels = sa_ref.shape[1]
    cidx = lax.broadcasted_iota(jnp.int32, (n, p2), 0) & (n_chunks - 1)
    lane = lax.broadcasted_iota(jnp.int32, (hh, lc * hh), 1)
    y = None
    for d in range(2):
        strip = strip_ref[d]
        for ti in range(lc):
            if d == 0:
                blk = jnp.where(lane >= ti * hh, pltpu.roll(strip, ti * hh, 1), 0.0)
            else:
                blk = jnp.where(lane < (ti + 1) * hh, pltpu.roll(strip, ((ti + 1) * hh) % (lc * hh), 1), 0.0)
            toep_ref[ti * hh:(ti + 1) * hh, :] = blk
        x = _dot(u, bmat_ref[d])
        for lv in range(levels):
            sh = 1 << lv
            if d == 0:
                xs = jnp.where(cidx >= sh, pltpu.roll(x, sh, 0), 0.0)
            else:
                xs = jnp.where(cidx < n_chunks - sh, pltpu.roll(x, n - sh, 0), 0.0)
            x = x + sa_ref[d, lv:lv + 1, :] * xs + sb_ref[d, lv:lv + 1, :] * pltpu.roll(xs, S5_STATE, 1)
        if d == 0:
            x_in = jnp.where(cidx >= 1, pltpu.roll(x, 1, 0), 0.0)
        else:
            x_in = jnp.where(cidx < n_chunks - 1, pltpu.roll(x, n - 1, 0), 0.0)
        yd = _dot(u, toep_ref[...]) + _dot_nt(x_in, cmat_ref[d])
        y = yd if y is None else y + yd
    y_ref[...] = y


def _s5_scan(u, tables):
    bsz, s, _ = u.shape
    lc, g, hh = S5_CHUNK, S5_GROUPS, S5_GROUP
    nc = s // lc
    strips, bmat, cmat, sa, sb = tables
    uc = jnp.transpose(u.reshape(bsz, nc, lc, g, hh), (3, 0, 1, 2, 4)).reshape(g, bsz * nc, lc * hh)
    levels = sa.shape[2]
    y = pl.pallas_call(
        functools.partial(_s5_kernel, n_chunks=nc),
        grid=(g,),
        in_specs=[pl.BlockSpec((None, bsz * nc, lc * hh), lambda i: (i, 0, 0)),
                  pl.BlockSpec((2, None, hh, lc * hh), lambda i: (0, i, 0, 0)),
                  pl.BlockSpec((2, None, lc * hh, 2 * S5_STATE), lambda i: (0, i, 0, 0)),
                  pl.BlockSpec((2, None, lc * hh, 2 * S5_STATE), lambda i: (0, i, 0, 0)),
                  pl.BlockSpec((2, None, levels, 2 * S5_STATE), lambda i: (0, i, 0, 0)),
                  pl.BlockSpec((2, None, levels, 2 * S5_STATE), lambda i: (0, i, 0, 0))],
        out_specs=pl.BlockSpec((None, bsz * nc, lc * hh), lambda i: (i, 0, 0)),
        out_shape=jax.ShapeDtypeStruct((g, bsz * nc, lc * hh), F32),
        scratch_shapes=[pltpu.VMEM((lc * hh, lc * hh), F32)],
        compiler_params=_params("parallel"),
        name="s5_scan",
    )(uc, strips, bmat, cmat, sa, sb)
    return jnp.transpose(y.reshape(g, bsz, nc, lc, hh), (1, 2, 3, 0, 4)).reshape(bsz, s, g * hh)


def _gelu_tanh(x):
    return 0.5 * x * (1.0 + jnp.tanh(math.sqrt(2.0 / math.pi) * (x + 0.044715 * (x * x * x))))


def _s5_post_kernel(u_ref, y_ref, d_ref, w_ref, b_ref, o_ref):
    y = _gelu_tanh(d_ref[...] * u_ref[...] + y_ref[...])
    o_ref[...] = y * _sigmoid(_dot(y, w_ref[...]) + b_ref[...])


def _s5_post(p0, y, d_skip, glu_w, glu_b, ts=512):
    bsz, s, w = y.shape
    u_blk = 4 * DN_WIDTH // w
    spec = pl.BlockSpec((None, ts, w), lambda b, i: (b, i, 0))
    row = pl.BlockSpec((1, w), lambda b, i: (0, 0))
    return pl.pallas_call(
        _s5_post_kernel,
        grid=(bsz, s // ts),
        in_specs=[pl.BlockSpec((None, ts, w), lambda b, i: (b, i, u_blk)), spec, row,
                  pl.BlockSpec((w, w), lambda b, i: (0, 0)), row],
        out_specs=spec,
        out_shape=jax.ShapeDtypeStruct((bsz, s, w), F32),
        compiler_params=_params("parallel", "parallel"),
        name="s5_post",
    )(p0, y, d_skip.reshape(1, w), glu_w, glu_b.reshape(1, w))


def _s5_mixer(p0, a_re, a_im, log_dt, b_re, b_im, c_re, c_im, d_skip, glu_w, glu_b):
    s = p0.shape[1]
    u = p0[..., 4 * DN_WIDTH:4 * DN_WIDTH + S5_WIDTH]
    tables = _s5_tables(a_re, a_im, log_dt, b_re, b_im, c_re, c_im, s // S5_CHUNK)
    return _s5_post(p0, _s5_scan(u, tables), d_skip, glu_w, glu_b)


def _mix_out_kernel(ya_ref, yb_ref, x_ref, w_ref, npost_ref, g1_ref, npre_ref, sh2_ref, sc2_ref, rwt_ref,
                    x1_ref, h2_ref, pt_ref):
    half = ya_ref.shape[-1]
    y = _dot(ya_ref[...].astype(BF16), w_ref[0:half, :]) + _dot(yb_ref[...].astype(BF16), w_ref[half:2 * half, :])
    x1 = x_ref[...] + g1_ref[...] * (_rms(y, NORM_EPS) * npost_ref[...])
    x1_ref[...] = x1
    h2 = _rms(x1, NORM_EPS) * npre_ref[...] * (1.0 + sc2_ref[...]) + sh2_ref[...]
    h2_ref[...] = h2.astype(BF16)
    logits = _dot_nt(rwt_ref[...], h2, HI)
    ex = jnp.exp(logits - jnp.max(logits, axis=0, keepdims=True))
    pt_ref[...] = ex / jnp.sum(ex, axis=0, keepdims=True)


def _mix_out(ya, yb, x, w_out_bf16, npost, mod, npre, router_w, tm=256):
    bsz, s, d = x.shape
    half = ya.shape[-1]
    ne = router_w.shape[-1]
    row = lambda a: a.reshape(1, d)
    vec = pl.BlockSpec((1, d), lambda b, i: (0, 0))
    modblk = lambda k: pl.BlockSpec((None, 1, d), lambda b, i: (b, 0, k))
    tile = pl.BlockSpec((None, tm, d), lambda b, i: (b, i, 0))
    halfspec = pl.BlockSpec((None, tm, half), lambda b, i: (b, i, 0))
    return pl.pallas_call(
        _mix_out_kernel,
        grid=(bsz, s // tm),
        in_specs=[halfspec, halfspec, tile,
                  pl.BlockSpec((d, d), lambda b, i: (0, 0)), vec, modblk(2), vec, modblk(3), modblk(4),
                  pl.BlockSpec((ne, d), lambda b, i: (0, 0))],
        out_specs=[tile, tile, pl.BlockSpec((None, ne, tm), lambda b, i: (b, 0, i))],
        out_shape=[jax.ShapeDtypeStruct((bsz, s, d), F32),
                   jax.ShapeDtypeStruct((bsz, s, d), BF16),
                   jax.ShapeDtypeStruct((bsz, ne, s), F32)],
        compiler_params=_params("parallel", "parallel"),
        name="mix_out",
    )(ya, yb, x, w_out_bf16, row(npost), mod, row(npre), mod, mod, router_w.T)


def _prefix_lanes(x):
    n = x.shape[1]
    lane = lax.broadcasted_iota(jnp.int32, x.shape, 1)
    sh = 1
    while sh < n:
        x = x + jnp.where(lane >= sh, pltpu.roll(x, sh, 1), 0.0)
        sh *= 2
    return x


def _topk_kernel(p_ref, slot_ref, count_ref, *, cap):
    p = p_ref[...]
    rows = p.shape[0]

    def halve(_, bounds):
        lo, hi = bounds
        mid = 0.5 * (lo + hi)
        ok = jnp.sum((p >= mid).astype(F32), axis=1, keepdims=True) >= cap
        return jnp.where(ok, mid, lo), jnp.where(ok, hi, mid)

    lo, _ = lax.fori_loop(0, TOPK_BISECTIONS, halve, (jnp.zeros((rows, 1), F32), jnp.full((rows, 1), 2.0, F32)))
    gt = p > lo
    eq = p == lo
    need = cap - jnp.sum(gt.astype(F32), axis=1, keepdims=True)
    sel = gt | (eq & (_prefix_lanes(eq.astype(F32)) <= need))
    count = _prefix_lanes(sel.astype(F32))
    slot_ref[...] = jnp.where(sel, count - 1.0, -1.0).astype(jnp.int32)
    count_ref[...] = count.astype(jnp.int32)


def _topk_slots(pt, cap):
    bsz, ne, s = pt.shape
    spec = pl.BlockSpec((bsz * ne, s), lambda i: (0, 0))
    slot, count = pl.pallas_call(
        functools.partial(_topk_kernel, cap=cap),
        grid=(1,), in_specs=[spec], out_specs=[spec, spec],
        out_shape=[jax.ShapeDtypeStruct((bsz * ne, s), jnp.int32)] * 2,
        compiler_params=_params("arbitrary"),
        name="topk_slots",
    )(pt.reshape(bsz * ne, s))
    return slot.reshape(bsz, ne, s), count.reshape(bsz, ne, s)


def _tile_starts(count, tile):
    bsz, ne, _ = count.shape
    ends = count[:, :, tile - 1::tile]
    return jnp.concatenate([jnp.zeros((bsz, ne, 1), jnp.int32), ends], axis=2).reshape(-1)


BF16_ROWS = 2 * SUBLANE


def _expert_windows(starts_ref, ne, win):
    n_tiles = pl.num_programs(1)
    shift = win.bit_length() - 1
    firsts, n_pass = [], 0
    for e in range(ne):
        base = (pl.program_id(0) * ne + e) * (n_tiles + 1) + pl.program_id(1)
        first = (starts_ref[base] // BF16_ROWS) * BF16_ROWS
        firsts.append(first)
        n_pass = jnp.maximum(n_pass, (starts_ref[base + 1] - first + win - 1) >> shift)
    return firsts, n_pass


def _window_hits(slot_row, first, w, cap, win):
    row0 = first + w * win
    src = pl.multiple_of(jnp.minimum(row0, cap - win), BF16_ROWS)
    rows = src + lax.broadcasted_iota(jnp.int32, (win, slot_row.shape[1]), 0)
    return src, (rows == slot_row) & (rows >= row0)


def _gather_kernel(starts_ref, slot_ref, p_ref, h_ref, xin_ref, gate_ref, hit_ref, *, cap, win):
    ne = slot_ref.shape[0]

    @pl.when(pl.program_id(1) == 0)
    def _():
        xin_ref[...] = jnp.zeros_like(xin_ref)
        gate_ref[...] = jnp.zeros_like(gate_ref)

    firsts, n_pass = _expert_windows(starts_ref, ne, win)

    def one_pass(w, carry):
        srcs = []
        for e in range(ne):
            src, hit = _window_hits(slot_ref[e:e + 1, :], firsts[e], w, cap, win)
            srcs.append(src)
            hit_ref[e * win:(e + 1) * win, :] = hit.astype(BF16)
            gate_ref[e, pl.ds(src, win), :] += jnp.sum(jnp.where(hit, p_ref[e:e + 1, :], 0.0), axis=1, keepdims=True)
        rows = _dot(hit_ref[...], h_ref[...])
        for e in range(ne):
            dst = xin_ref.at[e, pl.ds(srcs[e], win), :]
            dst[...] = (dst[...].astype(F32) + rows[e * win:(e + 1) * win, :]).astype(BF16)
        return carry

    lax.fori_loop(0, n_pass, one_pass, 0)


def _moe_gather(starts, slot, pt, h_bf16, cap, tk, win=64):
    bsz, ne, s = slot.shape
    d = h_bf16.shape[-1]
    rowspec = pl.BlockSpec((None, ne, tk), lambda b, k, st: (b, 0, k))
    return pl.pallas_call(
        functools.partial(_gather_kernel, cap=cap, win=win),
        grid_spec=pltpu.PrefetchScalarGridSpec(
            num_scalar_prefetch=1,
            grid=(bsz, s // tk),
            in_specs=[rowspec, rowspec, pl.BlockSpec((None, tk, d), lambda b, k, st: (b, k, 0))],
            out_specs=[pl.BlockSpec((None, ne, cap, d), lambda b, k, st: (b, 0, 0, 0)),
                       pl.BlockSpec((None, ne, cap, 1), lambda b, k, st: (b, 0, 0, 0))],
            scratch_shapes=[pltpu.VMEM((ne * win, tk), BF16)]),
        out_shape=[jax.ShapeDtypeStruct((bsz, ne, cap, d), BF16),
                   jax.ShapeDtypeStruct((bsz, ne, cap, 1), F32)],
        compiler_params=_params("parallel", "arbitrary"),
        name="moe_gather",
    )(starts, slot, pt, h_bf16)


def _ffn_kernel(xin_ref, gate_ref, wg_ref, wu_ref, wd_ref, y_ref, acc_ref):
    f = pl.program_id(1)
    bsz, cap, d = xin_ref.shape
    x = xin_ref[...].reshape(bsz * cap, d)
    @pl.when(f == 0)
    def _():
        acc_ref[...] = jnp.zeros_like(acc_ref)

    hid = _silu(_dot(x, wg_ref[...].astype(BF16))) * _dot(x, wu_ref[...].astype(BF16))
    acc_ref[...] += _dot(hid.astype(BF16), wd_ref[...].astype(BF16))

    @pl.when(f == pl.num_programs(1) - 1)
    def _():
        y_ref[...] = (acc_ref[...] * gate_ref[...].reshape(bsz * cap, 1)).astype(BF16).reshape(bsz, cap, d)


def _moe_ffn(xin, gate, w_gate, w_up, w_down, layer, tf=256):
    bsz, ne, cap, d = xin.shape
    dexp = w_gate.shape[-1]
    return pl.pallas_call(
        _ffn_kernel,
        grid=(ne, dexp // tf),
        in_specs=[pl.BlockSpec((bsz, None, cap, d), lambda e, f: (0, e, 0, 0)),
                  pl.BlockSpec((bsz, None, cap, 1), lambda e, f: (0, e, 0, 0)),
                  pl.BlockSpec((None, None, d, tf), lambda e, f: (layer, e, 0, f)),
                  pl.BlockSpec((None, None, d, tf), lambda e, f: (layer, e, 0, f)),
                  pl.BlockSpec((None, None, tf, d), lambda e, f: (layer, e, f, 0))],
        out_specs=pl.BlockSpec((bsz, None, cap, d), lambda e, f: (0, e, 0, 0)),
        out_shape=jax.ShapeDtypeStruct((bsz, ne, cap, d), BF16),
        scratch_shapes=[pltpu.VMEM((bsz * cap, d), F32)],
        compiler_params=_params("parallel", "arbitrary"),
        name="moe_ffn",
    )(xin, gate, w_gate, w_up, w_down)


def _scatter_kernel(starts_ref, slot_ref, y_ref, x_ref, g2_ref, npost_ref, o_ref, ywin_ref, hit_ref, acc_ref,
                    *, cap, win):
    ne = y_ref.shape[0]
    firsts, n_pass = _expert_windows(starts_ref, ne, win)
    acc_ref[...] = jnp.zeros_like(acc_ref)

    def one_pass(w, carry):
        for e in range(ne):
            src, hit = _window_hits(slot_ref[e:e + 1, :], firsts[e], w, cap, win)
            hit_ref[e * win:(e + 1) * win, :] = hit.astype(BF16)
            ywin_ref[e * win:(e + 1) * win, :] = y_ref[e, pl.ds(src, win), :]
        acc_ref[...] += _dot_tn(hit_ref[...], ywin_ref[...])
        return carry

    lax.fori_loop(0, n_pass, one_pass, 0)
    o_ref[...] = x_ref[...] + g2_ref[...] * (_rms(acc_ref[...], NORM_EPS) * npost_ref[...])


def _moe_scatter(starts, slot, y, x, mod, npost, ts, win=128):
    bsz, ne, cap, d = y.shape
    s = x.shape[1]
    win = min(win, cap)
    tile = pl.BlockSpec((None, ts, d), lambda b, i, st: (b, i, 0))
    return pl.pallas_call(
        functools.partial(_scatter_kernel, cap=cap, win=win),
        grid_spec=pltpu.PrefetchScalarGridSpec(
            num_scalar_prefetch=1,
            grid=(bsz, s // ts),
            in_specs=[pl.BlockSpec((None, ne, ts), lambda b, i, st: (b, 0, i)),
                      pl.BlockSpec((None, ne, cap, d), lambda b, i, st: (b, 0, 0, 0)),
                      tile,
                      pl.BlockSpec((None, 1, d), lambda b, i, st: (b, 0, 5)),
                      pl.BlockSpec((1, d), lambda b, i, st: (0, 0))],
            out_specs=tile,
            scratch_shapes=[pltpu.VMEM((ne * win, d), BF16), pltpu.VMEM((ne * win, ts), BF16),
                            pltpu.VMEM((ts, d), F32)]),
        out_shape=jax.ShapeDtypeStruct((bsz, s, d), F32),
        compiler_params=_params("parallel", "arbitrary"),
        name="moe_scatter",
    )(starts, slot, y, x, mod, npost.reshape(1, d))


def _expert_choice_ffn(x1, h2, pt, mod, npost, w_gate, w_up, w_down, layer):
    s = x1.shape[1]
    ne = pt.shape[1]
    cap = EC_CAPACITY_FACTOR * s // ne
    tile = MOE_TOKEN_TILE
    slot, count = _topk_slots(pt, cap)
    starts = _tile_starts(count, tile)
    xin, gate = _moe_gather(starts, slot, pt, h2, cap, tile)
    y = _moe_ffn(xin, gate, w_gate, w_up, w_down, layer)
    return _moe_scatter(starts, slot, y, x1, mod, npost, tile)


def _rw_prep_kernel(cur_ref, prev_ref, next_ref, mu_ref, w0_ref, w2_ref, a0_ref, a2_ref, g2_ref, kkw_ref, ka_ref,
                    hsum_ref, r_ref, v_ref, kk_ref, g_ref, lw0_ref, lw1_ref, kd0_ref, kd1_ref, b0_ref, b1_ref,
                    ext_ref, *, ts):
    i = pl.program_id(1)
    n = pl.num_programs(1)
    halo = SUBLANE
    ext_ref[0:halo, :] = jnp.where(i > 0, prev_ref[...], 0.0)
    ext_ref[halo:halo + ts, :] = cur_ref[...]
    ext_ref[halo + ts:2 * halo + ts, :] = jnp.where(i < n - 1, next_ref[...], 0.0)

    def shifted(cols):
        pf = ext_ref[halo:halo + ts, cols]
        neigh = 0.5 * (ext_ref[halo - 1:halo - 1 + ts, cols] + ext_ref[halo + 1:halo + 1 + ts, cols])
        return pf + mu_ref[:, cols] * (neigh - pf)

    w = RW_WIDTH
    r = shifted(slice(0, w))
    k = shifted(slice(w, 2 * w))
    v = shifted(slice(2 * w, 3 * w))
    lora = shifted(slice(3 * w, 3 * w + 3 * LANE))
    wdec = -_softplus(-(w0_ref[...] + _dot(jnp.tanh(lora[:, 0:LANE]), w2_ref[...]))) - 0.5
    lw = -jnp.exp(wdec)
    a = _sigmoid(a0_ref[...] + _dot(lora[:, LANE:2 * LANE], a2_ref[...]))
    g_ref[...] = _dot(_sigmoid(lora[:, 2 * LANE:3 * LANE]), g2_ref[...])
    kk = k * kkw_ref[...]
    kk = kk * lax.rsqrt(_dot_mask(kk * kk, hsum_ref[...]) + 1e-6)
    r_ref[...] = r
    v_ref[...] = v
    kk_ref[...] = kk
    for d, (lw_ref, kd_ref, b_ref) in enumerate(((lw0_ref, kd0_ref, b0_ref), (lw1_ref, kd1_ref, b1_ref))):
        ad = a[:, d * w:(d + 1) * w]
        lw_ref[...] = lw[:, d * w:(d + 1) * w]
        kd_ref[...] = k * (1.0 + (ad - 1.0) * ka_ref[...])
        b_ref[...] = kk * ad


def _block_diag2(m):
    z = jnp.zeros_like(m[0])
    return jnp.concatenate([jnp.concatenate([m[0], z], 1), jnp.concatenate([z, m[1]], 1)], 0)


def _head_sum_matrix(width, head_dim):
    idx = np.arange(width) // head_dim
    return jnp.asarray(idx[:, None] == idx[None, :], F32)


def _rw_prep(p1, mu, w0, w2, a0, a2, g2, k_k, k_a, ts=256):
    bsz, s, _ = p1.shape
    cin = RW_IN
    w = RW_WIDTH
    nb8 = s // SUBLANE
    r8 = ts // SUBLANE
    full = lambda shape: pl.BlockSpec(shape, lambda b, i: (0,) * len(shape))
    out_spec = pl.BlockSpec((None, ts, w), lambda b, i: (b, i, 0))
    return pl.pallas_call(
        functools.partial(_rw_prep_kernel, ts=ts),
        grid=(bsz, s // ts),
        in_specs=[pl.BlockSpec((None, ts, cin), lambda b, i: (b, i, 0)),
                  pl.BlockSpec((None, SUBLANE, cin), lambda b, i: (b, jnp.maximum(i * r8 - 1, 0), 0)),
                  pl.BlockSpec((None, SUBLANE, cin), lambda b, i: (b, jnp.minimum((i + 1) * r8, nb8 - 1), 0)),
                  full((1, cin)), full((1, 2 * w)), full((LANE, 2 * w)), full((1, 2 * w)), full((LANE, 2 * w)),
                  full((RW_GATE_LORA, w)), full((1, w)), full((1, w)), full((w, w))],
        out_specs=[out_spec] * 10,
        out_shape=[jax.ShapeDtypeStruct((bsz, s, w), F32)] * 10,
        scratch_shapes=[pltpu.VMEM((ts + 2 * SUBLANE, cin), F32)],
        compiler_params=_params("parallel", "parallel"),
        name="rw_prep",
    )(p1, p1, p1, mu.reshape(1, cin), w0.reshape(1, 2 * w), _block_diag2(w2), a0.reshape(1, 2 * w),
      _block_diag2(a2), g2, k_k.reshape(1, w), k_a.reshape(1, w), _head_sum_matrix(w, RW_HEAD_DIM))


def _rw_direction(r_ref, v_ref, kk_ref, lw_ref, kd_ref, b_ref, o_ref, state_ref, *, sc, reverse):
    chunk = RW_CHUNK
    nck = sc // chunk
    hd = RW_HEAD_DIM
    masks = _ChunkMasks(sc, chunk, reverse)
    lw = lw_ref[...]
    lg = _mask_dot(masks.incl, lw)
    lt = _mask_dot(masks.same, lw)
    e_neg = jnp.exp(-lg)
    kkt = kk_ref[...] * jnp.exp(lg - lw)
    khat = kd_ref[...] * e_neg
    bhat = b_ref[...] * e_neg
    rt = r_ref[...] * jnp.exp(lg)
    e_rest = jnp.exp(lt - lg)
    kbar = kd_ref[...] * e_rest
    bbar = b_ref[...] * e_rest
    e_tot = jnp.exp(lt)
    order = range(nck - 1, -1, -1) if reverse else range(nck)
    steps = []
    for h in range(RW_HEADS):
        cols = slice(h * hd, (h + 1) * hd)
        v = v_ref[:, cols]
        kkt_h = kkt[:, cols]
        khat_h = khat[:, cols]
        bhat_h = bhat[:, cols]
        rt_h = rt[:, cols]
        l_a = jnp.where(masks.strict, _dot_nt(kkt_h, khat_h), 0.0)
        a_qb = jnp.where(masks.incl, _dot_nt(rt_h, bhat_h), 0.0)
        steps.append(dict(
            l_b=jnp.where(masks.strict, _dot_nt(kkt_h, bhat_h), 0.0), kkt=kkt_h, la_v=_dot(l_a, v), rt=rt_h,
            y0=_dot(jnp.where(masks.incl, _dot_nt(rt_h, khat_h), 0.0), v),
            a_qb=[a_qb[c * chunk:(c + 1) * chunk, c * chunk:(c + 1) * chunk] for c in range(nck)],
            v=v, kbar=kbar[:, cols], bbar=bbar[:, cols], gl=e_tot[:, cols],
            cols=cols, o_ref=o_ref, state_ref=state_ref, h=h, order=order))
    for st, t in zip(steps, _unit_tri_inverses([st.pop('l_b') for st in steps], masks)):
        st['wm'] = _dot(t, st.pop('kkt'))
        st['u'] = _dot(t, st.pop('la_v'))
    return steps


def _rw_recurrence(steps, chunk):
    states = [st['state_ref'][st['h']] for st in steps]
    for pos in range(len(steps[0]['order'])):
        for n, st in enumerate(steps):
            c = st['order'][pos]
            rows = slice(c * chunk, (c + 1) * chunk)
            p = _dot_nt(st['wm'][rows], states[n]) + st['u'][rows]
            st['o_ref'][rows, st['cols']] = (st['y0'][rows] + _dot_nt(st['rt'][rows], states[n])
                                             - _dot(st['a_qb'][c], p))
            states[n] = (states[n] * st['gl'][c * chunk:c * chunk + 1] + _dot_tn(st['v'][rows], st['kbar'][rows])
                         - _dot_tn(p, st['bbar'][rows]))
    for n, st in enumerate(steps):
        st['state_ref'][st['h']] = states[n]


def _rw_chunk_kernel(rf, vf, kkf, lwf, kdf, bf, rr, vr, kkr, lwr, kdr, br, of_ref, or_ref, sf_ref, sr_ref, *, sc):
    @pl.when(pl.program_id(1) == 0)
    def _():
        sf_ref[...] = jnp.zeros_like(sf_ref)
        sr_ref[...] = jnp.zeros_like(sr_ref)

    steps = (_rw_direction(rf, vf, kkf, lwf, kdf, bf, of_ref, sf_ref, sc=sc, reverse=False)
             + _rw_direction(rr, vr, kkr, lwr, kdr, br, or_ref, sr_ref, sc=sc, reverse=True))
    _rw_recurrence(steps, RW_CHUNK)


def _rw_chunked(r, v, kk, lw0, lw1, kd0, kd1, b0, b1, sc=256):
    bsz, s, w = r.shape
    n = s // sc
    fwd = pl.BlockSpec((None, sc, w), lambda b, i: (b, i, 0))
    rev = pl.BlockSpec((None, sc, w), lambda b, i: (b, n - 1 - i, 0))
    state = pltpu.VMEM((RW_HEADS, RW_HEAD_DIM, RW_HEAD_DIM), F32)
    return pl.pallas_call(
        functools.partial(_rw_chunk_kernel, sc=sc),
        grid=(bsz, n),
        in_specs=[fwd] * 6 + [rev] * 6,
        out_specs=[fwd, rev],
        out_shape=[jax.ShapeDtypeStruct((bsz, s, w), F32)] * 2,
        scratch_shapes=[state, state],
        compiler_params=_params("parallel", "arbitrary"),
        name="rw_chunked",
    )(r, v, kk, lw0, kd0, b0, r, v, kk, lw1, kd1, b1)


def _rw_post_kernel(yf_ref, yr_ref, r_ref, v_ref, g_ref, kd0_ref, kd1_ref, rk_ref, lnw_ref, lnb_ref, hsum_ref, o_ref):
    inv = 1.0 / RW_HEAD_DIM
    y = yf_ref[...] + yr_ref[...]
    mean = _dot_mask(y, hsum_ref[...]) * inv
    cen = y - mean
    var = _dot_mask(cen * cen, hsum_ref[...]) * inv
    y = cen * lax.rsqrt(var + RW_GN_EPS) * lnw_ref[...] + lnb_ref[...]
    k_bonus = 0.5 * (kd0_ref[...] + kd1_ref[...])
    y = y + _dot_mask(r_ref[...] * k_bonus * rk_ref[...], hsum_ref[...]) * v_ref[...]
    o_ref[...] = y * g_ref[...]


def _rw_post(y_f, y_r, r, v, g, kd0, kd1, r_k, ln_w, ln_b, ts=512):
    bsz, s, w = r.shape
    spec = pl.BlockSpec((None, ts, w), lambda b, i: (b, i, 0))
    row = pl.BlockSpec((1, w), lambda b, i: (0, 0))
    return pl.pallas_call(
        _rw_post_kernel,
        grid=(bsz, s // ts),
        in_specs=[spec] * 7 + [row, row, row, pl.BlockSpec((w, w), lambda b, i: (0, 0))],
        out_specs=spec,
        out_shape=jax.ShapeDtypeStruct((bsz, s, w), F32),
        compiler_params=_params("parallel", "parallel"),
        name="rw_post",
    )(y_f, y_r, r, v, g, kd0, kd1, r_k.reshape(1, w), ln_w.reshape(1, w), ln_b.reshape(1, w),
      _head_sum_matrix(w, RW_HEAD_DIM))


def _rwkv7_mixer(p1, mu, w0, w2, a0, a2, g2, k_k, k_a, r_k, ln_w, ln_b):
    r, v, kk, g, lw0, lw1, kd0, kd1, b0, b1 = _rw_prep(p1, mu, w0, w2, a0, a2, g2, k_k, k_a)
    y_f, y_r = _rw_chunked(r, v, kk, lw0, lw1, kd0, kd1, b0, b1)
    return _rw_post(y_f, y_r, r, v, g, kd0, kd1, r_k, ln_w, ln_b)


def _diff_attn_kernel(q_ref, k_ref, v_ref, lam_ref, sub_ref, o_ref, *, tq, lambda_init):
    h = pl.program_id(1)
    i = pl.program_id(2)
    s = k_ref.shape[0]
    hd = DF_HEAD_DIM
    slope = jnp.where(h == 0, 2.0 ** -2, jnp.where(h == 1, 2.0 ** -4, jnp.where(h == 2, 2.0 ** -6, 2.0 ** -8)))
    lp = lam_ref[...]
    lam = (jnp.exp(jnp.sum(lp[0:1] * lp[1:2], axis=1, keepdims=True))
           - jnp.exp(jnp.sum(lp[2:3] * lp[3:4], axis=1, keepdims=True)) + lambda_init)
    qpos = (i * tq + lax.broadcasted_iota(jnp.int32, (tq, 1), 0)).astype(F32)
    kpos = lax.broadcasted_iota(jnp.int32, (1, s), 1).astype(F32)
    bias = (slope.astype(F32) * LOG2_E) * jnp.abs(qpos - kpos)
    v = v_ref[...].astype(BF16)
    outs = []
    for m in range(2):
        q = q_ref[:, m * hd:(m + 1) * hd] * (hd ** -0.5 * LOG2_E)
        sc = _dot_nt(q, k_ref[:, m * hd:(m + 1) * hd]) - bias
        ex = jnp.exp2(sc - jnp.max(sc, axis=1, keepdims=True))
        outs.append(_dot(ex, v) / jnp.sum(ex, axis=1, keepdims=True))
    o = outs[0] - lam * outs[1]
    o_ref[...] = _rms(o, 1e-5) * sub_ref[...] * (1.0 - lambda_init)


def _diff_attention(p1, lam_params, subln_w, lambda_init, tq=256):
    bsz, s, _ = p1.shape
    hw = 2 * DF_HEAD_DIM
    q0 = RW_IN // hw
    return pl.pallas_call(
        functools.partial(_diff_attn_kernel, tq=tq, lambda_init=lambda_init),
        grid=(bsz, DF_HEADS, s // tq),
        in_specs=[pl.BlockSpec((None, tq, hw), lambda b, h, i: (b, i, q0 + h)),
                  pl.BlockSpec((None, s, hw), lambda b, h, i: (b, 0, q0 + DF_HEADS + h)),
                  pl.BlockSpec((None, s, hw), lambda b, h, i: (b, 0, q0 + 2 * DF_HEADS + h)),
                  pl.BlockSpec((4, DF_HEAD_DIM), lambda b, h, i: (0, 0)),
                  pl.BlockSpec((1, hw), lambda b, h, i: (0, 0))],
        out_specs=pl.BlockSpec((None, tq, hw), lambda b, h, i: (b, i, h)),
        out_shape=jax.ShapeDtypeStruct((bsz, s, DF_HEADS * hw), F32),
        compiler_params=_params("parallel", "parallel", "parallel"),
        name="diff_attn",
    )(p1, p1, p1, lam_params, subln_w.reshape(1, hw))


def _layer0_weights(ab_w_in):
    d = ab_w_in.shape[0]
    main = 4 * DN_WIDTH
    gates = 4 * DN_HEADS
    w = jnp.concatenate([ab_w_in[:, :main], ab_w_in[:, main + gates:], ab_w_in[:, main:main + gates],
                         jnp.zeros((d, LANE - gates), ab_w_in.dtype)], axis=1)
    return w.astype(BF16)


def kernel(x, c, ada_w, ada_b, norm_mix_pre, norm_mix_post, norm_ffn_pre, norm_ffn_post, router_w, exp_w_gate, exp_w_up, exp_w_down, ab_w_in, ab_w_out, dn_conv, dn_a_log, dn_dt_bias, dn_norm, s5_a_re, s5_a_im, s5_log_dt, s5_b_re, s5_b_im, s5_c_re, s5_c_im, s5_d, s5_glu_w, s5_glu_b, cd_w_in, cd_w_out, rw_mu, rw_w0, rw_w2, rw_a0, rw_a2, rw_g2, rw_k_k, rw_k_a, rw_r_k, rw_ln_w, rw_ln_b, df_lambda, df_subln):
    depth = ada_w.shape[0]
    mod_all = _ada_mod(c, ada_w, ada_b)
    for layer in range(depth):
        mod = mod_all[layer][:, None, :]
        i = layer // 2
        if layer % 2 == 0:
            p = _in_proj(x, norm_mix_pre[layer], mod, _layer0_weights(ab_w_in[i]))
            ya = _gated_deltanet(p, dn_conv[i], dn_a_log[i], dn_dt_bias[i], dn_norm[i])
            yb = _s5_mixer(p, s5_a_re[i], s5_a_im[i], s5_log_dt[i], s5_b_re[i], s5_b_im[i], s5_c_re[i], s5_c_im[i],
                           s5_d[i], s5_glu_w[i], s5_glu_b[i])
            w_out = ab_w_out[i]
        else:
            lambda_init = 0.8 - 0.6 * math.exp(-0.3 * layer)
            p = _in_proj(x, norm_mix_pre[layer], mod, cd_w_in[i].astype(BF16))
            ya = _rwkv7_mixer(p, rw_mu[i], rw_w0[i], rw_w2[i], rw_a0[i], rw_a2[i], rw_g2[i], rw_k_k[i], rw_k_a[i],
                              rw_r_k[i].reshape(-1), rw_ln_w[i], rw_ln_b[i])
            yb = _diff_attention(p, df_lambda[i], df_subln[i], lambda_init)
            w_out = cd_w_out[i]
        x1, h2, pt = _mix_out(ya, yb, x, w_out.astype(BF16), norm_mix_post[layer], mod, norm_ffn_pre[layer],
                              router_w[layer])
        x = _expert_choice_ffn(x1, h2, pt, mod, norm_ffn_post[layer], exp_w_gate, exp_w_up, exp_w_down, layer)
    return x
```

```python
import functools
import math

import jax
import jax.numpy as jnp
import numpy as np
from jax import lax
from jax.experimental import pallas as pl
from jax.experimental.pallas import tpu as pltpu

F32 = jnp.float32
BF16 = jnp.bfloat16
HI = lax.Precision.HIGHEST

LANE = 128
SUBLANE = 8
VMEM_LIMIT = 56 * 1024 * 1024

NORM_EPS = 1e-6
DN_HEADS = 4
DN_HEAD_DIM = 128
DN_WIDTH = 512
DN_CONV = 5
DN_CHUNK = 64
S5_WIDTH = 512
S5_GROUP = 16
S5_GROUPS = 32
S5_STATE = 64
S5_CHUNK = 16
RW_WIDTH = 512
RW_HEADS = 8
RW_HEAD_DIM = 64
RW_LORA = 64
RW_GATE_LORA = 128
RW_GN_EPS = 64e-5
RW_CHUNK = 64
RW_IN = 1920
DF_WIDTH = 512
DF_HEADS = 4
DF_HEAD_DIM = 64
N_EXPERTS = 16
EC_CAPACITY_FACTOR = 2
NEG_BIG = -1e30
LOG2_E = 1.4426950408889634
MOE_TOKEN_TILE = 256
TOPK_BISECTIONS = 152


def _params(*sem):
    return pltpu.CompilerParams(dimension_semantics=sem, vmem_limit_bytes=VMEM_LIMIT)


def _sigmoid(x):
    return 1.0 / (1.0 + jnp.exp(-x))


def _silu(x):
    return x * _sigmoid(x)


def _softplus(x):
    return jnp.maximum(x, 0.0) + jnp.log(1.0 + jnp.exp(-jnp.abs(x)))


def _rms(x, eps):
    return x * lax.rsqrt(jnp.mean(x * x, axis=-1, keepdims=True) + eps)


def _mxu_operands(a, b, precision):
    if precision is None:
        return a.astype(BF16), b.astype(BF16)
    return a, b


def _dot(a, b, precision=None):
    a, b = _mxu_operands(a, b, precision)
    return jnp.dot(a, b, preferred_element_type=F32, precision=precision)


def _dot_nt(a, b, precision=None):
    a, b = _mxu_operands(a, b, precision)
    return lax.dot_general(a, b, (((1,), (1,)), ((), ())), preferred_element_type=F32, precision=precision)


def _dot_tn(a, b, precision=None):
    a, b = _mxu_operands(a, b, precision)
    return lax.dot_general(a, b, (((0,), (0,)), ((), ())), preferred_element_type=F32, precision=precision)


def _ada_kernel(c_ref, w_ref, b_ref, o_ref):
    o_ref[...] = _dot(_silu(c_ref[...]), w_ref[...], HI) + b_ref[...]


def _ada_mod(c, ada_w, ada_b):
    depth, d, n = ada_w.shape
    bsz = c.shape[0]
    rows = -(-bsz // SUBLANE) * SUBLANE
    c_pad = jnp.zeros((rows, d), F32).at[:bsz].set(c)
    out = pl.pallas_call(
        _ada_kernel,
        grid=(depth, n // d),
        in_specs=[pl.BlockSpec((rows, d), lambda l, j: (0, 0)),
                  pl.BlockSpec((None, d, d), lambda l, j: (l, 0, j)),
                  pl.BlockSpec((None, 1, d), lambda l, j: (l, 0, j))],
        out_specs=pl.BlockSpec((None, rows, d), lambda l, j: (l, 0, j)),
        out_shape=jax.ShapeDtypeStruct((depth, rows, n), F32),
        compiler_params=_params("parallel", "parallel"),
        name="ada_mod",
    )(c_pad, ada_w, ada_b.reshape(depth, 1, n))
    return out[:, :bsz]


def _in_proj_kernel(x_ref, gain_ref, sh_ref, sc_ref, w_ref, o_ref):
    h = _rms(x_ref[...], NORM_EPS) * gain_ref[...] * (1.0 + sc_ref[...]) + sh_ref[...]
    o_ref[...] = _dot(h.astype(BF16), w_ref[...])


def _in_proj(x, gain, mod, w_bf16, tm=256):
    bsz, s, d = x.shape
    n = w_bf16.shape[1]
    return pl.pallas_call(
        _in_proj_kernel,
        grid=(bsz, s // tm),
        in_specs=[pl.BlockSpec((None, tm, d), lambda b, i: (b, i, 0)),
                  pl.BlockSpec((1, d), lambda b, i: (0, 0)),
                  pl.BlockSpec((None, 1, d), lambda b, i: (b, 0, 0)),
                  pl.BlockSpec((None, 1, d), lambda b, i: (b, 0, 1)),
                  pl.BlockSpec((d, n), lambda b, i: (0, 0))],
        out_specs=pl.BlockSpec((None, tm, n), lambda b, i: (b, i, 0)),
        out_shape=jax.ShapeDtypeStruct((bsz, s, n), F32),
        compiler_params=_params("parallel", "parallel"),
        name="in_proj",
    )(x, gain.reshape(1, d), mod, mod, w_bf16)


def _split3(x):
    hi = x.astype(BF16)
    rest = x - hi.astype(F32)
    mid = rest.astype(BF16)
    return hi, mid, (rest - mid.astype(F32)).astype(BF16)


def _mask_dot(mask, x):
    m = mask.astype(BF16)
    return sum(jnp.dot(m, part, preferred_element_type=F32) for part in _split3(x))


def _dot_mask(x, mask):
    m = mask.astype(BF16)
    return sum(jnp.dot(part, m, preferred_element_type=F32) for part in _split3(x))


def _mask_dot_nt(x, mask):
    m = mask.astype(BF16)
    return sum(lax.dot_general(part, m, (((1,), (1,)), ((), ())), preferred_element_type=F32)
               for part in _split3(x))


class _ChunkMasks:
    def __init__(self, n, chunk, reverse):
        ii = lax.broadcasted_iota(jnp.int32, (n, n), 0)
        jj = lax.broadcasted_iota(jnp.int32, (n, n), 1)
        same = (ii // chunk) == (jj // chunk)
        tri = (ii <= jj) if reverse else (ii >= jj)
        self.same = same
        self.incl = same & tri
        self.strict = self.incl & (ii != jj)
        self.eye = (ii == jj).astype(F32)
        m16 = (ii // 16) == (jj // 16)
        m32 = (ii // 32) == (jj // 32)
        self.m16 = m16.astype(F32)
        self.m32_only = (m32 & ~m16).astype(F32)
        self.m64_only = (same & ~m32).astype(F32)


def _unit_tri_inverses(lms, masks):
    ps = [-(lm * masks.m16) for lm in lms]
    ts = [masks.eye + p for p in ps]
    for _ in range(3):
        ps = [_dot(p, p) for p in ps]
        ts = [t + _dot(t, p) for t, p in zip(ts, ps)]
    for level in (masks.m32_only, masks.m64_only):
        cs = [_dot(lm * level, t) for lm, t in zip(lms, ts)]
        ts = [t - _dot(t, c) for t, c in zip(ts, cs)]
    return ts


def _dn_prep_kernel(cur_ref, prev_ref, next_ref, gates_ref, convw_ref, alog_ref, dtb_ref,
                    qkv_ref, gb_ref, ext_ref, *, ts):
    i = pl.program_id(1)
    n = pl.num_programs(1)
    halo = SUBLANE
    ext_ref[0:halo, :] = jnp.where(i > 0, prev_ref[...], 0.0)
    ext_ref[halo:halo + ts, :] = cur_ref[...]
    ext_ref[halo + ts:2 * halo + ts, :] = jnp.where(i < n - 1, next_ref[...], 0.0)
    pad = DN_CONV // 2
    for blk in range(3 * DN_HEADS):
        cols = slice(blk * DN_HEAD_DIM, (blk + 1) * DN_HEAD_DIM)
        acc = jnp.zeros((ts, DN_HEAD_DIM), F32)
        for tap in range(DN_CONV):
            acc = acc + ext_ref[halo - pad + tap:halo - pad + tap + ts, cols] * convw_ref[tap:tap + 1, cols]
        val = _silu(acc)
        if blk < 2 * DN_HEADS:
            val = val * lax.rsqrt(jnp.sum(val * val, axis=-1, keepdims=True) + 1e-6)
            if blk < DN_HEADS:
                val = val * (DN_HEAD_DIM ** -0.5)
        qkv_ref[:, cols] = val
    gin = gates_ref[...]
    g = -jnp.exp(alog_ref[...]) * _softplus(gin + dtb_ref[...])
    lane = lax.broadcasted_iota(jnp.int32, gin.shape, 1)
    gb_ref[...] = jnp.where(lane < 2 * DN_HEADS, g, _sigmoid(gin))


def _dn_prep(p0, conv_w, a_log, dt_bias, ts=512):
    bsz, s, _ = p0.shape
    c3 = 3 * DN_WIDTH
    gate_blk = (4 * DN_WIDTH + S5_WIDTH) // LANE
    nb8 = s // SUBLANE
    r8 = ts // SUBLANE
    alog = jnp.zeros((1, LANE), F32).at[0, :2 * DN_HEADS].set(a_log.reshape(-1))
    dtb = jnp.zeros((1, LANE), F32).at[0, :2 * DN_HEADS].set(dt_bias.reshape(-1))
    return pl.pallas_call(
        functools.partial(_dn_prep_kernel, ts=ts),
        grid=(bsz, s // ts),
        in_specs=[pl.BlockSpec((None, ts, c3), lambda b, i: (b, i, 0)),
                  pl.BlockSpec((None, SUBLANE, c3), lambda b, i: (b, jnp.maximum(i * r8 - 1, 0), 0)),
                  pl.BlockSpec((None, SUBLANE, c3), lambda b, i: (b, jnp.minimum((i + 1) * r8, nb8 - 1), 0)),
                  pl.BlockSpec((None, ts, LANE), lambda b, i: (b, i, gate_blk)),
                  pl.BlockSpec((DN_CONV, c3), lambda b, i: (0, 0)),
                  pl.BlockSpec((1, LANE), lambda b, i: (0, 0)),
                  pl.BlockSpec((1, LANE), lambda b, i: (0, 0))],
        out_specs=[pl.BlockSpec((None, ts, c3), lambda b, i: (b, i, 0)),
                   pl.BlockSpec((None, ts, LANE), lambda b, i: (b, i, 0))],
        out_shape=[jax.ShapeDtypeStruct((bsz, s, c3), F32),
                   jax.ShapeDtypeStruct((bsz, s, LANE), F32)],
        scratch_shapes=[pltpu.VMEM((ts + 2 * SUBLANE, c3), F32)],
        compiler_params=_params("parallel", "parallel"),
        name="dn_prep",
    )(p0, p0, p0, p0, conv_w, alog, dtb)


def _dn_direction(q_ref, k_ref, v_ref, gb_ref, gbt_ref, o_ref, state_ref, *, sc, reverse, direction):
    chunk = DN_CHUNK
    nck = sc // chunk
    masks = _ChunkMasks(sc, chunk, reverse)
    gcols = gb_ref[...]
    gc_cols = _mask_dot(masks.incl, gcols)
    gt_cols = _mask_dot(masks.same, gcols)
    gc_rows = _mask_dot_nt(gbt_ref[...], masks.incl)
    e_gc = jnp.exp(gc_cols)
    e_rest = jnp.exp(gt_cols - gc_cols)
    e_tot = jnp.exp(gt_cols)
    order = range(nck - 1, -1, -1) if reverse else range(nck)
    steps = []
    for h in range(DN_HEADS):
        gi = direction * DN_HEADS + h
        bi = 2 * DN_HEADS + gi
        cols = slice(h * DN_HEAD_DIM, (h + 1) * DN_HEAD_DIM)
        q = q_ref[:, cols]
        k = k_ref[:, cols]
        beta = gcols[:, bi:bi + 1]
        eg = e_gc[:, gi:gi + 1]
        decay = jnp.exp(jnp.where(masks.incl, gc_cols[:, gi:gi + 1] - gc_rows[gi:gi + 1, :], NEG_BIG))
        kb = k * beta
        attn = _dot_nt(q, k) * decay
        steps.append(dict(
            lm=jnp.where(masks.strict, _dot_nt(kb, k) * decay, 0.0), vb=v_ref[:, cols] * beta, kbg=kb * eg,
            attn=[attn[c * chunk:(c + 1) * chunk, c * chunk:(c + 1) * chunk] for c in range(nck)], qd=q * eg,
            kd=k * e_rest[:, gi:gi + 1], gl=e_tot[:, gi:gi + 1], cols=cols, o_ref=o_ref, state_ref=state_ref, h=h,
            order=order))
    for st, t in zip(steps, _unit_tri_inverses([st.pop('lm') for st in steps], masks)):
        st['u'] = _dot(t, st.pop('vb'))
        st['w'] = _dot(t, st.pop('kbg'))
    return steps


def _dn_recurrence(steps, chunk):
    states = [st['state_ref'][st['h']] for st in steps]
    for pos in range(len(steps[0]['order'])):
        for n, st in enumerate(steps):
            c = st['order'][pos]
            rows = slice(c * chunk, (c + 1) * chunk)
            v_new = st['u'][rows] - _dot(st['w'][rows], states[n])
            st['o_ref'][rows, st['cols']] = _dot(st['qd'][rows], states[n]) + _dot(st['attn'][c], v_new)
            states[n] = states[n] * st['gl'][c * chunk:c * chunk + 1] + _dot_tn(st['kd'][rows], v_new)
    for n, st in enumerate(steps):
        st['state_ref'][st['h']] = states[n]


def _dn_chunk_kernel(qf, kf, vf, gbf, gbtf, qr, kr, vr, gbr, gbtr, of_ref, or_ref, sf_ref, sr_ref, *, sc):
    @pl.when(pl.program_id(1) == 0)
    def _():
        sf_ref[...] = jnp.zeros_like(sf_ref)
        sr_ref[...] = jnp.zeros_like(sr_ref)

    steps = (_dn_direction(qf, kf, vf, gbf, gbtf, of_ref, sf_ref, sc=sc, reverse=False, direction=0)
             + _dn_direction(qr, kr, vr, gbr, gbtr, or_ref, sr_ref, sc=sc, reverse=True, direction=1))
    _dn_recurrence(steps, DN_CHUNK)


def _dn_chunked(qkv, gb, sc=256):
    bsz, s, _ = qkv.shape
    n = s // sc
    gbt = jnp.swapaxes(gb[..., :4 * DN_HEADS], 1, 2)
    fwd = lambda col: pl.BlockSpec((None, sc, DN_WIDTH), lambda b, i: (b, i, col))
    rev = lambda col: pl.BlockSpec((None, sc, DN_WIDTH), lambda b, i: (b, n - 1 - i, col))
    in_specs = ([fwd(0), fwd(1), fwd(2),
                 pl.BlockSpec((None, sc, LANE), lambda b, i: (b, i, 0)),
                 pl.BlockSpec((None, 4 * DN_HEADS, sc), lambda b, i: (b, 0, i))]
                + [rev(0), rev(1), rev(2),
                   pl.BlockSpec((None, sc, LANE), lambda b, i: (b, n - 1 - i, 0)),
                   pl.BlockSpec((None, 4 * DN_HEADS, sc), lambda b, i: (b, 0, n - 1 - i))])
    state = pltpu.VMEM((DN_HEADS, DN_HEAD_DIM, DN_HEAD_DIM), F32)
    return pl.pallas_call(
        functools.partial(_dn_chunk_kernel, sc=sc),
        grid=(bsz, n),
        in_specs=in_specs,
        out_specs=[pl.BlockSpec((None, sc, DN_WIDTH), lambda b, i: (b, i, 0)),
                   pl.BlockSpec((None, sc, DN_WIDTH), lambda b, i: (b, n - 1 - i, 0))],
        out_shape=[jax.ShapeDtypeStruct((bsz, s, DN_WIDTH), F32)] * 2,
        scratch_shapes=[state, state],
        compiler_params=_params("parallel", "arbitrary"),
        name="dn_chunked",
    )(qkv, qkv, qkv, gb, gbt, qkv, qkv, qkv, gb, gbt)


def _dn_post_kernel(of_ref, or_ref, z_ref, nw_ref, o_ref):
    for h in range(DN_HEADS):
        cols = slice(h * DN_HEAD_DIM, (h + 1) * DN_HEAD_DIM)
        o = of_ref[:, cols] + or_ref[:, cols]
        o_ref[:, cols] = _rms(o, NORM_EPS) * nw_ref[...] * _silu(z_ref[:, cols])


def _dn_post(o_f, o_r, p0, norm_w, ts=512):
    bsz, s, w = o_f.shape
    z_blk = 3 * DN_WIDTH // w
    spec = pl.BlockSpec((None, ts, w), lambda b, i: (b, i, 0))
    return pl.pallas_call(
        _dn_post_kernel,
        grid=(bsz, s // ts),
        in_specs=[spec, spec,
                  pl.BlockSpec((None, ts, w), lambda b, i: (b, i, z_blk)),
                  pl.BlockSpec((1, DN_HEAD_DIM), lambda b, i: (0, 0))],
        out_specs=spec,
        out_shape=jax.ShapeDtypeStruct((bsz, s, w), F32),
        compiler_params=_params("parallel", "parallel"),
        name="dn_post",
    )(o_f, o_r, p0, norm_w.reshape(1, DN_HEAD_DIM))


def _gated_deltanet(p0, conv_w, a_log, dt_bias, norm_w):
    qkv, gb = _dn_prep(p0, conv_w, a_log, dt_bias)
    o_f, o_r = _dn_chunked(qkv, gb)
    return _dn_post(o_f, o_r, p0, norm_w)


def _s5_tables(a_re, a_im, log_dt, b_re, b_im, c_re, c_im, n_chunks):
    lc, p, hh = S5_CHUNK, S5_STATE, S5_GROUP
    levels = int(math.log2(n_chunks))
    dt = jnp.exp(log_dt)[..., None]
    mag = jnp.exp(a_re * dt)
    lb_re, lb_im = mag * jnp.cos(a_im * dt), mag * jnp.sin(a_im * dt)
    den = a_re * a_re + a_im * a_im
    nr, ni = lb_re - 1.0, lb_im
    f_re = (nr * a_re + ni * a_im) / den
    f_im = (ni * a_re - nr * a_im) / den
    bb_re = f_re[..., None] * b_re - f_im[..., None] * b_im
    bb_im = f_re[..., None] * b_im + f_im[..., None] * b_re

    def power(tau):
        tau = jnp.asarray(tau, F32)[:, None]
        m = jnp.exp((a_re * dt)[:, :, None, :] * tau)
        ang = (a_im * dt)[:, :, None, :] * tau
        return m * jnp.cos(ang), m * jnp.sin(ang)

    pw_re, pw_im = power(np.arange(lc + 1))
    bbt_re, bbt_im = jnp.swapaxes(bb_re, 2, 3), jnp.swapaxes(bb_im, 2, 3)
    lbb_re = pw_re[:, :, :, None, :] * bbt_re[:, :, None, :, :] - pw_im[:, :, :, None, :] * bbt_im[:, :, None, :, :]
    lbb_im = pw_re[:, :, :, None, :] * bbt_im[:, :, None, :, :] + pw_im[:, :, :, None, :] * bbt_re[:, :, None, :, :]
    kt = (jnp.einsum('dgop,dgtip->dgito', c_re, lbb_re[:, :, :lc], precision=HI)
          - jnp.einsum('dgop,dgtip->dgito', c_im, lbb_im[:, :, :lc], precision=HI))
    strips = jnp.stack([kt[0], kt[1][:, :, ::-1, :]]).reshape(2, -1, hh, lc * hh)
    bmats, cmats = [], []
    for d in range(2):
        e_in = (lc - 1 - np.arange(lc)) if d == 0 else np.arange(lc)
        bmats.append(jnp.concatenate([lbb_re[d][:, e_in], lbb_im[d][:, e_in]], axis=-1).reshape(-1, lc * hh, 2 * p))
        e_out = (np.arange(lc) + 1) if d == 0 else (lc - np.arange(lc))
        sel_re, sel_im = pw_re[d][:, e_out, None, :], pw_im[d][:, e_out, None, :]
        m_re = c_re[d][:, None] * sel_re - c_im[d][:, None] * sel_im
        m_im = c_re[d][:, None] * sel_im + c_im[d][:, None] * sel_re
        cmats.append(jnp.concatenate([m_re, -m_im], axis=-1).reshape(-1, lc * hh, 2 * p))
    sq_re, sq_im = power(lc * 2.0 ** np.arange(levels))
    scan_a = jnp.concatenate([sq_re, sq_re], axis=-1)
    scan_b = jnp.concatenate([-sq_im, sq_im], axis=-1)
    return strips, jnp.stack(bmats), jnp.stack(cmats), scan_a, scan_b


def _s5_kernel(u_ref, strip_ref, bmat_ref, cmat_ref, sa_ref, sb_ref, y_ref, toep_ref, *, n_chunks):
    u = u_ref[...]
    n = u.shape[0]
    p2 = 2 * S5_STATE
    lc, hh = S5_CHUNK, S5_GROUP
    levels = sa_ref.shape[1]
    cidx = lax.broadcasted_iota(jnp.int32, (n, p2), 0) & (n_chunks - 1)
    lane = lax.broadcasted_iota(jnp.int32, (hh, lc * hh), 1)
    y = None
    for d in range(2):
        strip = strip_ref[d]
        for ti in range(lc):
            if d == 0:
                blk = jnp.where(lane >= ti * hh, pltpu.roll(strip, ti * hh, 1), 0.0)
            else:
                blk = jnp.where(lane < (ti + 1) * hh, pltpu.roll(strip, ((ti + 1) * hh) % (lc * hh), 1), 0.0)
            toep_ref[ti * hh:(ti + 1) * hh, :] = blk
        x = _dot(u, bmat_ref[d])
        for lv in range(levels):
            sh = 1 << lv
            if d == 0:
                xs = jnp.where(cidx >= sh, pltpu.roll(x, sh, 0), 0.0)
            else:
                xs = jnp.where(cidx < n_chunks - sh, pltpu.roll(x, n - sh, 0), 0.0)
            x = x + sa_ref[d, lv:lv + 1, :] * xs + sb_ref[d, lv:lv + 1, :] * pltpu.roll(xs, S5_STATE, 1)
        if d == 0:
            x_in = jnp.where(cidx >= 1, pltpu.roll(x, 1, 0), 0.0)
        else:
            x_in = jnp.where(cidx < n_chunks - 1, pltpu.roll(x, n - 1, 0), 0.0)
        yd = _dot(u, toep_ref[...]) + _dot_nt(x_in, cmat_ref[d])
        y = yd if y is None else y + yd
    y_ref[...] = y


def _s5_pack_kernel(*refs):
    *u_refs, o_ref = refs
    lc, hh = S5_CHUNK, S5_GROUP
    per_slab = LANE // hh
    nch = o_ref.shape[0]
    for t in range(lc):
        for k, u_ref in enumerate(u_refs):
            piece = u_ref[pl.ds(t, nch, stride=lc), :]
            for j in range(per_slab):
                dst = (k * per_slab + j) * lc * hh + t * hh
                o_ref[:, dst:dst + hh] = piece[:, j * hh:(j + 1) * hh]


def _s5_pack(p0, ts=512):
    bsz, s, _ = p0.shape
    lc, g, hh = S5_CHUNK, S5_GROUPS, S5_GROUP
    slab0 = 4 * DN_WIDTH // LANE
    nt = s // ts
    slab = lambda k: pl.BlockSpec((None, ts, LANE), lambda b, i: (b, i, slab0 + k))
    n_slabs = S5_WIDTH // LANE
    return pl.pallas_call(
        _s5_pack_kernel,
        grid=(bsz, nt),
        in_specs=[slab(k) for k in range(n_slabs)],
        out_specs=pl.BlockSpec((ts // lc, g * lc * hh), lambda b, i: (b * nt + i, 0)),
        out_shape=jax.ShapeDtypeStruct((bsz * s // lc, g * lc * hh), F32),
        compiler_params=_params("parallel", "parallel"),
        name="s5_pack",
    )(*([p0] * n_slabs))


def _s5_scan(uc, tables, bsz):
    lc, g, hh = S5_CHUNK, S5_GROUPS, S5_GROUP
    nc = uc.shape[0] // bsz
    strips, bmat, cmat, sa, sb = tables
    levels = sa.shape[2]
    return pl.pallas_call(
        functools.partial(_s5_kernel, n_chunks=nc),
        grid=(g,),
        in_specs=[pl.BlockSpec((bsz * nc, lc * hh), lambda i: (0, i)),
                  pl.BlockSpec((2, None, hh, lc * hh), lambda i: (0, i, 0, 0)),
                  pl.BlockSpec((2, None, lc * hh, 2 * S5_STATE), lambda i: (0, i, 0, 0)),
                  pl.BlockSpec((2, None, lc * hh, 2 * S5_STATE), lambda i: (0, i, 0, 0)),
                  pl.BlockSpec((2, None, levels, 2 * S5_STATE), lambda i: (0, i, 0, 0)),
                  pl.BlockSpec((2, None, levels, 2 * S5_STATE), lambda i: (0, i, 0, 0))],
        out_specs=pl.BlockSpec((bsz * nc, lc * hh), lambda i: (0, i)),
        out_shape=jax.ShapeDtypeStruct((bsz * nc, g * lc * hh), F32),
        scratch_shapes=[pltpu.VMEM((lc * hh, lc * hh), F32)],
        compiler_params=_params("parallel"),
        name="s5_scan",
    )(uc, strips, bmat, cmat, sa, sb)


def _gelu_tanh(x):
    return 0.5 * x * (1.0 + jnp.tanh(math.sqrt(2.0 / math.pi) * (x + 0.044715 * (x * x * x))))


def _s5_post_kernel(u_ref, yc_ref, d_ref, w_ref, b_ref, o_ref, step_ref, y_ref):
    lc, hh = S5_CHUNK, S5_GROUP
    nch = yc_ref.shape[0]
    n_slabs = y_ref.shape[0]
    for t in range(lc):
        for g in range(S5_GROUPS):
            step_ref[:, g * hh:(g + 1) * hh] = yc_ref[:, g * lc * hh + t * hh:g * lc * hh + (t + 1) * hh]
        for k in range(n_slabs):
            y_ref[k, pl.ds(t, nch, stride=lc), :] = step_ref[:, k * LANE:(k + 1) * LANE]
    y_scan = jnp.concatenate([y_ref[k] for k in range(n_slabs)], axis=1)
    y = _gelu_tanh(d_ref[...] * u_ref[...] + y_scan)
    o_ref[...] = y * _sigmoid(_dot(y, w_ref[...]) + b_ref[...])


def _s5_post(p0, yc, d_skip, glu_w, glu_b, ts=512):
    bsz, s, _ = p0.shape
    w = S5_WIDTH
    lc = S5_CHUNK
    nt = s // ts
    u_blk = 4 * DN_WIDTH // w
    spec = pl.BlockSpec((None, ts, w), lambda b, i: (b, i, 0))
    row = pl.BlockSpec((1, w), lambda b, i: (0, 0))
    return pl.pallas_call(
        _s5_post_kernel,
        grid=(bsz, nt),
        in_specs=[pl.BlockSpec((None, ts, w), lambda b, i: (b, i, u_blk)),
                  pl.BlockSpec((ts // lc, yc.shape[1]), lambda b, i: (b * nt + i, 0)), row,
                  pl.BlockSpec((w, w), lambda b, i: (0, 0)), row],
        out_specs=spec,
        out_shape=jax.ShapeDtypeStruct((bsz, s, w), F32),
        scratch_shapes=[pltpu.VMEM((ts // lc, w), F32), pltpu.VMEM((w // LANE, ts, LANE), F32)],
        compiler_params=_params("parallel", "parallel"),
        name="s5_post",
    )(p0, yc, d_skip.reshape(1, w), glu_w, glu_b.reshape(1, w))


def _s5_mixer(p0, a_re, a_im, log_dt, b_re, b_im, c_re, c_im, d_skip, glu_w, glu_b):
    bsz, s, _ = p0.shape
    tables = _s5_tables(a_re, a_im, log_dt, b_re, b_im, c_re, c_im, s // S5_CHUNK)
    return _s5_post(p0, _s5_scan(_s5_pack(p0), tables, bsz), d_skip, glu_w, glu_b)


def _mix_out_kernel(ya_ref, yb_ref, x_ref, w_ref, npost_ref, g1_ref, npre_ref, sh2_ref, sc2_ref, rwt_ref,
                    x1_ref, h2_ref, pt_ref):
    half = ya_ref.shape[-1]
    y = _dot(ya_ref[...].astype(BF16), w_ref[0:half, :]) + _dot(yb_ref[...].astype(BF16), w_ref[half:2 * half, :])
    x1 = x_ref[...] + g1_ref[...] * (_rms(y, NORM_EPS) * npost_ref[...])
    x1_ref[...] = x1
    h2 = _rms(x1, NORM_EPS) * npre_ref[...] * (1.0 + sc2_ref[...]) + sh2_ref[...]
    h2_ref[...] = h2.astype(BF16)
    logits = _dot_nt(rwt_ref[...], h2, HI)
    ex = jnp.exp(logits - jnp.max(logits, axis=0, keepdims=True))
    pt_ref[...] = ex / jnp.sum(ex, axis=0, keepdims=True)


def _mix_out(ya, yb, x, w_out_bf16, npost, mod, npre, router_w, tm=256):
    bsz, s, d = x.shape
    half = ya.shape[-1]
    ne = router_w.shape[-1]
    row = lambda a: a.reshape(1, d)
    vec = pl.BlockSpec((1, d), lambda b, i: (0, 0))
    modblk = lambda k: pl.BlockSpec((None, 1, d), lambda b, i: (b, 0, k))
    tile = pl.BlockSpec((None, tm, d), lambda b, i: (b, i, 0))
    halfspec = pl.BlockSpec((None, tm, half), lambda b, i: (b, i, 0))
    return pl.pallas_call(
        _mix_out_kernel,
        grid=(bsz, s // tm),
        in_specs=[halfspec, halfspec, tile,
                  pl.BlockSpec((d, d), lambda b, i: (0, 0)), vec, modblk(2), vec, modblk(3), modblk(4),
                  pl.BlockSpec((ne, d), lambda b, i: (0, 0))],
        out_specs=[tile, tile, pl.BlockSpec((None, ne, tm), lambda b, i: (b, 0, i))],
        out_shape=[jax.ShapeDtypeStruct((bsz, s, d), F32),
                   jax.ShapeDtypeStruct((bsz, s, d), BF16),
                   jax.ShapeDtypeStruct((bsz, ne, s), F32)],
        compiler_params=_params("parallel", "parallel"),
        name="mix_out",
    )(ya, yb, x, w_out_bf16, row(npost), mod, row(npre), mod, mod, router_w.T)


def _prefix_lanes(x):
    n = x.shape[1]
    lane = lax.broadcasted_iota(jnp.int32, x.shape, 1)
    sh = 1
    while sh < n:
        x = x + jnp.where(lane >= sh, pltpu.roll(x, sh, 1), 0.0)
        sh *= 2
    return x


def _topk_kernel(p_ref, slot_ref, count_ref, *, cap):
    p = p_ref[...]
    rows = p.shape[0]

    def halve(_, bounds):
        lo, hi = bounds
        mid = 0.5 * (lo + hi)
        ok = jnp.sum((p >= mid).astype(F32), axis=1, keepdims=True) >= cap
        return jnp.where(ok, mid, lo), jnp.where(ok, hi, mid)

    lo, _ = lax.fori_loop(0, TOPK_BISECTIONS, halve, (jnp.zeros((rows, 1), F32), jnp.full((rows, 1), 2.0, F32)))
    gt = p > lo
    eq = p == lo
    need = cap - jnp.sum(gt.astype(F32), axis=1, keepdims=True)
    sel = gt | (eq & (_prefix_lanes(eq.astype(F32)) <= need))
    count = _prefix_lanes(sel.astype(F32))
    slot_ref[...] = jnp.where(sel, count - 1.0, -1.0).astype(jnp.int32)
    count_ref[...] = count.astype(jnp.int32)


def _topk_slots(pt, cap):
    bsz, ne, s = pt.shape
    spec = pl.BlockSpec((bsz * ne, s), lambda i: (0, 0))
    slot, count = pl.pallas_call(
        functools.partial(_topk_kernel, cap=cap),
        grid=(1,), in_specs=[spec], out_specs=[spec, spec],
        out_shape=[jax.ShapeDtypeStruct((bsz * ne, s), jnp.int32)] * 2,
        compiler_params=_params("arbitrary"),
        name="topk_slots",
    )(pt.reshape(bsz * ne, s))
    return slot.reshape(bsz, ne, s), count.reshape(bsz, ne, s)


def _tile_starts(count, tile):
    bsz, ne, _ = count.shape
    ends = count[:, :, tile - 1::tile]
    return jnp.concatenate([jnp.zeros((bsz, ne, 1), jnp.int32), ends], axis=2).reshape(-1)


BF16_ROWS = 2 * SUBLANE


def _expert_windows(starts_ref, ne, win):
    n_tiles = pl.num_programs(1)
    shift = win.bit_length() - 1
    firsts, n_pass = [], 0
    for e in range(ne):
        base = (pl.program_id(0) * ne + e) * (n_tiles + 1) + pl.program_id(1)
        first = (starts_ref[base] // BF16_ROWS) * BF16_ROWS
        firsts.append(first)
        n_pass = jnp.maximum(n_pass, (starts_ref[base + 1] - first + win - 1) >> shift)
    return firsts, n_pass


def _window_hits(slot_row, first, w, cap, win):
    row0 = first + w * win
    src = pl.multiple_of(jnp.minimum(row0, cap - win), BF16_ROWS)
    rows = src + lax.broadcasted_iota(jnp.int32, (win, slot_row.shape[1]), 0)
    return src, (rows == slot_row) & (rows >= row0)


def _gather_kernel(starts_ref, slot_ref, p_ref, h_ref, xin_ref, gate_ref, hit_ref, *, cap, win):
    ne = slot_ref.shape[0]

    @pl.when(pl.program_id(1) == 0)
    def _():
        xin_ref[...] = jnp.zeros_like(xin_ref)
        gate_ref[...] = jnp.zeros_like(gate_ref)

    firsts, n_pass = _expert_windows(starts_ref, ne, win)

    def one_pass(w, carry):
        srcs = []
        for e in range(ne):
            src, hit = _window_hits(slot_ref[e:e + 1, :], firsts[e], w, cap, win)
            srcs.append(src)
            hit_ref[e * win:(e + 1) * win, :] = hit.astype(BF16)
            gate_ref[e, pl.ds(src, win), :] += jnp.sum(jnp.where(hit, p_ref[e:e + 1, :], 0.0), axis=1, keepdims=True)
        rows = _dot(hit_ref[...], h_ref[...])
        for e in range(ne):
            dst = xin_ref.at[e, pl.ds(srcs[e], win), :]
            dst[...] = (dst[...].astype(F32) + rows[e * win:(e + 1) * win, :]).astype(BF16)
        return carry

    lax.fori_loop(0, n_pass, one_pass, 0)


def _moe_gather(starts, slot, pt, h_bf16, cap, tk, win=64):
    bsz, ne, s = slot.shape
    d = h_bf16.shape[-1]
    rowspec = pl.BlockSpec((None, ne, tk), lambda b, k, st: (b, 0, k))
    return pl.pallas_call(
        functools.partial(_gather_kernel, cap=cap, win=win),
        grid_spec=pltpu.PrefetchScalarGridSpec(
            num_scalar_prefetch=1,
            grid=(bsz, s // tk),
            in_specs=[rowspec, rowspec, pl.BlockSpec((None, tk, d), lambda b, k, st: (b, k, 0))],
            out_specs=[pl.BlockSpec((None, ne, cap, d), lambda b, k, st: (b, 0, 0, 0)),
                       pl.BlockSpec((None, ne, cap, 1), lambda b, k, st: (b, 0, 0, 0))],
            scratch_shapes=[pltpu.VMEM((ne * win, tk), BF16)]),
        out_shape=[jax.ShapeDtypeStruct((bsz, ne, cap, d), BF16),
                   jax.ShapeDtypeStruct((bsz, ne, cap, 1), F32)],
        compiler_params=_params("parallel", "arbitrary"),
        name="moe_gather",
    )(starts, slot, pt, h_bf16)


def _ffn_kernel(xin_ref, gate_ref, wg_ref, wu_ref, wd_ref, y_ref, acc_ref):
    f = pl.program_id(1)
    bsz, cap, d = xin_ref.shape
    x = xin_ref[...].reshape(bsz * cap, d)
    @pl.when(f == 0)
    def _():
        acc_ref[...] = jnp.zeros_like(acc_ref)

    hid = _silu(_dot(x, wg_ref[...].astype(BF16))) * _dot(x, wu_ref[...].astype(BF16))
    acc_ref[...] += _dot(hid.astype(BF16), wd_ref[...].astype(BF16))

    @pl.when(f == pl.num_programs(1) - 1)
    def _():
        y_ref[...] = (acc_ref[...] * gate_ref[...].reshape(bsz * cap, 1)).astype(BF16).reshape(bsz, cap, d)


def _moe_ffn(xin, gate, w_gate, w_up, w_down, layer, tf=256):
    bsz, ne, cap, d = xin.shape
    dexp = w_gate.shape[-1]
    return pl.pallas_call(
        _ffn_kernel,
        grid=(ne, dexp // tf),
        in_specs=[pl.BlockSpec((bsz, None, cap, d), lambda e, f: (0, e, 0, 0)),
                  pl.BlockSpec((bsz, None, cap, 1), lambda e, f: (0, e, 0, 0)),
                  pl.BlockSpec((None, None, d, tf), lambda e, f: (layer, e, 0, f)),
                  pl.BlockSpec((None, None, d, tf), lambda e, f: (layer, e, 0, f)),
                  pl.BlockSpec((None, None, tf, d), lambda e, f: (layer, e, f, 0))],
        out_specs=pl.BlockSpec((bsz, None, cap, d), lambda e, f: (0, e, 0, 0)),
        out_shape=jax.ShapeDtypeStruct((bsz, ne, cap, d), BF16),
        scratch_shapes=[pltpu.VMEM((bsz * cap, d), F32)],
        compiler_params=_params("parallel", "arbitrary"),
        name="moe_ffn",
    )(xin, gate, w_gate, w_up, w_down)


def _scatter_kernel(starts_ref, slot_ref, y_ref, x_ref, g2_ref, npost_ref, o_ref, ywin_ref, hit_ref, acc_ref,
                    *, cap, win):
    ne = y_ref.shape[0]
    firsts, n_pass = _expert_windows(starts_ref, ne, win)
    acc_ref[...] = jnp.zeros_like(acc_ref)

    def one_pass(w, carry):
        for e in range(ne):
            src, hit = _window_hits(slot_ref[e:e + 1, :], firsts[e], w, cap, win)
            hit_ref[e * win:(e + 1) * win, :] = hit.astype(BF16)
            ywin_ref[e * win:(e + 1) * win, :] = y_ref[e, pl.ds(src, win), :]
        acc_ref[...] += _dot_tn(hit_ref[...], ywin_ref[...])
        return carry

    lax.fori_loop(0, n_pass, one_pass, 0)
    o_ref[...] = x_ref[...] + g2_ref[...] * (_rms(acc_ref[...], NORM_EPS) * npost_ref[...])


def _moe_scatter(starts, slot, y, x, mod, npost, ts, win=128):
    bsz, ne, cap, d = y.shape
    s = x.shape[1]
    win = min(win, cap)
    tile = pl.BlockSpec((None, ts, d), lambda b, i, st: (b, i, 0))
    return pl.pallas_call(
        functools.partial(_scatter_kernel, cap=cap, win=win),
        grid_spec=pltpu.PrefetchScalarGridSpec(
            num_scalar_prefetch=1,
            grid=(bsz, s // ts),
            in_specs=[pl.BlockSpec((None, ne, ts), lambda b, i, st: (b, 0, i)),
                      pl.BlockSpec((None, ne, cap, d), lambda b, i, st: (b, 0, 0, 0)),
                      tile,
                      pl.BlockSpec((None, 1, d), lambda b, i, st: (b, 0, 5)),
                      pl.BlockSpec((1, d), lambda b, i, st: (0, 0))],
            out_specs=tile,
            scratch_shapes=[pltpu.VMEM((ne * win, d), BF16), pltpu.VMEM((ne * win, ts), BF16),
                            pltpu.VMEM((ts, d), F32)]),
        out_shape=jax.ShapeDtypeStruct((bsz, s, d), F32),
        compiler_params=_params("parallel", "arbitrary"),
        name="moe_scatter",
    )(starts, slot, y, x, mod, npost.reshape(1, d))


def _expert_choice_ffn(x1, h2, pt, mod, npost, w_gate, w_up, w_down, layer):
    s = x1.shape[1]
    ne = pt.shape[1]
    cap = EC_CAPACITY_FACTOR * s // ne
    tile = MOE_TOKEN_TILE
    slot, count = _topk_slots(pt, cap)
    starts = _tile_starts(count, tile)
    xin, gate = _moe_gather(starts, slot, pt, h2, cap, tile)
    y = _moe_ffn(xin, gate, w_gate, w_up, w_down, layer)
    return _moe_scatter(starts, slot, y, x1, mod, npost, tile)


def _rw_prep_kernel(cur_ref, prev_ref, next_ref, mu_ref, w0_ref, w2_ref, a0_ref, a2_ref, g2_ref, kkw_ref, ka_ref,
                    hsum_ref, r_ref, v_ref, kk_ref, g_ref, lw0_ref, lw1_ref, kd0_ref, kd1_ref, b0_ref, b1_ref,
                    ext_ref, *, ts):
    i = pl.program_id(1)
    n = pl.num_programs(1)
    halo = SUBLANE
    ext_ref[0:halo, :] = jnp.where(i > 0, prev_ref[...], 0.0)
    ext_ref[halo:halo + ts, :] = cur_ref[...]
    ext_ref[halo + ts:2 * halo + ts, :] = jnp.where(i < n - 1, next_ref[...], 0.0)

    def shifted(cols):
        pf = ext_ref[halo:halo + ts, cols]
        neigh = 0.5 * (ext_ref[halo - 1:halo - 1 + ts, cols] + ext_ref[halo + 1:halo + 1 + ts, cols])
        return pf + mu_ref[:, cols] * (neigh - pf)

    w = RW_WIDTH
    r = shifted(slice(0, w))
    k = shifted(slice(w, 2 * w))
    v = shifted(slice(2 * w, 3 * w))
    lora = shifted(slice(3 * w, 3 * w + 3 * LANE))
    wdec = -_softplus(-(w0_ref[...] + _dot(jnp.tanh(lora[:, 0:LANE]), w2_ref[...]))) - 0.5
    lw = -jnp.exp(wdec)
    a = _sigmoid(a0_ref[...] + _dot(lora[:, LANE:2 * LANE], a2_ref[...]))
    g_ref[...] = _dot(_sigmoid(lora[:, 2 * LANE:3 * LANE]), g2_ref[...])
    kk = k * kkw_ref[...]
    kk = kk * lax.rsqrt(_dot_mask(kk * kk, hsum_ref[...]) + 1e-6)
    r_ref[...] = r
    v_ref[...] = v
    kk_ref[...] = kk
    for d, (lw_ref, kd_ref, b_ref) in enumerate(((lw0_ref, kd0_ref, b0_ref), (lw1_ref, kd1_ref, b1_ref))):
        ad = a[:, d * w:(d + 1) * w]
        lw_ref[...] = lw[:, d * w:(d + 1) * w]
        kd_ref[...] = k * (1.0 + (ad - 1.0) * ka_ref[...])
        b_ref[...] = kk * ad


def _block_diag2(m):
    z = jnp.zeros_like(m[0])
    return jnp.concatenate([jnp.concatenate([m[0], z], 1), jnp.concatenate([z, m[1]], 1)], 0)


def _head_sum_matrix(width, head_dim):
    idx = np.arange(width) // head_dim
    return jnp.asarray(idx[:, None] == idx[None, :], F32)


def _rw_prep(p1, mu, w0, w2, a0, a2, g2, k_k, k_a, ts=256):
    bsz, s, _ = p1.shape
    cin = RW_IN
    w = RW_WIDTH
    nb8 = s // SUBLANE
    r8 = ts // SUBLANE
    full = lambda shape: pl.BlockSpec(shape, lambda b, i: (0,) * len(shape))
    out_spec = pl.BlockSpec((None, ts, w), lambda b, i: (b, i, 0))
    return pl.pallas_call(
        functools.partial(_rw_prep_kernel, ts=ts),
        grid=(bsz, s // ts),
        in_specs=[pl.BlockSpec((None, ts, cin), lambda b, i: (b, i, 0)),
                  pl.BlockSpec((None, SUBLANE, cin), lambda b, i: (b, jnp.maximum(i * r8 - 1, 0), 0)),
                  pl.BlockSpec((None, SUBLANE, cin), lambda b, i: (b, jnp.minimum((i + 1) * r8, nb8 - 1), 0)),
                  full((1, cin)), full((1, 2 * w)), full((LANE, 2 * w)), full((1, 2 * w)), full((LANE, 2 * w)),
                  full((RW_GATE_LORA, w)), full((1, w)), full((1, w)), full((w, w))],
        out_specs=[out_spec] * 10,
        out_shape=[jax.ShapeDtypeStruct((bsz, s, w), F32)] * 10,
        scratch_shapes=[pltpu.VMEM((ts + 2 * SUBLANE, cin), F32)],
        compiler_params=_params("parallel", "parallel"),
        name="rw_prep",
    )(p1, p1, p1, mu.reshape(1, cin), w0.reshape(1, 2 * w), _block_diag2(w2), a0.reshape(1, 2 * w),
      _block_diag2(a2), g2, k_k.reshape(1, w), k_a.reshape(1, w), _head_sum_matrix(w, RW_HEAD_DIM))


def _rw_direction(r_ref, v_ref, kk_ref, lw_ref, kd_ref, b_ref, o_ref, state_ref, *, sc, reverse):
    chunk = RW_CHUNK
    nck = sc // chunk
    hd = RW_HEAD_DIM
    masks = _ChunkMasks(sc, chunk, reverse)
    lw = lw_ref[...]
    lg = _mask_dot(masks.incl, lw)
    lt = _mask_dot(masks.same, lw)
    e_neg = jnp.exp(-lg)
    kkt = kk_ref[...] * jnp.exp(lg - lw)
    khat = kd_ref[...] * e_neg
    bhat = b_ref[...] * e_neg
    rt = r_ref[...] * jnp.exp(lg)
    e_rest = jnp.exp(lt - lg)
    kbar = kd_ref[...] * e_rest
    bbar = b_ref[...] * e_rest
    e_tot = jnp.exp(lt)
    order = range(nck - 1, -1, -1) if reverse else range(nck)
    steps = []
    for h in range(RW_HEADS):
        cols = slice(h * hd, (h + 1) * hd)
        v = v_ref[:, cols]
        kkt_h = kkt[:, cols]
        khat_h = khat[:, cols]
        bhat_h = bhat[:, cols]
        rt_h = rt[:, cols]
        l_a = jnp.where(masks.strict, _dot_nt(kkt_h, khat_h), 0.0)
        a_qb = jnp.where(masks.incl, _dot_nt(rt_h, bhat_h), 0.0)
        steps.append(dict(
            l_b=jnp.where(masks.strict, _dot_nt(kkt_h, bhat_h), 0.0), kkt=kkt_h, la_v=_dot(l_a, v), rt=rt_h,
            y0=_dot(jnp.where(masks.incl, _dot_nt(rt_h, khat_h), 0.0), v),
            a_qb=[a_qb[c * chunk:(c + 1) * chunk, c * chunk:(c + 1) * chunk] for c in range(nck)],
            v=v, kbar=kbar[:, cols], bbar=bbar[:, cols], gl=e_tot[:, cols],
            cols=cols, o_ref=o_ref, state_ref=state_ref, h=h, order=order))
    for st, t in zip(steps, _unit_tri_inverses([st.pop('l_b') for st in steps], masks)):
        st['wm'] = _dot(t, st.pop('kkt'))
        st['u'] = _dot(t, st.pop('la_v'))
    return steps


def _rw_recurrence(steps, chunk):
    states = [st['state_ref'][st['h']] for st in steps]
    for pos in range(len(steps[0]['order'])):
        for n, st in enumerate(steps):
            c = st['order'][pos]
            rows = slice(c * chunk, (c + 1) * chunk)
            p = _dot_nt(st['wm'][rows], states[n]) + st['u'][rows]
            st['o_ref'][rows, st['cols']] = (st['y0'][rows] + _dot_nt(st['rt'][rows], states[n])
                                             - _dot(st['a_qb'][c], p))
            states[n] = (states[n] * st['gl'][c * chunk:c * chunk + 1] + _dot_tn(st['v'][rows], st['kbar'][rows])
                         - _dot_tn(p, st['bbar'][rows]))
    for n, st in enumerate(steps):
        st['state_ref'][st['h']] = states[n]


def _rw_chunk_kernel(rf, vf, kkf, lwf, kdf, bf, rr, vr, kkr, lwr, kdr, br, of_ref, or_ref, sf_ref, sr_ref, *, sc):
    @pl.when(pl.program_id(1) == 0)
    def _():
        sf_ref[...] = jnp.zeros_like(sf_ref)
        sr_ref[...] = jnp.zeros_like(sr_ref)

    steps = (_rw_direction(rf, vf, kkf, lwf, kdf, bf, of_ref, sf_ref, sc=sc, reverse=False)
             + _rw_direction(rr, vr, kkr, lwr, kdr, br, or_ref, sr_ref, sc=sc, reverse=True))
    _rw_recurrence(steps, RW_CHUNK)


def _rw_chunked(r, v, kk, lw0, lw1, kd0, kd1, b0, b1, sc=256):
    bsz, s, w = r.shape
    n = s // sc
    fwd = pl.BlockSpec((None, sc, w), lambda b, i: (b, i, 0))
    rev = pl.BlockSpec((None, sc, w), lambda b, i: (b, n - 1 - i, 0))
    state = pltpu.VMEM((RW_HEADS, RW_HEAD_DIM, RW_HEAD_DIM), F32)
    return pl.pallas_call(
        functools.partial(_rw_chunk_kernel, sc=sc),
        grid=(bsz, n),
        in_specs=[fwd] * 6 + [rev] * 6,
        out_specs=[fwd, rev],
        out_shape=[jax.ShapeDtypeStruct((bsz, s, w), F32)] * 2,
        scratch_shapes=[state, state],
        compiler_params=_params("parallel", "arbitrary"),
        name="rw_chunked",
    )(r, v, kk, lw0, kd0, b0, r, v, kk, lw1, kd1, b1)


def _rw_post_kernel(yf_ref, yr_ref, r_ref, v_ref, g_ref, kd0_ref, kd1_ref, rk_ref, lnw_ref, lnb_ref, hsum_ref, o_ref):
    inv = 1.0 / RW_HEAD_DIM
    y = yf_ref[...] + yr_ref[...]
    mean = _dot_mask(y, hsum_ref[...]) * inv
    cen = y - mean
    var = _dot_mask(cen * cen, hsum_ref[...]) * inv
    y = cen * lax.rsqrt(var + RW_GN_EPS) * lnw_ref[...] + lnb_ref[...]
    k_bonus = 0.5 * (kd0_ref[...] + kd1_ref[...])
    y = y + _dot_mask(r_ref[...] * k_bonus * rk_ref[...], hsum_ref[...]) * v_ref[...]
    o_ref[...] = y * g_ref[...]


def _rw_post(y_f, y_r, r, v, g, kd0, kd1, r_k, ln_w, ln_b, ts=512):
    bsz, s, w = r.shape
    spec = pl.BlockSpec((None, ts, w), lambda b, i: (b, i, 0))
    row = pl.BlockSpec((1, w), lambda b, i: (0, 0))
    return pl.pallas_call(
        _rw_post_kernel,
        grid=(bsz, s // ts),
        in_specs=[spec] * 7 + [row, row, row, pl.BlockSpec((w, w), lambda b, i: (0, 0))],
        out_specs=spec,
        out_shape=jax.ShapeDtypeStruct((bsz, s, w), F32),
        compiler_params=_params("parallel", "parallel"),
        name="rw_post",
    )(y_f, y_r, r, v, g, kd0, kd1, r_k.reshape(1, w), ln_w.reshape(1, w), ln_b.reshape(1, w),
      _head_sum_matrix(w, RW_HEAD_DIM))


def _rwkv7_mixer(p1, mu, w0, w2, a0, a2, g2, k_k, k_a, r_k, ln_w, ln_b):
    r, v, kk, g, lw0, lw1, kd0, kd1, b0, b1 = _rw_prep(p1, mu, w0, w2, a0, a2, g2, k_k, k_a)
    y_f, y_r = _rw_chunked(r, v, kk, lw0, lw1, kd0, kd1, b0, b1)
    return _rw_post(y_f, y_r, r, v, g, kd0, kd1, r_k, ln_w, ln_b)


def _diff_attn_kernel(q_ref, k_ref, v_ref, lam_ref, sub_ref, o_ref, *, tq, lambda_init):
    h = pl.program_id(1)
    i = pl.program_id(2)
    s = k_ref.shape[0]
    hd = DF_HEAD_DIM
    slope = jnp.where(h == 0, 2.0 ** -2, jnp.where(h == 1, 2.0 ** -4, jnp.where(h == 2, 2.0 ** -6, 2.0 ** -8)))
    lp = lam_ref[...]
    lam = (jnp.exp(jnp.sum(lp[0:1] * lp[1:2], axis=1, keepdims=True))
           - jnp.exp(jnp.sum(lp[2:3] * lp[3:4], axis=1, keepdims=True)) + lambda_init)
    qpos = (i * tq + lax.broadcasted_iota(jnp.int32, (tq, 1), 0)).astype(F32)
    kpos = lax.broadcasted_iota(jnp.int32, (1, s), 1).astype(F32)
    bias = (slope.astype(F32) * LOG2_E) * jnp.abs(qpos - kpos)
    v = v_ref[...].astype(BF16)
    outs = []
    for m in range(2):
        q = q_ref[:, m * hd:(m + 1) * hd] * (hd ** -0.5 * LOG2_E)
        sc = _dot_nt(q, k_ref[:, m * hd:(m + 1) * hd]) - bias
        ex = jnp.exp2(sc - jnp.max(sc, axis=1, keepdims=True))
        outs.append(_dot(ex, v) / jnp.sum(ex, axis=1, keepdims=True))
    o = outs[0] - lam * outs[1]
    o_ref[...] = _rms(o, 1e-5) * sub_ref[...] * (1.0 - lambda_init)


def _diff_attention(p1, lam_params, subln_w, lambda_init, tq=256):
    bsz, s, _ = p1.shape
    hw = 2 * DF_HEAD_DIM
    q0 = RW_IN // hw
    return pl.pallas_call(
        functools.partial(_diff_attn_kernel, tq=tq, lambda_init=lambda_init),
        grid=(bsz, DF_HEADS, s // tq),
        in_specs=[pl.BlockSpec((None, tq, hw), lambda b, h, i: (b, i, q0 + h)),
                  pl.BlockSpec((None, s, hw), lambda b, h, i: (b, 0, q0 + DF_HEADS + h)),
                  pl.BlockSpec((None, s, hw), lambda b, h, i: (b, 0, q0 + 2 * DF_HEADS + h)),
                  pl.BlockSpec((4, DF_HEAD_DIM), lambda b, h, i: (0, 0)),
                  pl.BlockSpec((1, hw), lambda b, h, i: (0, 0))],
        out_specs=pl.BlockSpec((None, tq, hw), lambda b, h, i: (b, i, h)),
        out_shape=jax.ShapeDtypeStruct((bsz, s, DF_HEADS * hw), F32),
        compiler_params=_params("parallel", "parallel", "parallel"),
        name="diff_attn",
    )(p1, p1, p1, lam_params, subln_w.reshape(1, hw))


def _layer0_weights(ab_w_in):
    d = ab_w_in.shape[0]
    main = 4 * DN_WIDTH
    gates = 4 * DN_HEADS
    w = jnp.concatenate([ab_w_in[:, :main], ab_w_in[:, main + gates:], ab_w_in[:, main:main + gates],
                         jnp.zeros((d, LANE - gates), ab_w_in.dtype)], axis=1)
    return w.astype(BF16)


def kernel(x, c, ada_w, ada_b, norm_mix_pre, norm_mix_post, norm_ffn_pre, norm_ffn_post, router_w, exp_w_gate, exp_w_up, exp_w_down, ab_w_in, ab_w_out, dn_conv, dn_a_log, dn_dt_bias, dn_norm, s5_a_re, s5_a_im, s5_log_dt, s5_b_re, s5_b_im, s5_c_re, s5_c_im, s5_d, s5_glu_w, s5_glu_b, cd_w_in, cd_w_out, rw_mu, rw_w0, rw_w2, rw_a0, rw_a2, rw_g2, rw_k_k, rw_k_a, rw_r_k, rw_ln_w, rw_ln_b, df_lambda, df_subln):
    depth = ada_w.shape[0]
    mod_all = _ada_mod(c, ada_w, ada_b)
    for layer in range(depth):
        mod = mod_all[layer][:, None, :]
        i = layer // 2
        if layer % 2 == 0:
            p = _in_proj(x, norm_mix_pre[layer], mod, _layer0_weights(ab_w_in[i]))
            ya = _gated_deltanet(p, dn_conv[i], dn_a_log[i], dn_dt_bias[i], dn_norm[i])
            yb = _s5_mixer(p, s5_a_re[i], s5_a_im[i], s5_log_dt[i], s5_b_re[i], s5_b_im[i], s5_c_re[i], s5_c_im[i],
                           s5_d[i], s5_glu_w[i], s5_glu_b[i])
            w_out = ab_w_out[i]
        else:
            lambda_init = 0.8 - 0.6 * math.exp(-0.3 * layer)
            p = _in_proj(x, norm_mix_pre[layer], mod, cd_w_in[i].astype(BF16))
            ya = _rwkv7_mixer(p, rw_mu[i], rw_w0[i], rw_w2[i], rw_a0[i], rw_a2[i], rw_g2[i], rw_k_k[i], rw_k_a[i],
                              rw_r_k[i].reshape(-1), rw_ln_w[i], rw_ln_b[i])
            yb = _diff_attention(p, df_lambda[i], df_subln[i], lambda_init)
            w_out = cd_w_out[i]
        x1, h2, pt = _mix_out(ya, yb, x, w_out.astype(BF16), norm_mix_post[layer], mod, norm_ffn_pre[layer],
                              router_w[layer])
        x = _expert_choice_ffn(x1, h2, pt, mod, norm_ffn_post[layer], exp_w_gate, exp_w_up, exp_w_down, layer)
    return x
```

```python
import functools
import math

import jax
import jax.numpy as jnp
import numpy as np
from jax import lax
from jax.experimental import pallas as pl
from jax.experimental.pallas import tpu as pltpu

F32 = jnp.float32
BF16 = jnp.bfloat16
HI = lax.Precision.HIGHEST

LANE = 128
SUBLANE = 8
VMEM_LIMIT = 56 * 1024 * 1024

NORM_EPS = 1e-6
DN_HEADS = 4
DN_HEAD_DIM = 128
DN_WIDTH = 512
DN_CONV = 5
DN_CHUNK = 64
S5_WIDTH = 512
S5_GROUP = 16
S5_GROUPS = 32
S5_STATE = 64
S5_CHUNK = 16
RW_WIDTH = 512
RW_HEADS = 8
RW_HEAD_DIM = 64
RW_LORA = 64
RW_GATE_LORA = 128
RW_GN_EPS = 64e-5
RW_CHUNK = 64
RW_IN = 1920
DF_WIDTH = 512
DF_HEADS = 4
DF_HEAD_DIM = 64
N_EXPERTS = 16
EC_CAPACITY_FACTOR = 2
NEG_BIG = -1e30
LOG2_E = 1.4426950408889634
MOE_TOKEN_TILE = 256
TOPK_BISECTIONS = 152


def _params(*sem):
    return pltpu.CompilerParams(dimension_semantics=sem, vmem_limit_bytes=VMEM_LIMIT)


def _sigmoid(x):
    return 1.0 / (1.0 + jnp.exp(-x))


def _silu(x):
    return x * _sigmoid(x)


def _softplus(x):
    return jnp.maximum(x, 0.0) + jnp.log(1.0 + jnp.exp(-jnp.abs(x)))


def _rms(x, eps):
    return x * lax.rsqrt(jnp.mean(x * x, axis=-1, keepdims=True) + eps)


def _mxu_operands(a, b, precision):
    if precision is None:
        return a.astype(BF16), b.astype(BF16)
    return a, b


def _dot(a, b, precision=None):
    a, b = _mxu_operands(a, b, precision)
    return jnp.dot(a, b, preferred_element_type=F32, precision=precision)


def _dot_nt(a, b, precision=None):
    a, b = _mxu_operands(a, b, precision)
    return lax.dot_general(a, b, (((1,), (1,)), ((), ())), preferred_element_type=F32, precision=precision)


def _dot_tn(a, b, precision=None):
    a, b = _mxu_operands(a, b, precision)
    return lax.dot_general(a, b, (((0,), (0,)), ((), ())), preferred_element_type=F32, precision=precision)


def _ada_kernel(c_ref, w_ref, b_ref, o_ref):
    o_ref[...] = _dot(_silu(c_ref[...]), w_ref[...], HI) + b_ref[...]


def _ada_mod(c, ada_w, ada_b):
    depth, d, n = ada_w.shape
    bsz = c.shape[0]
    rows = -(-bsz // SUBLANE) * SUBLANE
    c_pad = jnp.zeros((rows, d), F32).at[:bsz].set(c)
    out = pl.pallas_call(
        _ada_kernel,
        grid=(depth, n // d),
        in_specs=[pl.BlockSpec((rows, d), lambda l, j: (0, 0)),
                  pl.BlockSpec((None, d, d), lambda l, j: (l, 0, j)),
                  pl.BlockSpec((None, 1, d), lambda l, j: (l, 0, j))],
        out_specs=pl.BlockSpec((None, rows, d), lambda l, j: (l, 0, j)),
        out_shape=jax.ShapeDtypeStruct((depth, rows, n), F32),
        compiler_params=_params("parallel", "parallel"),
        name="ada_mod",
    )(c_pad, ada_w, ada_b.reshape(depth, 1, n))
    return out[:, :bsz]


def _in_proj_kernel(x_ref, gain_ref, sh_ref, sc_ref, w_ref, o_ref):
    h = _rms(x_ref[...], NORM_EPS) * gain_ref[...] * (1.0 + sc_ref[...]) + sh_ref[...]
    o_ref[...] = _dot(h.astype(BF16), w_ref[...])


def _in_proj(x, gain, mod, w_bf16, tm=256):
    bsz, s, d = x.shape
    n = w_bf16.shape[1]
    return pl.pallas_call(
        _in_proj_kernel,
        grid=(bsz, s // tm),
        in_specs=[pl.BlockSpec((None, tm, d), lambda b, i: (b, i, 0)),
                  pl.BlockSpec((1, d), lambda b, i: (0, 0)),
                  pl.BlockSpec((None, 1, d), lambda b, i: (b, 0, 0)),
                  pl.BlockSpec((None, 1, d), lambda b, i: (b, 0, 1)),
                  pl.BlockSpec((d, n), lambda b, i: (0, 0))],
        out_specs=pl.BlockSpec((None, tm, n), lambda b, i: (b, i, 0)),
        out_shape=jax.ShapeDtypeStruct((bsz, s, n), F32),
        compiler_params=_params("parallel", "parallel"),
        name="in_proj",
    )(x, gain.reshape(1, d), mod, mod, w_bf16)


def _split3(x):
    hi = x.astype(BF16)
    rest = x - hi.astype(F32)
    mid = rest.astype(BF16)
    return hi, mid, (rest - mid.astype(F32)).astype(BF16)


def _mask_dot(mask, x):
    m = mask.astype(BF16)
    return sum(jnp.dot(m, part, preferred_element_type=F32) for part in _split3(x))


def _dot_mask(x, mask):
    m = mask.astype(BF16)
    return sum(jnp.dot(part, m, preferred_element_type=F32) for part in _split3(x))


def _mask_dot_nt(x, mask):
    m = mask.astype(BF16)
    return sum(lax.dot_general(part, m, (((1,), (1,)), ((), ())), preferred_element_type=F32)
               for part in _split3(x))


class _ChunkMasks:
    def __init__(self, n, chunk, reverse):
        ii = lax.broadcasted_iota(jnp.int32, (n, n), 0)
        jj = lax.broadcasted_iota(jnp.int32, (n, n), 1)
        same = (ii // chunk) == (jj // chunk)
        tri = (ii <= jj) if reverse else (ii >= jj)
        self.same = same
        self.incl = same & tri
        self.strict = self.incl & (ii != jj)
        self.eye = (ii == jj).astype(F32)
        m16 = (ii // 16) == (jj // 16)
        m32 = (ii // 32) == (jj // 32)
        self.m16 = m16.astype(F32)
        self.m32_only = (m32 & ~m16).astype(F32)
        self.m64_only = (same & ~m32).astype(F32)


def _unit_tri_inverses(lms, masks):
    ps = [-(lm * masks.m16) for lm in lms]
    ts = [masks.eye + p for p in ps]
    for _ in range(3):
        ps = [_dot(p, p) for p in ps]
        ts = [t + _dot(t, p) for t, p in zip(ts, ps)]
    for level in (masks.m32_only, masks.m64_only):
        cs = [_dot(lm * level, t) for lm, t in zip(lms, ts)]
        ts = [t - _dot(t, c) for t, c in zip(ts, cs)]
    return ts


def _dn_prep_kernel(cur_ref, prev_ref, next_ref, gates_ref, convw_ref, alog_ref, dtb_ref,
                    qkv_ref, gb_ref, ext_ref, *, ts):
    i = pl.program_id(1)
    n = pl.num_programs(1)
    halo = SUBLANE
    ext_ref[0:halo, :] = jnp.where(i > 0, prev_ref[...], 0.0)
    ext_ref[halo:halo + ts, :] = cur_ref[...]
    ext_ref[halo + ts:2 * halo + ts, :] = jnp.where(i < n - 1, next_ref[...], 0.0)
    pad = DN_CONV // 2
    for blk in range(3 * DN_HEADS):
        cols = slice(blk * DN_HEAD_DIM, (blk + 1) * DN_HEAD_DIM)
        acc = jnp.zeros((ts, DN_HEAD_DIM), F32)
        for tap in range(DN_CONV):
            acc = acc + ext_ref[halo - pad + tap:halo - pad + tap + ts, cols] * convw_ref[tap:tap + 1, cols]
        val = _silu(acc)
        if blk < 2 * DN_HEADS:
            val = val * lax.rsqrt(jnp.sum(val * val, axis=-1, keepdims=True) + 1e-6)
            if blk < DN_HEADS:
                val = val * (DN_HEAD_DIM ** -0.5)
        qkv_ref[:, cols] = val
    gin = gates_ref[...]
    g = -jnp.exp(alog_ref[...]) * _softplus(gin + dtb_ref[...])
    lane = lax.broadcasted_iota(jnp.int32, gin.shape, 1)
    gb_ref[...] = jnp.where(lane < 2 * DN_HEADS, g, _sigmoid(gin))


def _dn_prep(p0, conv_w, a_log, dt_bias, ts=512):
    bsz, s, _ = p0.shape
    c3 = 3 * DN_WIDTH
    gate_blk = (4 * DN_WIDTH + S5_WIDTH) // LANE
    nb8 = s // SUBLANE
    r8 = ts // SUBLANE
    alog = jnp.zeros((1, LANE), F32).at[0, :2 * DN_HEADS].set(a_log.reshape(-1))
    dtb = jnp.zeros((1, LANE), F32).at[0, :2 * DN_HEADS].set(dt_bias.reshape(-1))
    return pl.pallas_call(
        functools.partial(_dn_prep_kernel, ts=ts),
        grid=(bsz, s // ts),
        in_specs=[pl.BlockSpec((None, ts, c3), lambda b, i: (b, i, 0)),
                  pl.BlockSpec((None, SUBLANE, c3), lambda b, i: (b, jnp.maximum(i * r8 - 1, 0), 0)),
                  pl.BlockSpec((None, SUBLANE, c3), lambda b, i: (b, jnp.minimum((i + 1) * r8, nb8 - 1), 0)),
                  pl.BlockSpec((None, ts, LANE), lambda b, i: (b, i, gate_blk)),
                  pl.BlockSpec((DN_CONV, c3), lambda b, i: (0, 0)),
                  pl.BlockSpec((1, LANE), lambda b, i: (0, 0)),
                  pl.BlockSpec((1, LANE), lambda b, i: (0, 0))],
        out_specs=[pl.BlockSpec((None, ts, c3), lambda b, i: (b, i, 0)),
                   pl.BlockSpec((None, ts, LANE), lambda b, i: (b, i, 0))],
        out_shape=[jax.ShapeDtypeStruct((bsz, s, c3), F32),
                   jax.ShapeDtypeStruct((bsz, s, LANE), F32)],
        scratch_shapes=[pltpu.VMEM((ts + 2 * SUBLANE, c3), F32)],
        compiler_params=_params("parallel", "parallel"),
        name="dn_prep",
    )(p0, p0, p0, p0, conv_w, alog, dtb)


def _dn_direction(q_ref, k_ref, v_ref, gb_ref, gbt_ref, o_ref, state_ref, *, sc, reverse, direction):
    chunk = DN_CHUNK
    nck = sc // chunk
    masks = _ChunkMasks(sc, chunk, reverse)
    gcols = gb_ref[...]
    gc_cols = _mask_dot(masks.incl, gcols)
    gt_cols = _mask_dot(masks.same, gcols)
    gc_rows = _mask_dot_nt(gbt_ref[...], masks.incl)
    e_gc = jnp.exp(gc_cols)
    e_rest = jnp.exp(gt_cols - gc_cols)
    e_tot = jnp.exp(gt_cols)
    order = range(nck - 1, -1, -1) if reverse else range(nck)
    steps = []
    for h in range(DN_HEADS):
        gi = direction * DN_HEADS + h
        bi = 2 * DN_HEADS + gi
        cols = slice(h * DN_HEAD_DIM, (h + 1) * DN_HEAD_DIM)
        q = q_ref[:, cols]
        k = k_ref[:, cols]
        beta = gcols[:, bi:bi + 1]
        eg = e_gc[:, gi:gi + 1]
        decay = jnp.exp(jnp.where(masks.incl, gc_cols[:, gi:gi + 1] - gc_rows[gi:gi + 1, :], NEG_BIG))
        kb = k * beta
        attn = _dot_nt(q, k) * decay
        steps.append(dict(
            lm=jnp.where(masks.strict, _dot_nt(kb, k) * decay, 0.0), vb=v_ref[:, cols] * beta, kbg=kb * eg,
            attn=[attn[c * chunk:(c + 1) * chunk, c * chunk:(c + 1) * chunk] for c in range(nck)], qd=q * eg,
            kd=k * e_rest[:, gi:gi + 1], gl=e_tot[:, gi:gi + 1], cols=cols, o_ref=o_ref, state_ref=state_ref, h=h,
            order=order))
    for st, t in zip(steps, _unit_tri_inverses([st.pop('lm') for st in steps], masks)):
        st['u'] = _dot(t, st.pop('vb'))
        st['w'] = _dot(t, st.pop('kbg'))
    return steps


def _dn_recurrence(steps, chunk):
    states = [st['state_ref'][st['h']] for st in steps]
    for pos in range(len(steps[0]['order'])):
        for n, st in enumerate(steps):
            c = st['order'][pos]
            rows = slice(c * chunk, (c + 1) * chunk)
            v_new = st['u'][rows] - _dot(st['w'][rows], states[n])
            st['o_ref'][rows, st['cols']] = _dot(st['qd'][rows], states[n]) + _dot(st['attn'][c], v_new)
            states[n] = states[n] * st['gl'][c * chunk:c * chunk + 1] + _dot_tn(st['kd'][rows], v_new)
    for n, st in enumerate(steps):
        st['state_ref'][st['h']] = states[n]


def _dn_chunk_kernel(qf, kf, vf, gbf, gbtf, qr, kr, vr, gbr, gbtr, of_ref, or_ref, sf_ref, sr_ref, *, sc):
    @pl.when(pl.program_id(1) == 0)
    def _():
        sf_ref[...] = jnp.zeros_like(sf_ref)
        sr_ref[...] = jnp.zeros_like(sr_ref)

    steps = (_dn_direction(qf, kf, vf, gbf, gbtf, of_ref, sf_ref, sc=sc, reverse=False, direction=0)
             + _dn_direction(qr, kr, vr, gbr, gbtr, or_ref, sr_ref, sc=sc, reverse=True, direction=1))
    _dn_recurrence(steps, DN_CHUNK)


def _dn_chunked(qkv, gb, sc=256):
    bsz, s, _ = qkv.shape
    n = s // sc
    gbt = jnp.swapaxes(gb[..., :4 * DN_HEADS], 1, 2)
    fwd = lambda col: pl.BlockSpec((None, sc, DN_WIDTH), lambda b, i: (b, i, col))
    rev = lambda col: pl.BlockSpec((None, sc, DN_WIDTH), lambda b, i: (b, n - 1 - i, col))
    in_specs = ([fwd(0), fwd(1), fwd(2),
                 pl.BlockSpec((None, sc, LANE), lambda b, i: (b, i, 0)),
                 pl.BlockSpec((None, 4 * DN_HEADS, sc), lambda b, i: (b, 0, i))]
                + [rev(0), rev(1), rev(2),
                   pl.BlockSpec((None, sc, LANE), lambda b, i: (b, n - 1 - i, 0)),
                   pl.BlockSpec((None, 4 * DN_HEADS, sc), lambda b, i: (b, 0, n - 1 - i))])
    state = pltpu.VMEM((DN_HEADS, DN_HEAD_DIM, DN_HEAD_DIM), F32)
    return pl.pallas_call(
        functools.partial(_dn_chunk_kernel, sc=sc),
        grid=(bsz, n),
        in_specs=in_specs,
        out_specs=[pl.BlockSpec((None, sc, DN_WIDTH), lambda b, i: (b, i, 0)),
                   pl.BlockSpec((None, sc, DN_WIDTH), lambda b, i: (b, n - 1 - i, 0))],
        out_shape=[jax.ShapeDtypeStruct((bsz, s, DN_WIDTH), F32)] * 2,
        scratch_shapes=[state, state],
        compiler_params=_params("parallel", "arbitrary"),
        name="dn_chunked",
    )(qkv, qkv, qkv, gb, gbt, qkv, qkv, qkv, gb, gbt)


def _dn_post_kernel(of_ref, or_ref, z_ref, nw_ref, o_ref):
    for h in range(DN_HEADS):
        cols = slice(h * DN_HEAD_DIM, (h + 1) * DN_HEAD_DIM)
        o = of_ref[:, cols] + or_ref[:, cols]
        o_ref[:, cols] = _rms(o, NORM_EPS) * nw_ref[...] * _silu(z_ref[:, cols])


def _dn_post(o_f, o_r, p0, norm_w, ts=512):
    bsz, s, w = o_f.shape
    z_blk = 3 * DN_WIDTH // w
    spec = pl.BlockSpec((None, ts, w), lambda b, i: (b, i, 0))
    return pl.pallas_call(
        _dn_post_kernel,
        grid=(bsz, s // ts),
        in_specs=[spec, spec,
                  pl.BlockSpec((None, ts, w), lambda b, i: (b, i, z_blk)),
                  pl.BlockSpec((1, DN_HEAD_DIM), lambda b, i: (0, 0))],
        out_specs=spec,
        out_shape=jax.ShapeDtypeStruct((bsz, s, w), F32),
        compiler_params=_params("parallel", "parallel"),
        name="dn_post",
    )(o_f, o_r, p0, norm_w.reshape(1, DN_HEAD_DIM))


def _gated_deltanet(p0, conv_w, a_log, dt_bias, norm_w):
    qkv, gb = _dn_prep(p0, conv_w, a_log, dt_bias)
    o_f, o_r = _dn_chunked(qkv, gb)
    return _dn_post(o_f, o_r, p0, norm_w)


def _s5_tables(a_re, a_im, log_dt, b_re, b_im, c_re, c_im, n_chunks):
    lc, p, hh = S5_CHUNK, S5_STATE, S5_GROUP
    levels = int(math.log2(n_chunks))
    dt = jnp.exp(log_dt)[..., None]
    mag = jnp.exp(a_re * dt)
    lb_re, lb_im = mag * jnp.cos(a_im * dt), mag * jnp.sin(a_im * dt)
    den = a_re * a_re + a_im * a_im
    nr, ni = lb_re - 1.0, lb_im
    f_re = (nr * a_re + ni * a_im) / den
    f_im = (ni * a_re - nr * a_im) / den
    bb_re = f_re[..., None] * b_re - f_im[..., None] * b_im
    bb_im = f_re[..., None] * b_im + f_im[..., None] * b_re

    def power(tau):
        tau = jnp.asarray(tau, F32)[:, None]
        m = jnp.exp((a_re * dt)[:, :, None, :] * tau)
        ang = (a_im * dt)[:, :, None, :] * tau
        return m * jnp.cos(ang), m * jnp.sin(ang)

    pw_re, pw_im = power(np.arange(lc + 1))
    bbt_re, bbt_im = jnp.swapaxes(bb_re, 2, 3), jnp.swapaxes(bb_im, 2, 3)
    lbb_re = pw_re[:, :, :, None, :] * bbt_re[:, :, None, :, :] - pw_im[:, :, :, None, :] * bbt_im[:, :, None, :, :]
    lbb_im = pw_re[:, :, :, None, :] * bbt_im[:, :, None, :, :] + pw_im[:, :, :, None, :] * bbt_re[:, :, None, :, :]
    kt = (jnp.einsum('dgop,dgtip->dgito', c_re, lbb_re[:, :, :lc], precision=HI)
          - jnp.einsum('dgop,dgtip->dgito', c_im, lbb_im[:, :, :lc], precision=HI))
    strips = jnp.stack([kt[0], kt[1][:, :, ::-1, :]]).reshape(2, -1, hh, lc * hh)
    bmats, cmats = [], []
    for d in range(2):
        e_in = (lc - 1 - np.arange(lc)) if d == 0 else np.arange(lc)
        bmats.append(jnp.concatenate([lbb_re[d][:, e_in], lbb_im[d][:, e_in]], axis=-1).reshape(-1, lc * hh, 2 * p))
        e_out = (np.arange(lc) + 1) if d == 0 else (lc - np.arange(lc))
        sel_re, sel_im = pw_re[d][:, e_out, None, :], pw_im[d][:, e_out, None, :]
        m_re = c_re[d][:, None] * sel_re - c_im[d][:, None] * sel_im
        m_im = c_re[d][:, None] * sel_im + c_im[d][:, None] * sel_re
        cmats.append(jnp.concatenate([m_re, -m_im], axis=-1).reshape(-1, lc * hh, 2 * p))
    sq_re, sq_im = power(lc * 2.0 ** np.arange(levels))
    scan_a = jnp.concatenate([sq_re, sq_re], axis=-1)
    scan_b = jnp.concatenate([-sq_im, sq_im], axis=-1)
    return strips, jnp.stack(bmats), jnp.stack(cmats), scan_a, scan_b


def _s5_kernel(u_ref, strip_ref, bmat_ref, cmat_ref, sa_ref, sb_ref, y_ref, toep_ref, *, n_chunks):
    u = u_ref[...]
    n = u.shape[0]
    p2 = 2 * S5_STATE
    lc, hh = S5_CHUNK, S5_GROUP
    levels = sa_ref.shape[1]
    cidx = lax.broadcasted_iota(jnp.int32, (n, p2), 0) & (n_chunks - 1)
    lane = lax.broadcasted_iota(jnp.int32, (hh, lc * hh), 1)
    y = None
    for d in range(2):
        strip = strip_ref[d]
        for ti in range(lc):
            if d == 0:
                blk = jnp.where(lane >= ti * hh, pltpu.roll(strip, ti * hh, 1), 0.0)
            else:
                blk = jnp.where(lane < (ti + 1) * hh, pltpu.roll(strip, ((ti + 1) * hh) % (lc * hh), 1), 0.0)
            toep_ref[ti * hh:(ti + 1) * hh, :] = blk
        x = _dot(u, bmat_ref[d])
        for lv in range(levels):
            sh = 1 << lv
            if d == 0:
                xs = jnp.where(cidx >= sh, pltpu.roll(x, sh, 0), 0.0)
            else:
                xs = jnp.where(cidx < n_chunks - sh, pltpu.roll(x, n - sh, 0), 0.0)
            x = x + sa_ref[d, lv:lv + 1, :] * xs + sb_ref[d, lv:lv + 1, :] * pltpu.roll(xs, S5_STATE, 1)
        if d == 0:
            x_in = jnp.where(cidx >= 1, pltpu.roll(x, 1, 0), 0.0)
        else:
            x_in = jnp.where(cidx < n_chunks - 1, pltpu.roll(x, n - 1, 0), 0.0)
        yd = _dot(u, toep_ref[...]) + _dot_nt(x_in, cmat_ref[d])
        y = yd if y is None else y + yd
    y_ref[...] = y


def _s5_pack_kernel(*refs):
    *u_refs, o_ref = refs
    lc, hh = S5_CHUNK, S5_GROUP
    per_slab = LANE // hh
    nch = o_ref.shape[0]
    for t in range(lc):
        for k, u_ref in enumerate(u_refs):
            piece = u_ref[pl.ds(t, nch, stride=lc), :]
            for j in range(per_slab):
                dst = (k * per_slab + j) * lc * hh + t * hh
                o_ref[:, dst:dst + hh] = piece[:, j * hh:(j + 1) * hh]


def _s5_pack(p0, ts=512):
    bsz, s, _ = p0.shape
    lc, g, hh = S5_CHUNK, S5_GROUPS, S5_GROUP
    slab0 = 4 * DN_WIDTH // LANE
    nt = s // ts
    slab = lambda k: pl.BlockSpec((None, ts, LANE), lambda b, i: (b, i, slab0 + k))
    n_slabs = S5_WIDTH // LANE
    return pl.pallas_call(
        _s5_pack_kernel,
        grid=(bsz, nt),
        in_specs=[slab(k) for k in range(n_slabs)],
        out_specs=pl.BlockSpec((ts // lc, g * lc * hh), lambda b, i: (b * nt + i, 0)),
        out_shape=jax.ShapeDtypeStruct((bsz * s // lc, g * lc * hh), F32),
        compiler_params=_params("parallel", "parallel"),
        name="s5_pack",
    )(*([p0] * n_slabs))


def _s5_scan(uc, tables, bsz):
    lc, g, hh = S5_CHUNK, S5_GROUPS, S5_GROUP
    nc = uc.shape[0] // bsz
    strips, bmat, cmat, sa, sb = tables
    levels = sa.shape[2]
    return pl.pallas_call(
        functools.partial(_s5_kernel, n_chunks=nc),
        grid=(g,),
        in_specs=[pl.BlockSpec((bsz * nc, lc * hh), lambda i: (0, i)),
                  pl.BlockSpec((2, None, hh, lc * hh), lambda i: (0, i, 0, 0)),
                  pl.BlockSpec((2, None, lc * hh, 2 * S5_STATE), lambda i: (0, i, 0, 0)),
                  pl.BlockSpec((2, None, lc * hh, 2 * S5_STATE), lambda i: (0, i, 0, 0)),
                  pl.BlockSpec((2, None, levels, 2 * S5_STATE), lambda i: (0, i, 0, 0)),
                  pl.BlockSpec((2, None, levels, 2 * S5_STATE), lambda i: (0, i, 0, 0))],
        out_specs=pl.BlockSpec((bsz * nc, lc * hh), lambda i: (0, i)),
        out_shape=jax.ShapeDtypeStruct((bsz * nc, g * lc * hh), F32),
        scratch_shapes=[pltpu.VMEM((lc * hh, lc * hh), F32)],
        compiler_params=_params("parallel"),
        name="s5_scan",
    )(uc, strips, bmat, cmat, sa, sb)


def _gelu_tanh(x):
    return 0.5 * x * (1.0 + jnp.tanh(math.sqrt(2.0 / math.pi) * (x + 0.044715 * (x * x * x))))


def _s5_post_kernel(u_ref, yc_ref, d_ref, w_ref, b_ref, o_ref, step_ref, y_ref):
    lc, hh = S5_CHUNK, S5_GROUP
    nch = yc_ref.shape[0]
    n_slabs = y_ref.shape[0]
    for t in range(lc):
        for g in range(S5_GROUPS):
            step_ref[:, g * hh:(g + 1) * hh] = yc_ref[:, g * lc * hh + t * hh:g * lc * hh + (t + 1) * hh]
        for k in range(n_slabs):
            y_ref[k, pl.ds(t, nch, stride=lc), :] = step_ref[:, k * LANE:(k + 1) * LANE]
    y_scan = jnp.concatenate([y_ref[k] for k in range(n_slabs)], axis=1)
    y = _gelu_tanh(d_ref[...] * u_ref[...] + y_scan)
    o_ref[...] = y * _sigmoid(_dot(y, w_ref[...]) + b_ref[...])


def _s5_post(p0, yc, d_skip, glu_w, glu_b, ts=512):
    bsz, s, _ = p0.shape
    w = S5_WIDTH
    lc = S5_CHUNK
    nt = s // ts
    u_blk = 4 * DN_WIDTH // w
    spec = pl.BlockSpec((None, ts, w), lambda b, i: (b, i, 0))
    row = pl.BlockSpec((1, w), lambda b, i: (0, 0))
    return pl.pallas_call(
        _s5_post_kernel,
        grid=(bsz, nt),
        in_specs=[pl.BlockSpec((None, ts, w), lambda b, i: (b, i, u_blk)),
                  pl.BlockSpec((ts // lc, yc.shape[1]), lambda b, i: (b * nt + i, 0)), row,
                  pl.BlockSpec((w, w), lambda b, i: (0, 0)), row],
        out_specs=spec,
        out_shape=jax.ShapeDtypeStruct((bsz, s, w), F32),
        scratch_shapes=[pltpu.VMEM((ts // lc, w), F32), pltpu.VMEM((w // LANE, ts, LANE), F32)],
        compiler_params=_params("parallel", "parallel"),
        name="s5_post",
    )(p0, yc, d_skip.reshape(1, w), glu_w, glu_b.reshape(1, w))


def _s5_mixer(p0, a_re, a_im, log_dt, b_re, b_im, c_re, c_im, d_skip, glu_w, glu_b):
    bsz, s, _ = p0.shape
    tables = _s5_tables(a_re, a_im, log_dt, b_re, b_im, c_re, c_im, s // S5_CHUNK)
    return _s5_post(p0, _s5_scan(_s5_pack(p0), tables, bsz), d_skip, glu_w, glu_b)


def _mix_out_kernel(ya_ref, yb_ref, x_ref, w_ref, npost_ref, g1_ref, npre_ref, sh2_ref, sc2_ref, rwt_ref,
                    x1_ref, h2_ref, pt_ref):
    half = ya_ref.shape[-1]
    y = _dot(ya_ref[...].astype(BF16), w_ref[0:half, :]) + _dot(yb_ref[...].astype(BF16), w_ref[half:2 * half, :])
    x1 = x_ref[...] + g1_ref[...] * (_rms(y, NORM_EPS) * npost_ref[...])
    x1_ref[...] = x1
    h2 = _rms(x1, NORM_EPS) * npre_ref[...] * (1.0 + sc2_ref[...]) + sh2_ref[...]
    h_hi = h2.astype(BF16)
    h2_ref[...] = h_hi
    h_lo = (h2 - h_hi.astype(F32)).astype(BF16)
    r = rwt_ref[...]
    r_hi = r.astype(BF16)
    r_lo = (r - r_hi.astype(F32)).astype(BF16)
    logits = _dot_nt(r_hi, h_hi) + _dot_nt(r_lo, h_hi) + _dot_nt(r_hi, h_lo)
    ex = jnp.exp(logits - jnp.max(logits, axis=0, keepdims=True))
    pt_ref[...] = ex / jnp.sum(ex, axis=0, keepdims=True)


def _mix_out(ya, yb, x, w_out_bf16, npost, mod, npre, router_w, tm=256):
    bsz, s, d = x.shape
    half = ya.shape[-1]
    ne = router_w.shape[-1]
    row = lambda a: a.reshape(1, d)
    vec = pl.BlockSpec((1, d), lambda b, i: (0, 0))
    modblk = lambda k: pl.BlockSpec((None, 1, d), lambda b, i: (b, 0, k))
    tile = pl.BlockSpec((None, tm, d), lambda b, i: (b, i, 0))
    halfspec = pl.BlockSpec((None, tm, half), lambda b, i: (b, i, 0))
    return pl.pallas_call(
        _mix_out_kernel,
        grid=(bsz, s // tm),
        in_specs=[halfspec, halfspec, tile,
                  pl.BlockSpec((d, d), lambda b, i: (0, 0)), vec, modblk(2), vec, modblk(3), modblk(4),
                  pl.BlockSpec((ne, d), lambda b, i: (0, 0))],
        out_specs=[tile, tile, pl.BlockSpec((None, ne, tm), lambda b, i: (b, 0, i))],
        out_shape=[jax.ShapeDtypeStruct((bsz, s, d), F32),
                   jax.ShapeDtypeStruct((bsz, s, d), BF16),
                   jax.ShapeDtypeStruct((bsz, ne, s), F32)],
        compiler_params=_params("parallel", "parallel"),
        name="mix_out",
    )(ya, yb, x, w_out_bf16, row(npost), mod, row(npre), mod, mod, router_w.T)


def _prefix_lanes(x):
    n = x.shape[1]
    lane = lax.broadcasted_iota(jnp.int32, x.shape, 1)
    sh = 1
    while sh < n:
        x = x + jnp.where(lane >= sh, pltpu.roll(x, sh, 1), 0.0)
        sh *= 2
    return x


def _topk_kernel(p_ref, slot_ref, count_ref, *, cap):
    p = p_ref[...]
    rows = p.shape[0]

    def halve(_, bounds):
        lo, hi = bounds
        mid = 0.5 * (lo + hi)
        ok = jnp.sum((p >= mid).astype(F32), axis=1, keepdims=True) >= cap
        return jnp.where(ok, mid, lo), jnp.where(ok, hi, mid)

    lo, _ = lax.fori_loop(0, TOPK_BISECTIONS, halve, (jnp.zeros((rows, 1), F32), jnp.full((rows, 1), 2.0, F32)))
    gt = p > lo
    eq = p == lo
    need = cap - jnp.sum(gt.astype(F32), axis=1, keepdims=True)
    sel = gt | (eq & (_prefix_lanes(eq.astype(F32)) <= need))
    count = _prefix_lanes(sel.astype(F32))
    slot_ref[...] = jnp.where(sel, count - 1.0, -1.0).astype(jnp.int32)
    count_ref[...] = count.astype(jnp.int32)


def _topk_slots(pt, cap):
    bsz, ne, s = pt.shape
    spec = pl.BlockSpec((bsz * ne, s), lambda i: (0, 0))
    slot, count = pl.pallas_call(
        functools.partial(_topk_kernel, cap=cap),
        grid=(1,), in_specs=[spec], out_specs=[spec, spec],
        out_shape=[jax.ShapeDtypeStruct((bsz * ne, s), jnp.int32)] * 2,
        compiler_params=_params("arbitrary"),
        name="topk_slots",
    )(pt.reshape(bsz * ne, s))
    return slot.reshape(bsz, ne, s), count.reshape(bsz, ne, s)


def _tile_starts(count, tile):
    bsz, ne, _ = count.shape
    ends = count[:, :, tile - 1::tile]
    return jnp.concatenate([jnp.zeros((bsz, ne, 1), jnp.int32), ends], axis=2).reshape(-1)


BF16_ROWS = 2 * SUBLANE


def _expert_windows(starts_ref, ne, win):
    n_tiles = pl.num_programs(1)
    shift = win.bit_length() - 1
    firsts, n_pass = [], 0
    for e in range(ne):
        base = (pl.program_id(0) * ne + e) * (n_tiles + 1) + pl.program_id(1)
        first = (starts_ref[base] // BF16_ROWS) * BF16_ROWS
        firsts.append(first)
        n_pass = jnp.maximum(n_pass, (starts_ref[base + 1] - first + win - 1) >> shift)
    return firsts, n_pass


def _window_hits(slot_row, first, w, cap, win):
    row0 = first + w * win
    src = pl.multiple_of(jnp.minimum(row0, cap - win), BF16_ROWS)
    rows = src + lax.broadcasted_iota(jnp.int32, (win, slot_row.shape[1]), 0)
    return src, (rows == slot_row) & (rows >= row0)


def _gather_kernel(starts_ref, slot_ref, p_ref, h_ref, xin_ref, gate_ref, hit_ref, *, cap, win):
    ne = slot_ref.shape[0]

    @pl.when(pl.program_id(1) == 0)
    def _():
        xin_ref[...] = jnp.zeros_like(xin_ref)
        gate_ref[...] = jnp.zeros_like(gate_ref)

    firsts, n_pass = _expert_windows(starts_ref, ne, win)

    def one_pass(w, carry):
        srcs = []
        for e in range(ne):
            src, hit = _window_hits(slot_ref[e:e + 1, :], firsts[e], w, cap, win)
            srcs.append(src)
            hit_ref[e * win:(e + 1) * win, :] = hit.astype(BF16)
            gate_ref[e, pl.ds(src, win), :] += jnp.sum(jnp.where(hit, p_ref[e:e + 1, :], 0.0), axis=1, keepdims=True)
        rows = _dot(hit_ref[...], h_ref[...])
        for e in range(ne):
            dst = xin_ref.at[e, pl.ds(srcs[e], win), :]
            dst[...] = (dst[...].astype(F32) + rows[e * win:(e + 1) * win, :]).astype(BF16)
        return carry

    lax.fori_loop(0, n_pass, one_pass, 0)


def _moe_gather(starts, slot, pt, h_bf16, cap, tk, win=64):
    bsz, ne, s = slot.shape
    d = h_bf16.shape[-1]
    rowspec = pl.BlockSpec((None, ne, tk), lambda b, k, st: (b, 0, k))
    return pl.pallas_call(
        functools.partial(_gather_kernel, cap=cap, win=win),
        grid_spec=pltpu.PrefetchScalarGridSpec(
            num_scalar_prefetch=1,
            grid=(bsz, s // tk),
            in_specs=[rowspec, rowspec, pl.BlockSpec((None, tk, d), lambda b, k, st: (b, k, 0))],
            out_specs=[pl.BlockSpec((None, ne, cap, d), lambda b, k, st: (b, 0, 0, 0)),
                       pl.BlockSpec((None, ne, cap, 1), lambda b, k, st: (b, 0, 0, 0))],
            scratch_shapes=[pltpu.VMEM((ne * win, tk), BF16)]),
        out_shape=[jax.ShapeDtypeStruct((bsz, ne, cap, d), BF16),
                   jax.ShapeDtypeStruct((bsz, ne, cap, 1), F32)],
        compiler_params=_params("parallel", "arbitrary"),
        name="moe_gather",
    )(starts, slot, pt, h_bf16)


def _ffn_kernel(xin_ref, gate_ref, wg_ref, wu_ref, wd_ref, y_ref, acc_ref):
    f = pl.program_id(1)
    bsz, cap, d = xin_ref.shape
    x = xin_ref[...].reshape(bsz * cap, d)
    @pl.when(f == 0)
    def _():
        acc_ref[...] = jnp.zeros_like(acc_ref)

    hid = _silu(_dot(x, wg_ref[...].astype(BF16))) * _dot(x, wu_ref[...].astype(BF16))
    acc_ref[...] += _dot(hid.astype(BF16), wd_ref[...].astype(BF16))

    @pl.when(f == pl.num_programs(1) - 1)
    def _():
        y_ref[...] = (acc_ref[...] * gate_ref[...].reshape(bsz * cap, 1)).astype(BF16).reshape(bsz, cap, d)


def _moe_ffn(xin, gate, w_gate, w_up, w_down, layer, tf=256):
    bsz, ne, cap, d = xin.shape
    dexp = w_gate.shape[-1]
    return pl.pallas_call(
        _ffn_kernel,
        grid=(ne, dexp // tf),
        in_specs=[pl.BlockSpec((bsz, None, cap, d), lambda e, f: (0, e, 0, 0)),
                  pl.BlockSpec((bsz, None, cap, 1), lambda e, f: (0, e, 0, 0)),
                  pl.BlockSpec((None, None, d, tf), lambda e, f: (layer, e, 0, f)),
                  pl.BlockSpec((None, None, d, tf), lambda e, f: (layer, e, 0, f)),
                  pl.BlockSpec((None, None, tf, d), lambda e, f: (layer, e, f, 0))],
        out_specs=pl.BlockSpec((bsz, None, cap, d), lambda e, f: (0, e, 0, 0)),
        out_shape=jax.ShapeDtypeStruct((bsz, ne, cap, d), BF16),
        scratch_shapes=[pltpu.VMEM((bsz * cap, d), F32)],
        compiler_params=_params("parallel", "arbitrary"),
        name="moe_ffn",
    )(xin, gate, w_gate, w_up, w_down)


def _scatter_kernel(starts_ref, slot_ref, y_ref, x_ref, g2_ref, npost_ref, o_ref, ywin_ref, hit_ref, acc_ref,
                    *, cap, win):
    ne = y_ref.shape[0]
    firsts, n_pass = _expert_windows(starts_ref, ne, win)
    acc_ref[...] = jnp.zeros_like(acc_ref)

    def one_pass(w, carry):
        for e in range(ne):
            src, hit = _window_hits(slot_ref[e:e + 1, :], firsts[e], w, cap, win)
            hit_ref[e * win:(e + 1) * win, :] = hit.astype(BF16)
            ywin_ref[e * win:(e + 1) * win, :] = y_ref[e, pl.ds(src, win), :]
        acc_ref[...] += _dot_tn(hit_ref[...], ywin_ref[...])
        return carry

    lax.fori_loop(0, n_pass, one_pass, 0)
    o_ref[...] = x_ref[...] + g2_ref[...] * (_rms(acc_ref[...], NORM_EPS) * npost_ref[...])


def _moe_scatter(starts, slot, y, x, mod, npost, ts, win=128):
    bsz, ne, cap, d = y.shape
    s = x.shape[1]
    win = min(win, cap)
    tile = pl.BlockSpec((None, ts, d), lambda b, i, st: (b, i, 0))
    return pl.pallas_call(
        functools.partial(_scatter_kernel, cap=cap, win=win),
        grid_spec=pltpu.PrefetchScalarGridSpec(
            num_scalar_prefetch=1,
            grid=(bsz, s // ts),
            in_specs=[pl.BlockSpec((None, ne, ts), lambda b, i, st: (b, 0, i)),
                      pl.BlockSpec((None, ne, cap, d), lambda b, i, st: (b, 0, 0, 0)),
                      tile,
                      pl.BlockSpec((None, 1, d), lambda b, i, st: (b, 0, 5)),
                      pl.BlockSpec((1, d), lambda b, i, st: (0, 0))],
            out_specs=tile,
            scratch_shapes=[pltpu.VMEM((ne * win, d), BF16), pltpu.VMEM((ne * win, ts), BF16),
                            pltpu.VMEM((ts, d), F32)]),
        out_shape=jax.ShapeDtypeStruct((bsz, s, d), F32),
        compiler_params=_params("parallel", "arbitrary"),
        name="moe_scatter",
    )(starts, slot, y, x, mod, npost.reshape(1, d))


def _expert_choice_ffn(x1, h2, pt, mod, npost, w_gate, w_up, w_down, layer):
    s = x1.shape[1]
    ne = pt.shape[1]
    cap = EC_CAPACITY_FACTOR * s // ne
    tile = MOE_TOKEN_TILE
    slot, count = _topk_slots(pt, cap)
    starts = _tile_starts(count, tile)
    xin, gate = _moe_gather(starts, slot, pt, h2, cap, tile)
    y = _moe_ffn(xin, gate, w_gate, w_up, w_down, layer)
    return _moe_scatter(starts, slot, y, x1, mod, npost, tile)


def _rw_prep_kernel(cur_ref, prev_ref, next_ref, mu_ref, w0_ref, w2_ref, a0_ref, a2_ref, g2_ref, kkw_ref, ka_ref,
                    hsum_ref, r_ref, v_ref, kk_ref, g_ref, lw0_ref, lw1_ref, kd0_ref, kd1_ref, b0_ref, b1_ref,
                    ext_ref, *, ts):
    i = pl.program_id(1)
    n = pl.num_programs(1)
    halo = SUBLANE
    ext_ref[0:halo, :] = jnp.where(i > 0, prev_ref[...], 0.0)
    ext_ref[halo:halo + ts, :] = cur_ref[...]
    ext_ref[halo + ts:2 * halo + ts, :] = jnp.where(i < n - 1, next_ref[...], 0.0)

    def shifted(cols):
        pf = ext_ref[halo:halo + ts, cols]
        neigh = 0.5 * (ext_ref[halo - 1:halo - 1 + ts, cols] + ext_ref[halo + 1:halo + 1 + ts, cols])
        return pf + mu_ref[:, cols] * (neigh - pf)

    w = RW_WIDTH
    r = shifted(slice(0, w))
    k = shifted(slice(w, 2 * w))
    v = shifted(slice(2 * w, 3 * w))
    lora = shifted(slice(3 * w, 3 * w + 3 * LANE))
    wdec = -_softplus(-(w0_ref[...] + _dot(jnp.tanh(lora[:, 0:LANE]), w2_ref[...]))) - 0.5
    lw = -jnp.exp(wdec)
    a = _sigmoid(a0_ref[...] + _dot(lora[:, LANE:2 * LANE], a2_ref[...]))
    g_ref[...] = _dot(_sigmoid(lora[:, 2 * LANE:3 * LANE]), g2_ref[...])
    kk = k * kkw_ref[...]
    kk = kk * lax.rsqrt(_dot_mask(kk * kk, hsum_ref[...]) + 1e-6)
    r_ref[...] = r
    v_ref[...] = v
    kk_ref[...] = kk
    for d, (lw_ref, kd_ref, b_ref) in enumerate(((lw0_ref, kd0_ref, b0_ref), (lw1_ref, kd1_ref, b1_ref))):
        ad = a[:, d * w:(d + 1) * w]
        lw_ref[...] = lw[:, d * w:(d + 1) * w]
        kd_ref[...] = k * (1.0 + (ad - 1.0) * ka_ref[...])
        b_ref[...] = kk * ad


def _block_diag2(m):
    z = jnp.zeros_like(m[0])
    return jnp.concatenate([jnp.concatenate([m[0], z], 1), jnp.concatenate([z, m[1]], 1)], 0)


def _head_sum_matrix(width, head_dim):
    idx = np.arange(width) // head_dim
    return jnp.asarray(idx[:, None] == idx[None, :], F32)


def _rw_prep(p1, mu, w0, w2, a0, a2, g2, k_k, k_a, ts=256):
    bsz, s, _ = p1.shape
    cin = RW_IN
    w = RW_WIDTH
    nb8 = s // SUBLANE
    r8 = ts // SUBLANE
    full = lambda shape: pl.BlockSpec(shape, lambda b, i: (0,) * len(shape))
    out_spec = pl.BlockSpec((None, ts, w), lambda b, i: (b, i, 0))
    return pl.pallas_call(
        functools.partial(_rw_prep_kernel, ts=ts),
        grid=(bsz, s // ts),
        in_specs=[pl.BlockSpec((None, ts, cin), lambda b, i: (b, i, 0)),
                  pl.BlockSpec((None, SUBLANE, cin), lambda b, i: (b, jnp.maximum(i * r8 - 1, 0), 0)),
                  pl.BlockSpec((None, SUBLANE, cin), lambda b, i: (b, jnp.minimum((i + 1) * r8, nb8 - 1), 0)),
                  full((1, cin)), full((1, 2 * w)), full((LANE, 2 * w)), full((1, 2 * w)), full((LANE, 2 * w)),
                  full((RW_GATE_LORA, w)), full((1, w)), full((1, w)), full((w, w))],
        out_specs=[out_spec] * 10,
        out_shape=[jax.ShapeDtypeStruct((bsz, s, w), F32)] * 10,
        scratch_shapes=[pltpu.VMEM((ts + 2 * SUBLANE, cin), F32)],
        compiler_params=_params("parallel", "parallel"),
        name="rw_prep",
    )(p1, p1, p1, mu.reshape(1, cin), w0.reshape(1, 2 * w), _block_diag2(w2), a0.reshape(1, 2 * w),
      _block_diag2(a2), g2, k_k.reshape(1, w), k_a.reshape(1, w), _head_sum_matrix(w, RW_HEAD_DIM))


def _rw_direction(r_ref, v_ref, kk_ref, lw_ref, kd_ref, b_ref, o_ref, state_ref, *, sc, reverse):
    chunk = RW_CHUNK
    nck = sc // chunk
    hd = RW_HEAD_DIM
    masks = _ChunkMasks(sc, chunk, reverse)
    lw = lw_ref[...]
    lg = _mask_dot(masks.incl, lw)
    lt = _mask_dot(masks.same, lw)
    e_neg = jnp.exp(-lg)
    kkt = kk_ref[...] * jnp.exp(lg - lw)
    khat = kd_ref[...] * e_neg
    bhat = b_ref[...] * e_neg
    rt = r_ref[...] * jnp.exp(lg)
    e_rest = jnp.exp(lt - lg)
    kbar = kd_ref[...] * e_rest
    bbar = b_ref[...] * e_rest
    e_tot = jnp.exp(lt)
    order = range(nck - 1, -1, -1) if reverse else range(nck)
    steps = []
    for h in range(RW_HEADS):
        cols = slice(h * hd, (h + 1) * hd)
        v = v_ref[:, cols]
        kkt_h = kkt[:, cols]
        khat_h = khat[:, cols]
        bhat_h = bhat[:, cols]
        rt_h = rt[:, cols]
        l_a = jnp.where(masks.strict, _dot_nt(kkt_h, khat_h), 0.0)
        a_qb = jnp.where(masks.incl, _dot_nt(rt_h, bhat_h), 0.0)
        steps.append(dict(
            l_b=jnp.where(masks.strict, _dot_nt(kkt_h, bhat_h), 0.0), kkt=kkt_h, la_v=_dot(l_a, v), rt=rt_h,
            y0=_dot(jnp.where(masks.incl, _dot_nt(rt_h, khat_h), 0.0), v),
            a_qb=[a_qb[c * chunk:(c + 1) * chunk, c * chunk:(c + 1) * chunk] for c in range(nck)],
            v=v, kbar=kbar[:, cols], bbar=bbar[:, cols], gl=e_tot[:, cols],
            cols=cols, o_ref=o_ref, state_ref=state_ref, h=h, order=order))
    for st, t in zip(steps, _unit_tri_inverses([st.pop('l_b') for st in steps], masks)):
        st['wm'] = _dot(t, st.pop('kkt'))
        st['u'] = _dot(t, st.pop('la_v'))
    return steps


def _rw_recurrence(steps, chunk):
    states = [st['state_ref'][st['h']] for st in steps]
    for pos in range(len(steps[0]['order'])):
        for n, st in enumerate(steps):
            c = st['order'][pos]
            rows = slice(c * chunk, (c + 1) * chunk)
            p = _dot_nt(st['wm'][rows], states[n]) + st['u'][rows]
            st['o_ref'][rows, st['cols']] = (st['y0'][rows] + _dot_nt(st['rt'][rows], states[n])
                                             - _dot(st['a_qb'][c], p))
            states[n] = (states[n] * st['gl'][c * chunk:c * chunk + 1] + _dot_tn(st['v'][rows], st['kbar'][rows])
                         - _dot_tn(p, st['bbar'][rows]))
    for n, st in enumerate(steps):
        st['state_ref'][st['h']] = states[n]


def _rw_chunk_kernel(rf, vf, kkf, lwf, kdf, bf, rr, vr, kkr, lwr, kdr, br, of_ref, or_ref, sf_ref, sr_ref, *, sc):
    @pl.when(pl.program_id(1) == 0)
    def _():
        sf_ref[...] = jnp.zeros_like(sf_ref)
        sr_ref[...] = jnp.zeros_like(sr_ref)

    steps = (_rw_direction(rf, vf, kkf, lwf, kdf, bf, of_ref, sf_ref, sc=sc, reverse=False)
             + _rw_direction(rr, vr, kkr, lwr, kdr, br, or_ref, sr_ref, sc=sc, reverse=True))
    _rw_recurrence(steps, RW_CHUNK)


def _rw_chunked(r, v, kk, lw0, lw1, kd0, kd1, b0, b1, sc=256):
    bsz, s, w = r.shape
    n = s // sc
    fwd = pl.BlockSpec((None, sc, w), lambda b, i: (b, i, 0))
    rev = pl.BlockSpec((None, sc, w), lambda b, i: (b, n - 1 - i, 0))
    state = pltpu.VMEM((RW_HEADS, RW_HEAD_DIM, RW_HEAD_DIM), F32)
    return pl.pallas_call(
        functools.partial(_rw_chunk_kernel, sc=sc),
        grid=(bsz, n),
        in_specs=[fwd] * 6 + [rev] * 6,
        out_specs=[fwd, rev],
        out_shape=[jax.ShapeDtypeStruct((bsz, s, w), F32)] * 2,
        scratch_shapes=[state, state],
        compiler_params=_params("parallel", "arbitrary"),
        name="rw_chunked",
    )(r, v, kk, lw0, kd0, b0, r, v, kk, lw1, kd1, b1)


def _rw_post_kernel(yf_ref, yr_ref, r_ref, v_ref, g_ref, kd0_ref, kd1_ref, rk_ref, lnw_ref, lnb_ref, hsum_ref, o_ref):
    inv = 1.0 / RW_HEAD_DIM
    y = yf_ref[...] + yr_ref[...]
    mean = _dot_mask(y, hsum_ref[...]) * inv
    cen = y - mean
    var = _dot_mask(cen * cen, hsum_ref[...]) * inv
    y = cen * lax.rsqrt(var + RW_GN_EPS) * lnw_ref[...] + lnb_ref[...]
    k_bonus = 0.5 * (kd0_ref[...] + kd1_ref[...])
    y = y + _dot_mask(r_ref[...] * k_bonus * rk_ref[...], hsum_ref[...]) * v_ref[...]
    o_ref[...] = y * g_ref[...]


def _rw_post(y_f, y_r, r, v, g, kd0, kd1, r_k, ln_w, ln_b, ts=512):
    bsz, s, w = r.shape
    spec = pl.BlockSpec((None, ts, w), lambda b, i: (b, i, 0))
    row = pl.BlockSpec((1, w), lambda b, i: (0, 0))
    return pl.pallas_call(
        _rw_post_kernel,
        grid=(bsz, s // ts),
        in_specs=[spec] * 7 + [row, row, row, pl.BlockSpec((w, w), lambda b, i: (0, 0))],
        out_specs=spec,
        out_shape=jax.ShapeDtypeStruct((bsz, s, w), F32),
        compiler_params=_params("parallel", "parallel"),
        name="rw_post",
    )(y_f, y_r, r, v, g, kd0, kd1, r_k.reshape(1, w), ln_w.reshape(1, w), ln_b.reshape(1, w),
      _head_sum_matrix(w, RW_HEAD_DIM))


def _rwkv7_mixer(p1, mu, w0, w2, a0, a2, g2, k_k, k_a, r_k, ln_w, ln_b):
    r, v, kk, g, lw0, lw1, kd0, kd1, b0, b1 = _rw_prep(p1, mu, w0, w2, a0, a2, g2, k_k, k_a)
    y_f, y_r = _rw_chunked(r, v, kk, lw0, lw1, kd0, kd1, b0, b1)
    return _rw_post(y_f, y_r, r, v, g, kd0, kd1, r_k, ln_w, ln_b)


def _diff_attn_kernel(q_ref, k_ref, v_ref, lam_ref, sub_ref, o_ref, *, tq, lambda_init):
    h = pl.program_id(1)
    i = pl.program_id(2)
    s = k_ref.shape[0]
    hd = DF_HEAD_DIM
    slope = jnp.where(h == 0, 2.0 ** -2, jnp.where(h == 1, 2.0 ** -4, jnp.where(h == 2, 2.0 ** -6, 2.0 ** -8)))
    lp = lam_ref[...]
    lam = (jnp.exp(jnp.sum(lp[0:1] * lp[1:2], axis=1, keepdims=True))
           - jnp.exp(jnp.sum(lp[2:3] * lp[3:4], axis=1, keepdims=True)) + lambda_init)
    qpos = (i * tq + lax.broadcasted_iota(jnp.int32, (tq, 1), 0)).astype(F32)
    kpos = lax.broadcasted_iota(jnp.int32, (1, s), 1).astype(F32)
    bias = (slope.astype(F32) * LOG2_E) * jnp.abs(qpos - kpos)
    v = v_ref[...].astype(BF16)
    outs = []
    for m in range(2):
        q = q_ref[:, m * hd:(m + 1) * hd] * (hd ** -0.5 * LOG2_E)
        sc = _dot_nt(q, k_ref[:, m * hd:(m + 1) * hd]) - bias
        ex = jnp.exp2(sc - jnp.max(sc, axis=1, keepdims=True))
        outs.append(_dot(ex, v) / jnp.sum(ex, axis=1, keepdims=True))
    o = outs[0] - lam * outs[1]
    o_ref[...] = _rms(o, 1e-5) * sub_ref[...] * (1.0 - lambda_init)


def _diff_attention(p1, lam_params, subln_w, lambda_init, tq=256):
    bsz, s, _ = p1.shape
    hw = 2 * DF_HEAD_DIM
    q0 = RW_IN // hw
    return pl.pallas_call(
        functools.partial(_diff_attn_kernel, tq=tq, lambda_init=lambda_init),
        grid=(bsz, DF_HEADS, s // tq),
        in_specs=[pl.BlockSpec((None, tq, hw), lambda b, h, i: (b, i, q0 + h)),
                  pl.BlockSpec((None, s, hw), lambda b, h, i: (b, 0, q0 + DF_HEADS + h)),
                  pl.BlockSpec((None, s, hw), lambda b, h, i: (b, 0, q0 + 2 * DF_HEADS + h)),
                  pl.BlockSpec((4, DF_HEAD_DIM), lambda b, h, i: (0, 0)),
                  pl.BlockSpec((1, hw), lambda b, h, i: (0, 0))],
        out_specs=pl.BlockSpec((None, tq, hw), lambda b, h, i: (b, i, h)),
        out_shape=jax.ShapeDtypeStruct((bsz, s, DF_HEADS * hw), F32),
        compiler_params=_params("parallel", "parallel", "parallel"),
        name="diff_attn",
    )(p1, p1, p1, lam_params, subln_w.reshape(1, hw))


def _layer0_weights(ab_w_in):
    d = ab_w_in.shape[0]
    main = 4 * DN_WIDTH
    gates = 4 * DN_HEADS
    w = jnp.concatenate([ab_w_in[:, :main], ab_w_in[:, main + gates:], ab_w_in[:, main:main + gates],
                         jnp.zeros((d, LANE - gates), ab_w_in.dtype)], axis=1)
    return w.astype(BF16)


def kernel(x, c, ada_w, ada_b, norm_mix_pre, norm_mix_post, norm_ffn_pre, norm_ffn_post, router_w, exp_w_gate, exp_w_up, exp_w_down, ab_w_in, ab_w_out, dn_conv, dn_a_log, dn_dt_bias, dn_norm, s5_a_re, s5_a_im, s5_log_dt, s5_b_re, s5_b_im, s5_c_re, s5_c_im, s5_d, s5_glu_w, s5_glu_b, cd_w_in, cd_w_out, rw_mu, rw_w0, rw_w2, rw_a0, rw_a2, rw_g2, rw_k_k, rw_k_a, rw_r_k, rw_ln_w, rw_ln_b, df_lambda, df_subln):
    depth = ada_w.shape[0]
    mod_all = _ada_mod(c, ada_w, ada_b)
    for layer in range(depth):
        mod = mod_all[layer][:, None, :]
        i = layer // 2
        if layer % 2 == 0:
            p = _in_proj(x, norm_mix_pre[layer], mod, _layer0_weights(ab_w_in[i]))
            ya = _gated_deltanet(p, dn_conv[i], dn_a_log[i], dn_dt_bias[i], dn_norm[i])
            yb = _s5_mixer(p, s5_a_re[i], s5_a_im[i], s5_log_dt[i], s5_b_re[i], s5_b_im[i], s5_c_re[i], s5_c_im[i],
                           s5_d[i], s5_glu_w[i], s5_glu_b[i])
            w_out = ab_w_out[i]
        else:
            lambda_init = 0.8 - 0.6 * math.exp(-0.3 * layer)
            p = _in_proj(x, norm_mix_pre[layer], mod, cd_w_in[i].astype(BF16))
            ya = _rwkv7_mixer(p, rw_mu[i], rw_w0[i], rw_w2[i], rw_a0[i], rw_a2[i], rw_g2[i], rw_k_k[i], rw_k_a[i],
                              rw_r_k[i].reshape(-1), rw_ln_w[i], rw_ln_b[i])
            yb = _diff_attention(p, df_lambda[i], df_subln[i], lambda_init)
            w_out = cd_w_out[i]
        x1, h2, pt = _mix_out(ya, yb, x, w_out.astype(BF16), norm_mix_post[layer], mod, norm_ffn_pre[layer],
                              router_w[layer])
        x = _expert_choice_ffn(x1, h2, pt, mod, norm_ffn_post[layer], exp_w_gate, exp_w_up, exp_w_down, layer)
    return x
```
